```python
import numpy as np
import jax
import jax.numpy as jnp
from jax import lax

D_MODEL = 1024
BATCH = 4
SEQ = 4096
DEPTH = 2

GRID_W = 64
CTX_LEN = 256
RMS_EPS = 1e-6

SSD_HEAD_DIM = 64
SSD_W = D_MODEL
SSD_HEADS = SSD_W // SSD_HEAD_DIM
SSD_GROUPS = 2
SSD_STATE = 128
SSD_CONV = 3
SSD_CHUNK = 128
XBC_W = SSD_W + 2 * SSD_GROUPS * SSD_STATE

ML_HEADS = 4
ML_V_W = D_MODEL
ML_DV = ML_V_W // ML_HEADS
ML_DK = ML_DV // 2
ML_CHUNK = 64

MIX_W = SSD_W + ML_V_W
IN_SIZES = (SSD_W, XBC_W, 2 * SSD_HEADS, ML_HEADS * ML_DK, ML_HEADS * ML_DK, ML_V_W, ML_V_W, 2 * ML_HEADS, 2 * ML_HEADS)
IN_W = sum(IN_SIZES)
IN_SPLITS = [int(v) for v in np.cumsum(IN_SIZES)[:-1]]

NA_HEAD_DIM = 64
NA_HEADS = D_MODEL // NA_HEAD_DIM
NA_W = NA_HEADS * NA_HEAD_DIM
NA_KH = 8
NA_KW = 16

MOE_GROUPS = 4
MOE_EXPERTS = 8
MOE_TOP_K = 2
MOE_FF = D_MODEL // 2

kernel_name = 'hybrid_ssd_mlstm_natten_hmoe_prefix'


def _rms_norm(x, w):
    xf = x.astype(jnp.float32)
    y = xf * lax.rsqrt(jnp.mean(xf * xf, axis=-1, keepdims=True) + RMS_EPS)
    return (y * w).astype(x.dtype)


def _modulate(h, shift, scale):
    return h * (1 + scale) + shift


def _centred_dwconv(u, w, b):
    ch = u.shape[-1]
    pad = w.shape[0] // 2
    y = lax.conv_general_dilated(u, w[:, None, :], window_strides=(1,), padding=[(pad, pad)],
                                 dimension_numbers=('NWC', 'WIO', 'NWC'), feature_group_count=ch)
    return y + b


def _ssd_chunked(xdt, log_a, b_in, c_in, state0):
    bsz, L, H, P = xdt.shape
    G, N = b_in.shape[2], b_in.shape[3]
    E = H // G
    nc = L // SSD_CHUNK
    xc = xdt.reshape(bsz, nc, SSD_CHUNK, G, E, P)
    bc = b_in.reshape(bsz, nc, SSD_CHUNK, G, N)
    cc = c_in.reshape(bsz, nc, SSD_CHUNK, G, N)
    ac = log_a.astype(jnp.float32).reshape(bsz, nc, SSD_CHUNK, G, E).transpose(0, 1, 3, 4, 2)
    a_cs = jnp.cumsum(ac, axis=-1)
    causal = jnp.tril(jnp.ones((SSD_CHUNK, SSD_CHUNK), dtype=bool))
    decay = jnp.exp(jnp.where(causal, a_cs[..., :, None] - a_cs[..., None, :], -jnp.inf))
    cb = jnp.einsum('bclgn,bcsgn->bcgls', cc, bc)
    y_diag = jnp.einsum('bcgels,bcsgep->bclgep', cb[:, :, :, None] * decay, xc)
    decay_to_end = jnp.exp(a_cs[..., -1:] - a_cs)
    chunk_states = jnp.einsum('bclgn,bcgel,bclgep->bcgepn', bc, decay_to_end, xc)
    chunk_decay = jnp.exp(a_cs[..., -1])

    def step(s, inp):
        cs, cd = inp
        return s * cd[..., None, None] + cs, s

    s0 = state0.reshape(bsz, G, E, P, N)
    final, prev = lax.scan(step, s0, (jnp.moveaxis(chunk_states, 1, 0), jnp.moveaxis(chunk_decay, 1, 0)))
    prev = jnp.moveaxis(prev, 0, 1)
    y_off = jnp.einsum('bclgn,bcgepn,bcgel->bclgep', cc, prev, jnp.exp(a_cs))
    y = (y_diag + y_off).reshape(bsz, L, H, P)
    return y, final.reshape(bsz, H, P, N)


def _mlstm_chunked(q, k, v, log_i, log_f, state0):
    bsz, L, H, Dk = q.shape
    Dv = v.shape[-1]
    nc = L // ML_CHUNK
    qc = q.reshape(bsz, nc, ML_CHUNK, H, Dk)
    kc = k.reshape(bsz, nc, ML_CHUNK, H, Dk)
    vc = v.reshape(bsz, nc, ML_CHUNK, H, Dv)
    li = log_i.astype(jnp.float32).reshape(bsz, nc, ML_CHUNK, H).transpose(0, 1, 3, 2)
    lf = log_f.astype(jnp.float32).reshape(bsz, nc, ML_CHUNK, H).transpose(0, 1, 3, 2)
    bcum = jnp.cumsum(lf, axis=-1)
    causal = jnp.tril(jnp.ones((ML_CHUNK, ML_CHUNK), dtype=bool))
    dmat = jnp.where(causal, bcum[..., :, None] - bcum[..., None, :] + li[..., None, :], -jnp.inf)
    m_intra = jnp.max(dmat, axis=-1)
    a_end = bcum[..., -1:] - bcum + li
    m_loc = jnp.max(a_end, axis=-1)
    w_end = jnp.exp(a_end - m_loc[..., None])
    c_chunk = jnp.einsum('bchl,bclhk,bclhe->bchke', w_end, kc, vc)
    n_chunk = jnp.einsum('bchl,bclhk->bchk', w_end, kc)
    g_chunk = bcum[..., -1]

    def step(carry, inp):
        C, n, m = carry
        cc_, nc_, ml, gg = inp
        m_new = jnp.maximum(gg + m, ml)
        a = jnp.exp(gg + m - m_new)
        bb = jnp.exp(ml - m_new)
        new = (C * a[..., None, None] + cc_ * bb[..., None, None], n * a[..., None] + nc_ * bb[..., None], m_new)
        return new, (C, n, m)

    final, (c_prev, n_prev, m_prev) = lax.scan(
        step, state0, (jnp.moveaxis(c_chunk, 1, 0), jnp.moveaxis(n_chunk, 1, 0),
                       jnp.moveaxis(m_loc, 1, 0), jnp.moveaxis(g_chunk, 1, 0)))
    c_prev = jnp.moveaxis(c_prev, 0, 1)
    n_prev = jnp.moveaxis(n_prev, 0, 1)
    m_prev = jnp.moveaxis(m_prev, 0, 1)
    m_inter = bcum + m_prev[..., None]
    m_t = jnp.maximum(m_intra, m_inter)
    s = jnp.einsum('bclhk,bcshk->bchls', qc, kc) * jnp.exp(dmat - m_t[..., None])
    w_inter = jnp.exp(m_inter - m_t)
    num = (jnp.einsum('bchls,bcshe->bclhe', s, vc)
           + jnp.einsum('bclhk,bchke->bclhe', qc, c_prev) * w_inter.transpose(0, 1, 3, 2)[..., None])
    den = jnp.sum(s, axis=-1) + jnp.einsum('bclhk,bchk->bchl', qc, n_prev) * w_inter
    den = jnp.maximum(jnp.abs(den), jnp.exp(-m_t))
    h = num / den.transpose(0, 1, 3, 2)[..., None]
    return h.reshape(bsz, L, H, Dv), final


def _ab_streams(h, w_in, conv_w, conv_b, dt_bias):
    bsz, L, _ = h.shape
    z, xbc, dt_raw, q, k, v, o, i_raw, f_raw = jnp.split(h @ w_in, IN_SPLITS, axis=-1)
    xbc = jax.nn.silu(_centred_dwconv(xbc, conv_w, conv_b))
    xs, b_ssm, c_ssm = jnp.split(xbc, [SSD_W, SSD_W + SSD_GROUPS * SSD_STATE], axis=-1)
    dt = jax.nn.softplus(dt_raw.reshape(bsz, L, 2, SSD_HEADS).astype(jnp.float32) + dt_bias.astype(jnp.float32))
    return {
        'z': z,
        'xs': xs.reshape(bsz, L, SSD_HEADS, SSD_HEAD_DIM),
        'b': b_ssm.reshape(bsz, L, SSD_GROUPS, SSD_STATE),
        'c': c_ssm.reshape(bsz, L, SSD_GROUPS, SSD_STATE),
        'dt': dt,
        'q': q.reshape(bsz, L, ML_HEADS, ML_DK),
        'k': k.reshape(bsz, L, ML_HEADS, ML_DK) * ML_DK ** -0.5,
        'v': v.reshape(bsz, L, ML_HEADS, ML_DV),
        'o': o,
        'i': i_raw.reshape(bsz, L, 2, ML_HEADS).astype(jnp.float32),
        'f': f_raw.reshape(bsz, L, 2, ML_HEADS).astype(jnp.float32),
    }


def _ssd_direction(s, d, a, state0):
    dt = s['dt']
    xdt = s['xs'] * dt[:, :, d, :, None]
    la = dt[:, :, d] * a[d]
    bm, cm = s['b'], s['c']
    if d == 1:
        xdt, la, bm, cm = (jnp.flip(t, axis=1) for t in (xdt, la, bm, cm))
    y, fin = _ssd_chunked(xdt, la, bm, cm, state0)
    if d == 1:
        y = jnp.flip(y, axis=1)
    return y, fin


def _mlstm_direction(s, d, i_bias, f_bias, state0):
    log_i = s['i'][:, :, d] + i_bias[d].astype(jnp.float32)
    log_f = jax.nn.log_sigmoid(s['f'][:, :, d] + f_bias[d].astype(jnp.float32))
    q, k, v = s['q'], s['k'], s['v']
    if d == 1:
        q, k, v, log_i, log_f = (jnp.flip(t, axis=1) for t in (q, k, v, log_i, log_f))
    h, fin = _mlstm_chunked(q, k, v, log_i, log_f, state0)
    if d == 1:
        h = jnp.flip(h, axis=1)
    return h, fin


def _ab_output(s, y_ssd, h_ml, d_skip, ssd_norm_w, ml_norm_w, w_out):
    bsz, L = y_ssd.shape[:2]
    dtype = s['z'].dtype
    y = (y_ssd + d_skip[:, None] * s['xs']).reshape(bsz, L, SSD_W) * jax.nn.silu(s['z'])
    y = _rms_norm(y, ssd_norm_w)
    hm = _rms_norm(h_ml, ml_norm_w).reshape(bsz, L, ML_V_W) * jax.nn.sigmoid(s['o'])
    return jnp.concatenate([y.astype(dtype), hm.astype(dtype)], axis=-1) @ w_out


def _ssd_mlstm_mixer(h_lat, h_ctx, w_in, conv_w, conv_b, a_log, dt_bias, d_skip, ssd_norm_w,
                     ml_i_bias, ml_f_bias, ml_norm_w, w_out, need_ctx):
    s_ctx = _ab_streams(h_ctx, w_in, conv_w, conv_b, dt_bias)
    s_lat = _ab_streams(h_lat, w_in, conv_w, conv_b, dt_bias)
    a = -jnp.exp(a_log.astype(jnp.float32))
    bsz = h_lat.shape[0]
    ssd0 = jnp.zeros((bsz, SSD_HEADS, SSD_HEAD_DIM, SSD_STATE), jnp.float32)
    ml0 = (jnp.zeros((bsz, ML_HEADS, ML_DK, ML_DV), jnp.float32),
           jnp.zeros((bsz, ML_HEADS, ML_DK), jnp.float32),
           jnp.zeros((bsz, ML_HEADS), jnp.float32))
    y_c, y_l, m_c, m_l = [], [], [], []
    for d in range(2):
        yc, sc = _ssd_direction(s_ctx, d, a, ssd0)
        yl, _ = _ssd_direction(s_lat, d, a, sc)
        hc, mc = _mlstm_direction(s_ctx, d, ml_i_bias, ml_f_bias, ml0)
        hl, _ = _mlstm_direction(s_lat, d, ml_i_bias, ml_f_bias, mc)
        y_c.append(yc); y_l.append(yl); m_c.append(hc); m_l.append(hl)
    o_lat = _ab_output(s_lat, y_l[0] + y_l[1], m_l[0] + m_l[1], d_skip, ssd_norm_w, ml_norm_w, w_out)
    o_ctx = None
    if need_ctx:
        o_ctx = _ab_output(s_ctx, y_c[0] + y_c[1], m_c[0] + m_c[1], d_skip, ssd_norm_w, ml_norm_w, w_out)
    return o_lat, o_ctx


def _na_qkv(h, w_qkv, q_norm, k_norm):
    q, k, v = jnp.split(h @ w_qkv, 3, axis=-1)
    shp = h.shape[:-1] + (NA_HEADS, NA_HEAD_DIM)
    q = _rms_norm(q.reshape(shp), q_norm) * NA_HEAD_DIM ** -0.5
    k = _rms_norm(k.reshape(shp), k_norm)
    return q, k, v.reshape(shp)


def _neighbourhood_attention(h_lat, h_ctx, w_qkv, q_norm, k_norm, rpb, w_out, need_ctx):
    bsz, seq, _ = h_lat.shape
    dtype = h_lat.dtype
    rows = seq // GRID_W
    kh = min(NA_KH, rows)
    q_l, k_l, v_l = _na_qkv(h_lat, w_qkv, q_norm, k_norm)
    q_c, k_c, v_c = _na_qkv(h_ctx, w_qkv, q_norm, k_norm)
    grid = (bsz, rows, GRID_W, NA_HEADS, NA_HEAD_DIM)
    q_g, k_g, v_g = q_l.reshape(grid), k_l.reshape(grid), v_l.reshape(grid)
    cols = np.arange(GRID_W)
    col_start = np.clip(cols - NA_KW // 2, 0, GRID_W - NA_KW)
    col_idx = col_start[:, None] + np.arange(NA_KW)[None, :]
    col_off = col_idx - cols[:, None] + (NA_KW - 1)
    n_loc = kh * NA_KW

    def row_block(r):
        r0 = jnp.clip(r - kh // 2, 0, rows - kh)
        kr = lax.dynamic_slice_in_dim(k_g, r0, kh, axis=1)
        vr = lax.dynamic_slice_in_dim(v_g, r0, kh, axis=1)
        kw_ = kr[:, :, col_idx]
        vw_ = vr[:, :, col_idx]
        qr = lax.dynamic_index_in_dim(q_g, r, axis=1, keepdims=False)
        s_loc = jnp.einsum('bchd,bicjhd->bhcij', qr, kw_).astype(jnp.float32)
        row_off = r0 + jnp.arange(kh) - r + (NA_KH - 1)
        bias = rpb[:, row_off[:, None, None], col_off[None, :, :]].astype(jnp.float32)
        s_loc = s_loc + bias.transpose(0, 2, 1, 3)[None]
        s_ctx = jnp.einsum('bchd,bmhd->bhcm', qr, k_c).astype(jnp.float32)
        s_all = jnp.concatenate([s_loc.reshape(bsz, NA_HEADS, GRID_W, n_loc), s_ctx], axis=-1)
        p = jax.nn.softmax(s_all, axis=-1)
        p_loc = p[..., :n_loc].reshape(bsz, NA_HEADS, GRID_W, kh, NA_KW)
        o = (jnp.einsum('bhcij,bicjhd->bchd', p_loc, vw_)
             + jnp.einsum('bhcm,bmhd->bchd', p[..., n_loc:], v_c))
        return o.astype(dtype)

    o = lax.map(row_block, jnp.arange(rows))
    o_lat = jnp.moveaxis(o, 0, 1).reshape(bsz, seq, NA_W) @ w_out
    o_ctx = None
    if need_ctx:
        p = jax.nn.softmax(jnp.einsum('bqhd,bkhd->bhqk', q_c, k_c).astype(jnp.float32), axis=-1)
        oc = jnp.einsum('bhqk,bkhd->bqhd', p, v_c).astype(dtype)
        o_ctx = oc.reshape(bsz, h_ctx.shape[1], NA_W) @ w_out
    return o_lat, o_ctx


def _hier_moe(h, w_group, b_group, w_expert, b_expert, w_gate, w_up, w_down):
    t = h.shape[0]
    hf = h.astype(jnp.float32)
    p_group = jax.nn.softmax(hf @ w_group.astype(jnp.float32) + b_group.astype(jnp.float32), axis=-1)
    g_prob, g_idx = lax.top_k(p_group, 1)
    e_logits = (hf @ w_expert.astype(jnp.float32) + b_expert.astype(jnp.float32)).reshape(t, MOE_GROUPS, MOE_EXPERTS)
    e_logits = jnp.take_along_axis(e_logits, g_idx[:, :, None], axis=1)[:, 0]
    e_prob, e_idx = lax.top_k(jax.nn.softmax(e_logits, axis=-1), MOE_TOP_K)
    e_w = g_prob * e_prob / jnp.sum(e_prob, axis=-1, keepdims=True)
    w_in_group = jnp.einsum('tk,tke->te', e_w, jax.nn.one_hot(e_idx, MOE_EXPERTS, dtype=jnp.float32))
    combine = jax.nn.one_hot(g_idx[:, 0], MOE_GROUPS, dtype=jnp.float32)[:, :, None] * w_in_group[:, None, :]
    combine = combine.astype(h.dtype)
    out = jnp.zeros_like(h)
    for g in range(MOE_GROUPS):
        a = jnp.einsum('td,edf->tef', h, w_gate[g])
        u = jnp.einsum('td,edf->tef', h, w_up[g])
        act = jax.nn.silu(a) * u * combine[:, g, :, None]
        out = out + jnp.einsum('tef,efd->td', act, w_down[g])
    return out


def setup_inputs(seed: int = 0) -> dict:
    key = jax.random.key(seed)
    keys = iter(jax.random.split(key, 48))
    n_even = (DEPTH + 1) // 2
    n_odd = DEPTH // 2
    f32 = jnp.float32

    def nrm(shape, scale):
        return scale * jax.random.normal(next(keys), shape, f32)

    def gain(shape):
        return 1.0 + nrm(shape, 0.02)

    dt0 = jnp.exp(jax.random.uniform(next(keys), (n_even, 2, SSD_HEADS), f32, float(np.log(1e-3)), float(np.log(1e-1))))
    inp = {
        'x': nrm((BATCH, SEQ, D_MODEL), 1.0),
        'c': nrm((BATCH, D_MODEL), 1.0),
        'ctx': nrm((BATCH, CTX_LEN, D_MODEL), 1.0),
        'c_ctx': nrm((D_MODEL,), 1.0),
        'norm1_w': gain((DEPTH, D_MODEL)),
        'norm2_w': gain((DEPTH, D_MODEL)),
        'mod_w': nrm((DEPTH, D_MODEL, 6 * D_MODEL), D_MODEL ** -0.5),
        'mod_b': nrm((DEPTH, 6 * D_MODEL), 0.02),
        'ab_w_in': nrm((n_even, D_MODEL, IN_W), D_MODEL ** -0.5),
        'ab_conv_w': nrm((n_even, SSD_CONV, XBC_W), SSD_CONV ** -0.5),
        'ab_conv_b': nrm((n_even, XBC_W), 0.02),
        'ssd_a_log': jnp.log(jax.random.uniform(next(keys), (n_even, 2, SSD_HEADS), f32, 1.0, 16.0)),
        'ssd_dt_bias': dt0 + jnp.log(-jnp.expm1(-dt0)),
        'ssd_d': 1.0 + nrm((n_even, SSD_HEADS), 0.1),
        'ssd_norm_w': gain((n_even, SSD_W)),
        'ml_i_bias': nrm((n_even, 2, ML_HEADS), 0.1),
        'ml_f_bias': jnp.linspace(3.0, 6.0, ML_HEADS, dtype=f32) + nrm((n_even, 2, ML_HEADS), 0.1),
        'ml_norm_w': gain((n_even, ML_HEADS, ML_DV)),
        'ab_w_out': nrm((n_even, MIX_W, D_MODEL), MIX_W ** -0.5),
        'na_w_qkv': nrm((n_odd, D_MODEL, 3 * NA_W), D_MODEL ** -0.5),
        'na_q_norm': gain((n_odd, NA_HEAD_DIM)),
        'na_k_norm': gain((n_odd, NA_HEAD_DIM)),
        'na_rpb': nrm((n_odd, NA_HEADS, 2 * NA_KH - 1, 2 * NA_KW - 1), 0.1),
        'na_w_out': nrm((n_odd, NA_W, D_MODEL), NA_W ** -0.5),
        'moe_w_group': nrm((DEPTH, D_MODEL, MOE_GROUPS), D_MODEL ** -0.5),
        'moe_b_group': nrm((DEPTH, MOE_GROUPS), 0.01),
        'moe_w_expert': nrm((DEPTH, D_MODEL, MOE_GROUPS * MOE_EXPERTS), D_MODEL ** -0.5),
        'moe_b_expert': nrm((DEPTH, MOE_GROUPS * MOE_EXPERTS), 0.01),
        'moe_w_gate': nrm((DEPTH, MOE_GROUPS, MOE_EXPERTS, D_MODEL, MOE_FF), D_MODEL ** -0.5),
        'moe_w_up': nrm((DEPTH, MOE_GROUPS, MOE_EXPERTS, D_MODEL, MOE_FF), D_MODEL ** -0.5),
        'moe_w_down': nrm((DEPTH, MOE_GROUPS, MOE_EXPERTS, MOE_FF, D_MODEL), MOE_FF ** -0.5),
    }
    return inp


def reference(x, c, ctx, c_ctx, norm1_w, norm2_w, mod_w, mod_b,
              ab_w_in, ab_conv_w, ab_conv_b, ssd_a_log, ssd_dt_bias, ssd_d, ssd_norm_w,
              ml_i_bias, ml_f_bias, ml_norm_w, ab_w_out,
              na_w_qkv, na_q_norm, na_k_norm, na_rpb, na_w_out,
              moe_w_group, moe_b_group, moe_w_expert, moe_b_expert, moe_w_gate, moe_w_up, moe_w_down):
    bsz, seq, dm = x.shape
    n_lat = bsz * seq
    cond = jax.nn.silu(c)
    cond_ctx = jax.nn.silu(c_ctx)
    x_lat, x_ctx = x, ctx
    for layer in range(DEPTH):
        last = layer == DEPTH - 1
        j = layer // 2
        mod = jnp.split(cond @ mod_w[layer] + mod_b[layer], 6, axis=-1)
        mod_c = jnp.split(cond_ctx @ mod_w[layer] + mod_b[layer], 6, axis=-1)
        h_lat = _modulate(_rms_norm(x_lat, norm1_w[layer]), mod[0][:, None], mod[1][:, None])
        h_ctx = _modulate(_rms_norm(x_ctx, norm1_w[layer]), mod_c[0], mod_c[1])
        if layer % 2 == 0:
            o_lat, o_ctx = _ssd_mlstm_mixer(h_lat, h_ctx, ab_w_in[j], ab_conv_w[j], ab_conv_b[j], ssd_a_log[j],
                                            ssd_dt_bias[j], ssd_d[j], ssd_norm_w[j], ml_i_bias[j], ml_f_bias[j],
                                            ml_norm_w[j], ab_w_out[j], not last)
        else:
            o_lat, o_ctx = _neighbourhood_attention(h_lat, h_ctx, na_w_qkv[j], na_q_norm[j], na_k_norm[j],
                                                    na_rpb[j], na_w_out[j], not last)
        x_lat = x_lat + mod[2][:, None] * o_lat
        h_lat = _modulate(_rms_norm(x_lat, norm2_w[layer]), mod[3][:, None], mod[4][:, None]).reshape(n_lat, dm)
        moe_args = (moe_w_group[layer], moe_b_group[layer], moe_w_expert[layer], moe_b_expert[layer],
                    moe_w_gate[layer], moe_w_up[layer], moe_w_down[layer])
        if last:
            f_lat = _hier_moe(h_lat, *moe_args)
        else:
            x_ctx = x_ctx + mod_c[2] * o_ctx
            h_ctx = _modulate(_rms_norm(x_ctx, norm2_w[layer]), mod_c[3], mod_c[4]).reshape(-1, dm)
            f_all = _hier_moe(jnp.concatenate([h_lat, h_ctx], axis=0), *moe_args)
            f_lat = f_all[:n_lat]
            x_ctx = x_ctx + mod_c[5] * f_all[n_lat:].reshape(x_ctx.shape)
        x_lat = x_lat + mod[5][:, None] * f_lat.reshape(x_lat.shape)
    return x_lat
```

```python
import functools

import numpy as np
import jax
import jax.numpy as jnp
from jax import lax
from jax.experimental import pallas as pl
from jax.experimental.pallas import tpu as pltpu

F32 = jnp.float32
BF16 = jnp.bfloat16
HIGHEST = lax.Precision.HIGHEST

RMS_EPS = 1e-6
GRID_W = 64
SSD_HEADS = 16
SSD_HEAD_DIM = 64
SSD_GROUPS = 2
SSD_STATE = 128
ML_HEADS = 4
ML_DK = 128
ML_DV = 256
NA_HEADS = 16
NA_HEAD_DIM = 64
NA_KH = 8
NA_KW = 16
MOE_GROUPS = 4
MOE_EXPERTS = 8
N_EXPERTS = MOE_GROUPS * MOE_EXPERTS

LANES = 128
ROW_TILE = 256
SCAN_CHUNK = 128
MOE_TILE = 256
VMEM_LIMIT = 56 * 1024 * 1024

GATE_DT = 0
GATE_I = 2 * SSD_HEADS
GATE_F = GATE_I + 2 * ML_HEADS
ROUTE_G = N_EXPERTS
NEG = -1e30


def _params(n_axes):
    return pltpu.CompilerParams(dimension_semantics=("arbitrary",) * n_axes,
                                vmem_limit_bytes=VMEM_LIMIT)


def _silu(x):
    return x * jax.nn.sigmoid(x)


def _softplus(x):
    return jnp.maximum(x, 0.0) + jnp.log1p(jnp.exp(-jnp.abs(x)))


def _rms(x, w):
    return x * lax.rsqrt(jnp.mean(x * x, axis=-1, keepdims=True) + RMS_EPS) * w


def _dot(a, b):
    return jnp.dot(a, b, preferred_element_type=F32)


def _dot_nt(a, b):
    return lax.dot_general(a, b, (((1,), (1,)), ((), ())), preferred_element_type=F32)


def _dot_hi(a, b):
    return jnp.dot(a, b, precision=HIGHEST, preferred_element_type=F32)


def _mod_kernel(c_ref, w_ref, b_ref, o_ref):
    o_ref[0] = _dot_hi(_silu(c_ref[...]), w_ref[0]) + b_ref[0]


def _mod_vectors(cvec, mod_w, mod_b, tn=512):
    depth, d, n = mod_w.shape
    rows = cvec.shape[0]
    out = pl.pallas_call(
        _mod_kernel,
        out_shape=jax.ShapeDtypeStruct((depth, rows, n), F32),
        grid=(depth, n // tn),
        in_specs=[pl.BlockSpec((rows, d), lambda l, j: (0, 0)),
                  pl.BlockSpec((1, d, tn), lambda l, j: (l, 0, j)),
                  pl.BlockSpec((1, 1, tn), lambda l, j: (l, 0, j))],
        out_specs=pl.BlockSpec((1, rows, tn), lambda l, j: (l, 0, j)),
        compiler_params=_params(2),
        name="mod_vectors",
    )(cvec, mod_w, mod_b.reshape(depth, 1, n))
    return out.reshape(depth, rows, 6, d)


def _nmm_kernel(x_ref, nw_ref, mod_ref, *refs, n_out):
    w_refs, o_refs = refs[:n_out], refs[n_out:]
    h = _rms(x_ref[0], nw_ref[...])
    h = h * (1.0 + mod_ref[0, 1:2, :]) + mod_ref[0, 0:1, :]
    hb = h.astype(BF16)
    for w_ref, o_ref in zip(w_refs, o_refs):
        if w_ref.dtype == F32:
            o_ref[0] = _dot_hi(h, w_ref[...])
        else:
            o_ref[0] = _dot(hb, w_ref[...]).astype(o_ref.dtype)


def _norm_mod_matmul(xs, norm_w, mod_l, weights, out_dtypes, ctx_tiles, ctx_row):
    bsz, s, d = xs.shape
    tm = ROW_TILE
    mod_idx = lambda b, i: (jnp.where(i < ctx_tiles, ctx_row, b), 0, 0)
    in_specs = [pl.BlockSpec((1, tm, d), lambda b, i: (b, i, 0)),
                pl.BlockSpec((1, d), lambda b, i: (0, 0)),
                pl.BlockSpec((1, 6, d), mod_idx)]
    in_specs += [pl.BlockSpec(w.shape, lambda b, i: (0, 0)) for w in weights]
    out_shape = [jax.ShapeDtypeStruct((bsz, s, w.shape[1]), dt) for w, dt in zip(weights, out_dtypes)]
    out_specs = [pl.BlockSpec((1, tm, w.shape[1]), lambda b, i: (b, i, 0)) for w in weights]
    return pl.pallas_call(
        functools.partial(_nmm_kernel, n_out=len(weights)),
        out_shape=out_shape, grid=(bsz, s // tm), in_specs=in_specs, out_specs=out_specs,
        compiler_params=_params(2), name="norm_mod_matmul",
    )(xs, norm_w.reshape(1, d), mod_l, *weights)


def _conv_kernel(u_ref, w_ref, b_ref, o_ref, *, ctx_len):
    u = u_ref[0].astype(F32)
    s = u.shape[0]
    t = lax.broadcasted_iota(jnp.int32, u.shape, 0)
    prev = jnp.where((t == 0) | (t == ctx_len), 0.0, pltpu.roll(u, 1, axis=0))
    nxt = jnp.where((t == ctx_len - 1) | (t == s - 1), 0.0, pltpu.roll(u, s - 1, axis=0))
    y = prev * w_ref[0:1, :] + u * w_ref[1:2, :] + nxt * w_ref[2:3, :] + b_ref[...]
    o_ref[0] = _silu(y).astype(o_ref.dtype)


def _conv_silu(u, conv_w, conv_b, ctx_len, tc=LANES):
    bsz, s, ch = u.shape
    return pl.pallas_call(
        functools.partial(_conv_kernel, ctx_len=ctx_len),
        out_shape=jax.ShapeDtypeStruct(u.shape, u.dtype),
        grid=(bsz, ch // tc),
        in_specs=[pl.BlockSpec((1, s, tc), lambda b, j: (b, 0, j)),
                  pl.BlockSpec((3, tc), lambda b, j: (0, j)),
                  pl.BlockSpec((1, tc), lambda b, j: (0, j))],
        out_specs=pl.BlockSpec((1, s, tc), lambda b, j: (b, 0, j)),
        compiler_params=_params(2), name="conv_silu",
    )(u, conv_w, conv_b.reshape(1, ch))


def _scan_chunk_index(c, rev, n_ctx, n_all):
    if not rev:
        return c
    return jnp.where(c < n_ctx, n_ctx - 1 - c, n_ctx + n_all - 1 - c)


def _tri(n, rev):
    row = lax.broadcasted_iota(jnp.int32, (n, n), 0)
    col = lax.broadcasted_iota(jnp.int32, (n, n), 1)
    return (col >= row) if rev else (col <= row)


def _ssd_kernel(*refs, rev, off):
    if rev:
        xs_ref, bc_ref, g_ref, dtb_ref, a_ref, acc_ref, o_ref, st_ref = refs
    else:
        xs_ref, bc_ref, g_ref, dtb_ref, a_ref, dsk_ref, o_ref, st_ref = refs

    @pl.when(pl.program_id(1) == 0)
    def _():
        st_ref[...] = jnp.zeros_like(st_ref)

    n = xs_ref.shape[1]
    last = 0 if rev else n - 1
    tri = _tri(n, rev)
    dt = _softplus(g_ref[0] + dtb_ref[...])
    log_a = dt * a_ref[...]
    cs = _dot_hi(tri.astype(F32), log_a)
    cs_t = cs.T
    xs = xs_ref[0].astype(F32)
    gw = SSD_GROUPS * SSD_STATE
    heads_per_group = SSD_HEADS // SSD_GROUPS
    for g in range(SSD_GROUPS):
        b_g = bc_ref[0, :, g * SSD_STATE:(g + 1) * SSD_STATE]
        c_g = bc_ref[0, :, gw + g * SSD_STATE:gw + (g + 1) * SSD_STATE]
        cb = _dot_nt(c_g, b_g)
        b_t = b_g.astype(F32).T
        for e in range(heads_per_group):
            h = g * heads_per_group + e
            col = off + h
            a_col = cs[:, col:col + 1]
            a_row = cs_t[col:col + 1, :]
            tot = cs_t[col:col + 1, last:last + 1]
            decay = jnp.exp(jnp.where(tri, a_col - a_row, -jnp.inf))
            p0 = h * SSD_HEAD_DIM
            xdt = (xs[:, p0:p0 + SSD_HEAD_DIM] * dt[:, col:col + 1]).astype(BF16)
            state = st_ref[h]
            y = _dot((cb * decay).astype(BF16), xdt)
            y = y + _dot(c_g, state.astype(BF16)) * jnp.exp(a_col)
            bw_t = (b_t * jnp.exp(tot - a_row)).astype(BF16)
            st_ref[h] = state * jnp.exp(tot) + _dot(bw_t, xdt)
            if rev:
                y = y + acc_ref[0, :, p0:p0 + SSD_HEAD_DIM].astype(F32)
            else:
                y = y + dsk_ref[:, p0:p0 + SSD_HEAD_DIM] * xs[:, p0:p0 + SSD_HEAD_DIM]
            o_ref[0, :, p0:p0 + SSD_HEAD_DIM] = y.astype(o_ref.dtype)


def _ssd_scan(xbc, gates, dtb_row, a_row, extra, *, rev, n_ctx):
    bsz, s, _ = xbc.shape
    n = SCAN_CHUNK
    n_all = s // n
    w = SSD_HEADS * SSD_HEAD_DIM
    bcw = 2 * SSD_GROUPS * SSD_STATE
    cidx = functools.partial(_scan_chunk_index, rev=rev, n_ctx=n_ctx, n_all=n_all)
    tok = lambda b, c: (b, cidx(c), 0)
    in_specs = [pl.BlockSpec((1, n, w), tok),
                pl.BlockSpec((1, n, bcw), lambda b, c: (b, cidx(c), w // bcw)),
                pl.BlockSpec((1, n, LANES), tok),
                pl.BlockSpec((1, LANES), lambda b, c: (0, 0)),
                pl.BlockSpec((1, LANES), lambda b, c: (0, 0))]
    if rev:
        in_specs.append(pl.BlockSpec((1, n, w), tok))
    else:
        in_specs.append(pl.BlockSpec((1, w), lambda b, c: (0, 0)))
    return pl.pallas_call(
        functools.partial(_ssd_kernel, rev=rev, off=GATE_DT + (SSD_HEADS if rev else 0)),
        out_shape=jax.ShapeDtypeStruct((bsz, s, w), BF16),
        grid=(bsz, n_all), in_specs=in_specs,
        out_specs=pl.BlockSpec((1, n, w), tok),
        scratch_shapes=[pltpu.VMEM((SSD_HEADS, SSD_STATE, SSD_HEAD_DIM), F32)],
        compiler_params=_params(2), name="ssd_scan_bwd" if rev else "ssd_scan_fwd",
    )(xbc, xbc, gates, dtb_row, a_row, extra)


def _mlstm_kernel(*refs, rev, d):
    if rev:
        q_ref, k_ref, v_ref, g_ref, ib_ref, fb_ref, acc_ref, o_ref, c_st, n_st, m_st = refs
    else:
        q_ref, k_ref, v_ref, g_ref, ib_ref, fb_ref, o_ref, c_st, n_st, m_st = refs

    @pl.when(pl.program_id(1) == 0)
    def _():
        c_st[...] = jnp.zeros_like(c_st)
        n_st[...] = jnp.zeros_like(n_st)
        m_st[...] = jnp.zeros_like(m_st)

    n = q_ref.shape[1]
    last = 0 if rev else n - 1
    tri = _tri(n, rev)
    g = g_ref[0]
    log_i = g + ib_ref[...]
    log_f = -_softplus(-(g + fb_ref[...]))
    cs = _dot_hi(tri.astype(F32), log_f)
    cs_t = cs.T
    li_t = log_i.T
    for h in range(ML_HEADS):
        ci = GATE_I + ML_HEADS * d + h
        cf = GATE_F + ML_HEADS * d + h
        b_col = cs[:, cf:cf + 1]
        b_row = cs_t[cf:cf + 1, :]
        li_row = li_t[ci:ci + 1, :]
        tot = cs_t[cf:cf + 1, last:last + 1]
        dmat = jnp.where(tri, b_col - b_row + li_row, -jnp.inf)
        m_intra = jnp.max(dmat, axis=-1, keepdims=True)
        a_end = tot - b_row + li_row
        m_loc = jnp.max(a_end, axis=-1, keepdims=True)
        w_end = jnp.exp(a_end - m_loc)
        m_prev = m_st[h, 0:1, 0:1]
        m_inter = b_col + m_prev
        m_t = jnp.maximum(m_intra, m_inter)
        w_inter = jnp.exp(m_inter - m_t)
        qh = q_ref[0, :, h * ML_DK:(h + 1) * ML_DK]
        kf = k_ref[0, :, h * ML_DK:(h + 1) * ML_DK].astype(F32) * (ML_DK ** -0.5)
        kb = kf.astype(BF16)
        vh = v_ref[0, :, h * ML_DV:(h + 1) * ML_DV]
        s_mat = _dot_nt(qh, kb) * jnp.exp(dmat - m_t)
        c_prev = c_st[h]
        n_prev = n_st[h, 0:1, :]
        num = _dot(s_mat.astype(BF16), vh) + _dot(qh, c_prev.astype(BF16)) * w_inter
        den = (jnp.sum(s_mat, axis=-1, keepdims=True)
               + jnp.sum(qh.astype(F32) * n_prev, axis=-1, keepdims=True) * w_inter)
        den = jnp.maximum(jnp.abs(den), jnp.exp(-m_t))
        out = num / den
        kw_t = (kf.T * w_end).astype(BF16)
        c_chunk = _dot(kw_t, vh)
        n_chunk = _dot(jnp.broadcast_to(w_end, (8, n)).astype(BF16), kb)
        m_new = jnp.maximum(tot + m_prev, m_loc)
        a_sc = jnp.exp(tot + m_prev - m_new)
        b_sc = jnp.exp(m_loc - m_new)
        c_st[h] = c_prev * a_sc + c_chunk * b_sc
        n_st[h] = n_st[h] * a_sc + n_chunk * b_sc
        m_st[h] = jnp.broadcast_to(m_new, m_st.shape[1:])
        if rev:
            out = out + acc_ref[0, :, h * ML_DV:(h + 1) * ML_DV].astype(F32)
        o_ref[0, :, h * ML_DV:(h + 1) * ML_DV] = out.astype(o_ref.dtype)


def _mlstm_scan(q, k, v, gates, ib_row, fb_row, acc, *, rev, n_ctx):
    bsz, s, _ = q.shape
    n = SCAN_CHUNK
    n_all = s // n
    cidx = functools.partial(_scan_chunk_index, rev=rev, n_ctx=n_ctx, n_all=n_all)
    tok = lambda b, c: (b, cidx(c), 0)
    qw, vw = ML_HEADS * ML_DK, ML_HEADS * ML_DV
    in_specs = [pl.BlockSpec((1, n, qw), tok), pl.BlockSpec((1, n, qw), tok),
                pl.BlockSpec((1, n, vw), tok), pl.BlockSpec((1, n, LANES), tok),
                pl.BlockSpec((1, LANES), lambda b, c: (0, 0)),
                pl.BlockSpec((1, LANES), lambda b, c: (0, 0))]
    args = [q, k, v, gates, ib_row, fb_row]
    if rev:
        in_specs.append(pl.BlockSpec((1, n, vw), tok))
        args.append(acc)
    return pl.pallas_call(
        functools.partial(_mlstm_kernel, rev=rev, d=1 if rev else 0),
        out_shape=jax.ShapeDtypeStruct((bsz, s, vw), F32),
        grid=(bsz, n_all), in_specs=in_specs,
        out_specs=pl.BlockSpec((1, n, vw), tok),
        scratch_shapes=[pltpu.VMEM((ML_HEADS, ML_DK, ML_DV), F32),
                        pltpu.VMEM((ML_HEADS, 8, ML_DK), F32),
                        pltpu.VMEM((ML_HEADS, 8, LANES), F32)],
        compiler_params=_params(2), name="mlstm_scan_bwd" if rev else "mlstm_scan_fwd",
    )(*args)


def _post_kernel(*refs, mode):
    if mode == "mix":
        (x_ref, y_ref, z_ref, hm_ref, og_ref, snw_ref, mnw_ref, wa_ref, wb_ref,
         mod_ref, n2_ref, wr_ref, br_ref, xo_ref, h2_ref, lg_ref) = refs
        y = _rms(y_ref[0].astype(F32) * _silu(z_ref[0].astype(F32)), snw_ref[...])
        og = og_ref[0].astype(F32)
        o = _dot(y.astype(BF16), wa_ref[...])
        parts = []
        for h in range(ML_HEADS):
            sl = slice(h * ML_DV, (h + 1) * ML_DV)
            parts.append(_rms(hm_ref[0, :, sl], mnw_ref[:, sl]) * jax.nn.sigmoid(og[:, sl]))
        o = o + _dot(jnp.concatenate(parts, axis=-1).astype(BF16), wb_ref[...])
    else:
        (x_ref, a_ref, wa_ref, mod_ref, n2_ref, wr_ref, br_ref, xo_ref, h2_ref, lg_ref) = refs
        o = _dot(a_ref[0], wa_ref[...])
    x_new = x_ref[0] + mod_ref[0, 2:3, :] * o
    xo_ref[0] = x_new
    h2 = _rms(x_new, n2_ref[...]) * (1.0 + mod_ref[0, 4:5, :]) + mod_ref[0, 3:4, :]
    h2_ref[0] = h2.astype(h2_ref.dtype)
    lg_ref[0] = _dot_hi(h2, wr_ref[...]) + br_ref[...]


def _post_call(mode, x, acts, rows, mats, mod_l, norm2_w, w_router, b_router, x_tile_off, n_tiles, ctx_tiles, ctx_row):
    bsz, _, d = x.shape
    tm = ROW_TILE
    s_out = n_tiles * tm
    tok = lambda b, i: (b, i, 0)
    const = lambda b, i: (0, 0)
    mod_idx = lambda b, i: (jnp.where(i + x_tile_off < ctx_tiles, ctx_row, b), 0, 0)
    in_specs = [pl.BlockSpec((1, tm, d), lambda b, i: (b, i + x_tile_off, 0))]
    in_specs += [pl.BlockSpec((1, tm, a.shape[2]), tok) for a in acts]
    in_specs += [pl.BlockSpec(r.shape, const) for r in rows]
    in_specs += [pl.BlockSpec(m.shape, const) for m in mats]
    in_specs += [pl.BlockSpec((1, 6, d), mod_idx), pl.BlockSpec((1, d), const),
                 pl.BlockSpec(w_router.shape, const), pl.BlockSpec((1, LANES), const)]
    out_shape = [jax.ShapeDtypeStruct((bsz, s_out, d), F32),
                 jax.ShapeDtypeStruct((bsz, s_out, d), BF16),
                 jax.ShapeDtypeStruct((bsz, s_out, LANES), F32)]
    out_specs = [pl.BlockSpec((1, tm, d), tok), pl.BlockSpec((1, tm, d), tok),
                 pl.BlockSpec((1, tm, LANES), tok)]
    return pl.pallas_call(
        functools.partial(_post_kernel, mode=mode),
        out_shape=out_shape, grid=(bsz, n_tiles), in_specs=in_specs, out_specs=out_specs,
        compiler_params=_params(2), name="post_" + mode,
    )(x, *acts, *rows, *mats, mod_l, norm2_w.reshape(1, d), w_router, b_router)


def _router_kernel(lg_ref, route_ref, cnt_ref):
    @pl.when(pl.program_id(0) == 0)
    def _():
        cnt_ref[...] = jnp.zeros_like(cnt_ref)

    lg = lg_ref[...]
    tm = lg.shape[0]
    lane = lax.broadcasted_iota(jnp.int32, lg.shape, 1).astype(F32)
    big = float(LANES)
    is_g = (lane >= ROUTE_G) & (lane < ROUTE_G + MOE_GROUPS)
    lgg = jnp.where(is_g, lg, -jnp.inf)
    g_max = jnp.max(lgg, axis=-1, keepdims=True)
    g_idx = jnp.min(jnp.where(lgg == g_max, lane - ROUTE_G, big), axis=-1, keepdims=True)
    g_prob = 1.0 / jnp.sum(jnp.exp(lgg - g_max), axis=-1, keepdims=True)
    lo = g_idx * MOE_EXPERTS
    le = jnp.where((lane >= lo) & (lane < lo + MOE_EXPERTS), lg, -jnp.inf)
    l1 = jnp.max(le, axis=-1, keepdims=True)
    i1 = jnp.min(jnp.where(le == l1, lane, big), axis=-1, keepdims=True)
    le2 = jnp.where(lane == i1, -jnp.inf, le)
    l2 = jnp.max(le2, axis=-1, keepdims=True)
    i2 = jnp.min(jnp.where(le2 == l2, lane, big), axis=-1, keepdims=True)
    r = jnp.exp(l2 - l1)
    w1 = g_prob / (1.0 + r)
    w2 = w1 * r
    oh1 = jnp.where(lane == i1, 1.0, 0.0)
    oh2 = jnp.where(lane == i2, 1.0, 0.0)
    oh = oh1 + oh2
    row = lax.broadcasted_iota(jnp.int32, (tm, tm), 0)
    col = lax.broadcasted_iota(jnp.int32, (tm, tm), 1)
    before = jnp.where(col < row, 1.0, 0.0).astype(BF16)
    prefix = _dot(before, oh.astype(BF16)) + cnt_ref[0:1, :]
    rank1 = jnp.sum(prefix * oh1, axis=-1, keepdims=True)
    rank2 = jnp.sum(prefix * oh2, axis=-1, keepdims=True)
    cnt_ref[...] = cnt_ref[...] + jnp.sum(oh, axis=0, keepdims=True)
    fields = (i1, i2, w1, w2, rank1, rank2)
    out = jnp.zeros_like(lg)
    for j, f in enumerate(fields):
        out = jnp.where(lane == float(j), f, out)
    route_ref[...] = out


def _router(logits):
    t = logits.shape[0]
    tm = ROW_TILE
    return pl.pallas_call(
        _router_kernel,
        out_shape=[jax.ShapeDtypeStruct((t, LANES), F32), jax.ShapeDtypeStruct((8, LANES), F32)],
        grid=(t // tm,),
        in_specs=[pl.BlockSpec((tm, LANES), lambda i: (i, 0))],
        out_specs=[pl.BlockSpec((tm, LANES), lambda i: (i, 0)), pl.BlockSpec((8, LANES), lambda i: (0, 0))],
        compiler_params=_params(1), name="router",
    )(logits)


def _moe_kernel(te_ref, nt_ref, x_ref, wg_ref, wu_ref, wd_ref, o_ref, wgb, wub, wdb):
    i = pl.program_id(0)
    valid = i < nt_ref[0]
    fresh = (i == 0) | (te_ref[i] != te_ref[jnp.maximum(i - 1, 0)])

    @pl.when(valid & fresh)
    def _():
        wgb[...] = wg_ref[0].astype(BF16)
        wub[...] = wu_ref[0].astype(BF16)
        wdb[...] = wd_ref[0].astype(BF16)

    @pl.when(valid)
    def _():
        x = x_ref[...]
        act = _silu(_dot(x, wgb[...])) * _dot(x, wub[...])
        o_ref[...] = _dot(act.astype(BF16), wdb[...]).astype(o_ref.dtype)

    @pl.when(jnp.logical_not(valid))
    def _():
        o_ref[...] = jnp.zeros_like(o_ref)


def _moe_experts(x_sorted, tile_expert, n_tiles_used, wg, wu, wd):
    rows, d = x_sorted.shape
    tm = MOE_TILE
    ff = wg.shape[2]
    grid_spec = pltpu.PrefetchScalarGridSpec(
        num_scalar_prefetch=2, grid=(rows // tm,),
        in_specs=[pl.BlockSpec((tm, d), lambda i, te, nt: (i, 0)),
                  pl.BlockSpec((1, d, ff), lambda i, te, nt: (te[i], 0, 0)),
                  pl.BlockSpec((1, d, ff), lambda i, te, nt: (te[i], 0, 0)),
                  pl.BlockSpec((1, ff, d), lambda i, te, nt: (te[i], 0, 0))],
        out_specs=pl.BlockSpec((tm, d), lambda i, te, nt: (i, 0)),
        scratch_shapes=[pltpu.VMEM((d, ff), BF16), pltpu.VMEM((d, ff), BF16), pltpu.VMEM((ff, d), BF16)])
    return pl.pallas_call(
        _moe_kernel, out_shape=jax.ShapeDtypeStruct((rows, d), F32), grid_spec=grid_spec,
        compiler_params=_params(1), name="moe_experts",
    )(tile_expert, n_tiles_used, x_sorted, wg, wu, wd)


def _combine_kernel(x_ref, y1_ref, y2_ref, rt_ref, mod_ref, o_ref):
    f = rt_ref[0, :, 2:3] * y1_ref[0] + rt_ref[0, :, 3:4] * y2_ref[0]
    o_ref[0] = x_ref[0] + mod_ref[0, 5:6, :] * f


def _combine(x, y1, y2, route, mod_l, ctx_tiles, ctx_row):
    bsz, s, d = x.shape
    tm = ROW_TILE
    tok = lambda b, i: (b, i, 0)
    mod_idx = lambda b, i: (jnp.where(i < ctx_tiles, ctx_row, b), 0, 0)
    return pl.pallas_call(
        _combine_kernel, out_shape=jax.ShapeDtypeStruct(x.shape, F32), grid=(bsz, s // tm),
        in_specs=[pl.BlockSpec((1, tm, d), tok), pl.BlockSpec((1, tm, d), tok), pl.BlockSpec((1, tm, d), tok),
                  pl.BlockSpec((1, tm, LANES), tok), pl.BlockSpec((1, 6, d), mod_idx)],
        out_specs=pl.BlockSpec((1, tm, d), tok),
        compiler_params=_params(2), name="moe_combine",
    )(x, y1.reshape(x.shape), y2.reshape(x.shape), route.reshape(bsz, s, LANES), mod_l)


def _hier_moe(x, h2, logits, mod_l, wg, wu, wd, ctx_tiles, ctx_row):
    bsz, s, d = h2.shape
    t = bsz * s
    tm = MOE_TILE
    route, counts = _router(logits.reshape(t, LANES))
    n_tiles = 2 * t // tm + N_EXPERTS
    cnt = counts[0, :N_EXPERTS].astype(jnp.int32)
    tiles_per = (cnt + tm - 1) // tm
    tile_end = jnp.cumsum(tiles_per)
    row_off = (tile_end - tiles_per) * tm
    ri = route[:, :6].astype(jnp.int32)
    pos1 = row_off[ri[:, 0]] + ri[:, 4]
    pos2 = row_off[ri[:, 1]] + ri[:, 5]
    tok = jnp.arange(t, dtype=jnp.int32)
    src = jnp.zeros((n_tiles * tm,), jnp.int32).at[jnp.concatenate([pos1, pos2])].set(jnp.concatenate([tok, tok]))
    tile_expert = jnp.minimum(jnp.searchsorted(tile_end, jnp.arange(n_tiles, dtype=jnp.int32), side="right"),
                              N_EXPERTS - 1).astype(jnp.int32)
    x_sorted = jnp.take(h2.reshape(t, d), src, axis=0)
    y_sorted = _moe_experts(x_sorted, tile_expert, tile_end[-1:].astype(jnp.int32),
                            wg.reshape(N_EXPERTS, d, -1), wu.reshape(N_EXPERTS, d, -1), wd.reshape(N_EXPERTS, -1, d))
    y1 = jnp.take(y_sorted, pos1, axis=0)
    y2 = jnp.take(y_sorted, pos2, axis=0)
    return _combine(x, y1, y2, route, mod_l, ctx_tiles, ctx_row)


def _attn_kernel(q_ref, k_ref, v_ref, bias_ref, qw_ref, kw_ref, o_ref, qn_ref, kn_ref, *, ctx_len, rows):
    lane = lax.broadcasted_iota(jnp.int32, (1, LANES), 1)
    first = lane < NA_HEAD_DIM

    def head_norm(x, w):
        x2 = x * x
        s0 = jnp.sum(jnp.where(first, x2, 0.0), axis=-1, keepdims=True)
        s1 = jnp.sum(jnp.where(first, 0.0, x2), axis=-1, keepdims=True)
        ms = jnp.where(first, s0, s1) * (1.0 / NA_HEAD_DIM)
        return x * lax.rsqrt(ms + RMS_EPS) * w

    kn_ref[...] = head_norm(k_ref[0].astype(F32), kw_ref[...]).astype(BF16)
    qn_ref[...] = (head_norm(q_ref[0, ctx_len:, :].astype(F32), qw_ref[...]) * NA_HEAD_DIM ** -0.5).astype(BF16)
    kh = min(NA_KH, rows)
    n_loc = kh * GRID_W
    k_ctx = kn_ref[0:ctx_len, :]
    v_ctx = v_ref[0, 0:ctx_len, :]

    def body(r, carry):
        r0 = jnp.clip(r - kh // 2, 0, rows - kh)
        start = r0 - r + (NA_KH - 1)
        q = qn_ref[pl.ds(pl.multiple_of(r * GRID_W, GRID_W), GRID_W), :]
        k_off = pl.multiple_of(ctx_len + r0 * GRID_W, GRID_W)
        k_loc = kn_ref[pl.ds(k_off, n_loc), :]
        v_loc = v_ref[0, pl.ds(k_off, n_loc), :]
        outs = []
        for hh in range(2):
            sel = first if hh == 0 else jnp.logical_not(first)
            qm = jnp.where(sel, q, jnp.zeros_like(q))
            s_loc = _dot_nt(qm, k_loc) + bias_ref[hh, start]
            s_ctx = _dot_nt(qm, k_ctx)
            m = jnp.maximum(jnp.max(s_loc, axis=-1, keepdims=True), jnp.max(s_ctx, axis=-1, keepdims=True))
            p_loc = jnp.exp(s_loc - m)
            p_ctx = jnp.exp(s_ctx - m)
            den = jnp.sum(p_loc, axis=-1, keepdims=True) + jnp.sum(p_ctx, axis=-1, keepdims=True)
            o = _dot(p_loc.astype(BF16), v_loc) + _dot(p_ctx.astype(BF16), v_ctx)
            outs.append(o / den)
        o_ref[0, pl.ds(pl.multiple_of(r * GRID_W, GRID_W), GRID_W), :] = jnp.where(first, outs[0], outs[1]).astype(o_ref.dtype)
        return carry

    lax.fori_loop(0, rows, body, 0)


def _bias_windows(rpb, rows):
    kh = min(NA_KH, rows)
    qc = np.arange(GRID_W)
    c0 = np.clip(qc - NA_KW // 2, 0, GRID_W - NA_KW)
    kc = np.arange(GRID_W)
    inwin = (kc[None, :] >= c0[:, None]) & (kc[None, :] < c0[:, None] + NA_KW)
    coff = np.clip(kc[None, :] - qc[:, None] + NA_KW - 1, 0, 2 * NA_KW - 2)
    tab = jnp.where(inwin[None, None], rpb[:, :, coff], NEG)
    idx = np.clip(np.arange(NA_KH)[:, None] + np.arange(kh)[None, :], 0, 2 * NA_KH - 2)
    win = tab[:, idx]
    return win.transpose(0, 1, 3, 2, 4).reshape(rpb.shape[0], NA_KH, GRID_W, kh * GRID_W).astype(F32)


def _neighbourhood_attention(q, k, v, bias, qn_w, kn_w, ctx_len):
    bsz, s, w = q.shape
    seq = s - ctx_len
    rows = seq // GRID_W
    n_pairs = NA_HEADS // 2
    pair = lambda b, p: (b, 0, p)
    row2 = lambda b, p: (0, 0)
    return pl.pallas_call(
        functools.partial(_attn_kernel, ctx_len=ctx_len, rows=rows),
        out_shape=jax.ShapeDtypeStruct((bsz, seq, w), BF16),
        grid=(bsz, n_pairs),
        in_specs=[pl.BlockSpec((1, s, LANES), pair), pl.BlockSpec((1, s, LANES), pair),
                  pl.BlockSpec((1, s, LANES), pair),
                  pl.BlockSpec((2,) + bias.shape[1:], lambda b, p: (p, 0, 0, 0)),
                  pl.BlockSpec((1, LANES), row2), pl.BlockSpec((1, LANES), row2)],
        out_specs=pl.BlockSpec((1, seq, LANES), pair),
        scratch_shapes=[pltpu.VMEM((seq, LANES), BF16), pltpu.VMEM((s, LANES), BF16)],
        compiler_params=_params(2), name="neighbourhood_attention",
    )(q, k, v, bias, jnp.tile(qn_w, 2).reshape(1, LANES), jnp.tile(kn_w, 2).reshape(1, LANES))


def _pad_row(pieces, width=LANES):
    row = jnp.zeros((width,), F32)
    for off, vec in pieces:
        row = row.at[off:off + vec.shape[0]].set(vec.astype(F32))
    return row.reshape(1, width)


def _router_params(w_group, b_group, w_expert, b_expert):
    d = w_group.shape[0]
    w = jnp.zeros((d, LANES), F32).at[:, :N_EXPERTS].set(w_expert).at[:, ROUTE_G:ROUTE_G + MOE_GROUPS].set(w_group)
    return w, _pad_row([(0, b_expert), (ROUTE_G, b_group)])


def kernel(x, c, ctx, c_ctx, norm1_w, norm2_w, mod_w, mod_b, ab_w_in, ab_conv_w, ab_conv_b, ssd_a_log, ssd_dt_bias, ssd_d, ssd_norm_w, ml_i_bias, ml_f_bias, ml_norm_w, ab_w_out, na_w_qkv, na_q_norm, na_k_norm, na_rpb, na_w_out, moe_w_group, moe_b_group, moe_w_expert, moe_b_expert, moe_w_gate, moe_w_up, moe_w_down):
    bsz, seq, d = x.shape
    ctx_len = ctx.shape[1]
    depth = mod_w.shape[0]
    assert depth == 2 and ctx_len % ROW_TILE == 0 and seq % ROW_TILE == 0 and bsz < 8
    ctx_tiles = ctx_len // ROW_TILE
    lat_tiles = seq // ROW_TILE
    ctx_row = bsz

    cvec = jnp.zeros((8, d), F32).at[:bsz].set(c).at[bsz].set(c_ctx)
    mod = _mod_vectors(cvec, mod_w, mod_b)
    xs = jnp.concatenate([ctx, x], axis=1)

    ssd_w = SSD_HEADS * SSD_HEAD_DIM
    xbc_w = ssd_w + 2 * SSD_GROUPS * SSD_STATE
    qk_w, v_w = ML_HEADS * ML_DK, ML_HEADS * ML_DV
    sizes = (ssd_w, xbc_w, 2 * SSD_HEADS, qk_w, qk_w, v_w, v_w, 2 * ML_HEADS, 2 * ML_HEADS)
    w_z, w_xbc, w_dt, w_q, w_k, w_v, w_o, w_i, w_f = jnp.split(ab_w_in[0], np.cumsum(sizes)[:-1].tolist(), axis=1)
    w_gate = jnp.zeros((d, LANES), F32).at[:, :GATE_F + 2 * ML_HEADS].set(jnp.concatenate([w_dt, w_i, w_f], axis=1))
    weights = [w.astype(BF16) for w in (w_z, w_xbc, w_q, w_k, w_v, w_o)] + [w_gate]
    z, xbc, q, k, v, og, gates = _norm_mod_matmul(xs, norm1_w[0], mod[0], weights, [BF16] * 6 + [F32],
                                                  ctx_tiles, ctx_row)
    xbc = _conv_silu(xbc, ab_conv_w[0], ab_conv_b[0], ctx_len)
    n_ctx = ctx_len // SCAN_CHUNK
    a_neg = -jnp.exp(ssd_a_log[0].astype(F32))
    dsk_row = jnp.repeat(ssd_d[0].astype(F32), SSD_HEAD_DIM).reshape(1, ssd_w)
    y = None
    hm = None
    for dr in range(2):
        rev = dr == 1
        dtb_row = _pad_row([(GATE_DT + dr * SSD_HEADS, ssd_dt_bias[0, dr])])
        a_row = _pad_row([(GATE_DT + dr * SSD_HEADS, a_neg[dr])])
        y = _ssd_scan(xbc, gates, dtb_row, a_row, y if rev else dsk_row, rev=rev, n_ctx=n_ctx)
        ib_row = _pad_row([(GATE_I + dr * ML_HEADS, ml_i_bias[0, dr])])
        fb_row = _pad_row([(GATE_F + dr * ML_HEADS, ml_f_bias[0, dr])])
        hm = _mlstm_scan(q, k, v, gates, ib_row, fb_row, hm, rev=rev, n_ctx=n_ctx)
    w_r, b_r = _router_params(moe_w_group[0], moe_b_group[0], moe_w_expert[0], moe_b_expert[0])
    w_out = ab_w_out[0].astype(BF16)
    x1, h2, logits = _post_call(
        "mix", xs, [y, z, hm, og],
        [ssd_norm_w[0].reshape(1, ssd_w), ml_norm_w[0].reshape(1, v_w)], [w_out[:ssd_w], w_out[ssd_w:]],
        mod[0], norm2_w[0], w_r, b_r, 0, ctx_tiles + lat_tiles, ctx_tiles, ctx_row)
    xs = _hier_moe(x1, h2, logits, mod[0], moe_w_gate[0], moe_w_up[0], moe_w_down[0], ctx_tiles, ctx_row)

    w_qkv = na_w_qkv[0].astype(BF16)
    na_w = NA_HEADS * NA_HEAD_DIM
    q, k, v = _norm_mod_matmul(xs, norm1_w[1], mod[1], [w_qkv[:, :na_w], w_qkv[:, na_w:2 * na_w], w_qkv[:, 2 * na_w:]],
                               [BF16] * 3, ctx_tiles, ctx_row)
    bias = _bias_windows(na_rpb[0], seq // GRID_W)
    attn = _neighbourhood_attention(q, k, v, bias, na_q_norm[0], na_k_norm[0], ctx_len)
    w_r, b_r = _router_params(moe_w_group[1], moe_b_group[1], moe_w_expert[1], moe_b_expert[1])
    x1, h2, logits = _post_call("attn", xs, [attn], [], [na_w_out[0].astype(BF16)],
                                mod[1], norm2_w[1], w_r, b_r, ctx_tiles, lat_tiles, ctx_tiles, ctx_row)
    return _hier_moe(x1, h2, logits, mod[1], moe_w_gate[1], moe_w_up[1], moe_w_down[1], 0, ctx_row)
```

```python
import functools

import numpy as np
import jax
import jax.numpy as jnp
from jax import lax
from jax.experimental import pallas as pl
from jax.experimental.pallas import tpu as pltpu

F32 = jnp.float32
BF16 = jnp.bfloat16
HIGHEST = lax.Precision.HIGHEST

RMS_EPS = 1e-6
GRID_W = 64
SSD_HEADS = 16
SSD_HEAD_DIM = 64
SSD_GROUPS = 2
SSD_STATE = 128
ML_HEADS = 4
ML_DK = 128
ML_DV = 256
NA_HEADS = 16
NA_HEAD_DIM = 64
NA_KH = 8
NA_KW = 16
MOE_GROUPS = 4
MOE_EXPERTS = 8
N_EXPERTS = MOE_GROUPS * MOE_EXPERTS

LANES = 128
ROW_TILE = 256
SCAN_CHUNK = 128
MOE_TILE = 256
ATTN_GROUP_ROWS = 4
ATTN_KEY_ROWS = ATTN_GROUP_ROWS + NA_KH - 1
ATTN_GROUPS_PER_TRIP = 4
VMEM_LIMIT = 56 * 1024 * 1024

GATE_DT = 0
GATE_I = 2 * SSD_HEADS
GATE_F = GATE_I + 2 * ML_HEADS
ROUTE_G = N_EXPERTS
NEG = -1e30


def _params(n_axes):
    return pltpu.CompilerParams(dimension_semantics=("arbitrary",) * n_axes,
                                vmem_limit_bytes=VMEM_LIMIT)


def _silu(x):
    return x * jax.nn.sigmoid(x)


def _softplus(x):
    return jnp.maximum(x, 0.0) + jnp.log1p(jnp.exp(-jnp.abs(x)))


def _rms(x, w):
    return x * lax.rsqrt(jnp.mean(x * x, axis=-1, keepdims=True) + RMS_EPS) * w


def _dot(a, b):
    return jnp.dot(a, b, preferred_element_type=F32)


def _dot_nt(a, b):
    return lax.dot_general(a, b, (((1,), (1,)), ((), ())), preferred_element_type=F32)


def _dot_hi(a, b):
    return jnp.dot(a, b, precision=HIGHEST, preferred_element_type=F32)


def _mod_kernel(c_ref, w_ref, b_ref, o_ref):
    o_ref[0] = _dot_hi(_silu(c_ref[...]), w_ref[0]) + b_ref[0]


def _mod_vectors(cvec, mod_w, mod_b, tn=512):
    depth, d, n = mod_w.shape
    rows = cvec.shape[0]
    out = pl.pallas_call(
        _mod_kernel,
        out_shape=jax.ShapeDtypeStruct((depth, rows, n), F32),
        grid=(depth, n // tn),
        in_specs=[pl.BlockSpec((rows, d), lambda l, j: (0, 0)),
                  pl.BlockSpec((1, d, tn), lambda l, j: (l, 0, j)),
                  pl.BlockSpec((1, 1, tn), lambda l, j: (l, 0, j))],
        out_specs=pl.BlockSpec((1, rows, tn), lambda l, j: (l, 0, j)),
        compiler_params=_params(2),
        name="mod_vectors",
    )(cvec, mod_w, mod_b.reshape(depth, 1, n))
    return out.reshape(depth, rows, 6, d)


def _nmm_kernel(x_ref, nw_ref, mod_ref, *refs, n_out):
    w_refs, o_refs = refs[:n_out], refs[n_out:]
    h = _rms(x_ref[0], nw_ref[...])
    h = h * (1.0 + mod_ref[0, 1:2, :]) + mod_ref[0, 0:1, :]
    hb = h.astype(BF16)
    for w_ref, o_ref in zip(w_refs, o_refs):
        if w_ref.dtype == F32:
            o_ref[0] = _dot_hi(h, w_ref[...])
        else:
            o_ref[0] = _dot(hb, w_ref[...]).astype(o_ref.dtype)


def _norm_mod_matmul(xs, norm_w, mod_l, weights, out_dtypes, ctx_tiles, ctx_row):
    bsz, s, d = xs.shape
    tm = ROW_TILE
    mod_idx = lambda b, i: (jnp.where(i < ctx_tiles, ctx_row, b), 0, 0)
    in_specs = [pl.BlockSpec((1, tm, d), lambda b, i: (b, i, 0)),
                pl.BlockSpec((1, d), lambda b, i: (0, 0)),
                pl.BlockSpec((1, 6, d), mod_idx)]
    in_specs += [pl.BlockSpec(w.shape, lambda b, i: (0, 0)) for w in weights]
    out_shape = [jax.ShapeDtypeStruct((bsz, s, w.shape[1]), dt) for w, dt in zip(weights, out_dtypes)]
    out_specs = [pl.BlockSpec((1, tm, w.shape[1]), lambda b, i: (b, i, 0)) for w in weights]
    return pl.pallas_call(
        functools.partial(_nmm_kernel, n_out=len(weights)),
        out_shape=out_shape, grid=(bsz, s // tm), in_specs=in_specs, out_specs=out_specs,
        compiler_params=_params(2), name="norm_mod_matmul",
    )(xs, norm_w.reshape(1, d), mod_l, *weights)


def _conv_kernel(u_ref, w_ref, b_ref, o_ref, *, ctx_len):
    u = u_ref[0].astype(F32)
    s = u.shape[0]
    t = lax.broadcasted_iota(jnp.int32, u.shape, 0)
    prev = jnp.where((t == 0) | (t == ctx_len), 0.0, pltpu.roll(u, 1, axis=0))
    nxt = jnp.where((t == ctx_len - 1) | (t == s - 1), 0.0, pltpu.roll(u, s - 1, axis=0))
    y = prev * w_ref[0:1, :] + u * w_ref[1:2, :] + nxt * w_ref[2:3, :] + b_ref[...]
    o_ref[0] = _silu(y).astype(o_ref.dtype)


def _conv_silu(u, conv_w, conv_b, ctx_len, tc=LANES):
    bsz, s, ch = u.shape
    return pl.pallas_call(
        functools.partial(_conv_kernel, ctx_len=ctx_len),
        out_shape=jax.ShapeDtypeStruct(u.shape, u.dtype),
        grid=(bsz, ch // tc),
        in_specs=[pl.BlockSpec((1, s, tc), lambda b, j: (b, 0, j)),
                  pl.BlockSpec((3, tc), lambda b, j: (0, j)),
                  pl.BlockSpec((1, tc), lambda b, j: (0, j))],
        out_specs=pl.BlockSpec((1, s, tc), lambda b, j: (b, 0, j)),
        compiler_params=_params(2), name="conv_silu",
    )(u, conv_w, conv_b.reshape(1, ch))


def _scan_chunk_index(c, rev, n_ctx, n_all):
    if not rev:
        return c
    return jnp.where(c < n_ctx, n_ctx - 1 - c, n_ctx + n_all - 1 - c)


def _tri(n, rev):
    row = lax.broadcasted_iota(jnp.int32, (n, n), 0)
    col = lax.broadcasted_iota(jnp.int32, (n, n), 1)
    return (col >= row) if rev else (col <= row)


def _ssd_kernel(*refs, rev, off):
    if rev:
        xs_ref, bc_ref, g_ref, dtb_ref, a_ref, acc_ref, o_ref, st_ref = refs
    else:
        xs_ref, bc_ref, g_ref, dtb_ref, a_ref, dsk_ref, o_ref, st_ref = refs

    @pl.when(pl.program_id(1) == 0)
    def _():
        st_ref[...] = jnp.zeros_like(st_ref)

    n = xs_ref.shape[1]
    last = 0 if rev else n - 1
    tri = _tri(n, rev)
    dt = _softplus(g_ref[0] + dtb_ref[...])
    log_a = dt * a_ref[...]
    cs = _dot_hi(tri.astype(F32), log_a)
    cs_t = cs.T
    xs = xs_ref[0].astype(F32)
    gw = SSD_GROUPS * SSD_STATE
    heads_per_group = SSD_HEADS // SSD_GROUPS
    for g in range(SSD_GROUPS):
        b_g = bc_ref[0, :, g * SSD_STATE:(g + 1) * SSD_STATE]
        c_g = bc_ref[0, :, gw + g * SSD_STATE:gw + (g + 1) * SSD_STATE]
        cb = _dot_nt(c_g, b_g)
        b_t = b_g.astype(F32).T
        for e in range(heads_per_group):
            h = g * heads_per_group + e
            col = off + h
            a_col = cs[:, col:col + 1]
            a_row = cs_t[col:col + 1, :]
            tot = cs_t[col:col + 1, last:last + 1]
            decay = jnp.exp(jnp.where(tri, a_col - a_row, -jnp.inf))
            p0 = h * SSD_HEAD_DIM
            xdt = (xs[:, p0:p0 + SSD_HEAD_DIM] * dt[:, col:col + 1]).astype(BF16)
            state = st_ref[h]
            y = _dot((cb * decay).astype(BF16), xdt)
            y = y + _dot(c_g, state.astype(BF16)) * jnp.exp(a_col)
            bw_t = (b_t * jnp.exp(tot - a_row)).astype(BF16)
            st_ref[h] = state * jnp.exp(tot) + _dot(bw_t, xdt)
            if rev:
                y = y + acc_ref[0, :, p0:p0 + SSD_HEAD_DIM].astype(F32)
            else:
                y = y + dsk_ref[:, p0:p0 + SSD_HEAD_DIM] * xs[:, p0:p0 + SSD_HEAD_DIM]
            o_ref[0, :, p0:p0 + SSD_HEAD_DIM] = y.astype(o_ref.dtype)


def _ssd_scan(xbc, gates, dtb_row, a_row, extra, *, rev, n_ctx):
    bsz, s, _ = xbc.shape
    n = SCAN_CHUNK
    n_all = s // n
    w = SSD_HEADS * SSD_HEAD_DIM
    bcw = 2 * SSD_GROUPS * SSD_STATE
    cidx = functools.partial(_scan_chunk_index, rev=rev, n_ctx=n_ctx, n_all=n_all)
    tok = lambda b, c: (b, cidx(c), 0)
    in_specs = [pl.BlockSpec((1, n, w), tok),
                pl.BlockSpec((1, n, bcw), lambda b, c: (b, cidx(c), w // bcw)),
                pl.BlockSpec((1, n, LANES), tok),
                pl.BlockSpec((1, LANES), lambda b, c: (0, 0)),
                pl.BlockSpec((1, LANES), lambda b, c: (0, 0))]
    if rev:
        in_specs.append(pl.BlockSpec((1, n, w), tok))
    else:
        in_specs.append(pl.BlockSpec((1, w), lambda b, c: (0, 0)))
    return pl.pallas_call(
        functools.partial(_ssd_kernel, rev=rev, off=GATE_DT + (SSD_HEADS if rev else 0)),
        out_shape=jax.ShapeDtypeStruct((bsz, s, w), BF16),
        grid=(bsz, n_all), in_specs=in_specs,
        out_specs=pl.BlockSpec((1, n, w), tok),
        scratch_shapes=[pltpu.VMEM((SSD_HEADS, SSD_STATE, SSD_HEAD_DIM), F32)],
        compiler_params=_params(2), name="ssd_scan_bwd" if rev else "ssd_scan_fwd",
    )(xbc, xbc, gates, dtb_row, a_row, extra)


def _mlstm_kernel(*refs, rev, d):
    if rev:
        q_ref, k_ref, v_ref, g_ref, ib_ref, fb_ref, acc_ref, o_ref, c_st, n_st, m_st = refs
    else:
        q_ref, k_ref, v_ref, g_ref, ib_ref, fb_ref, o_ref, c_st, n_st, m_st = refs

    @pl.when(pl.program_id(1) == 0)
    def _():
        c_st[...] = jnp.zeros_like(c_st)
        n_st[...] = jnp.zeros_like(n_st)
        m_st[...] = jnp.zeros_like(m_st)

    n = q_ref.shape[1]
    last = 0 if rev else n - 1
    tri = _tri(n, rev)
    g = g_ref[0]
    log_i = g + ib_ref[...]
    log_f = -_softplus(-(g + fb_ref[...]))
    cs = _dot_hi(tri.astype(F32), log_f)
    cs_t = cs.T
    li_t = log_i.T
    for h in range(ML_HEADS):
        ci = GATE_I + ML_HEADS * d + h
        cf = GATE_F + ML_HEADS * d + h
        b_col = cs[:, cf:cf + 1]
        b_row = cs_t[cf:cf + 1, :]
        li_row = li_t[ci:ci + 1, :]
        tot = cs_t[cf:cf + 1, last:last + 1]
        dmat = jnp.where(tri, b_col - b_row + li_row, -jnp.inf)
        m_intra = jnp.max(dmat, axis=-1, keepdims=True)
        a_end = tot - b_row + li_row
        m_loc = jnp.max(a_end, axis=-1, keepdims=True)
        w_end = jnp.exp(a_end - m_loc)
        m_prev = m_st[h, 0:1, 0:1]
        m_inter = b_col + m_prev
        m_t = jnp.maximum(m_intra, m_inter)
        w_inter = jnp.exp(m_inter - m_t)
        qh = q_ref[0, :, h * ML_DK:(h + 1) * ML_DK]
        kf = k_ref[0, :, h * ML_DK:(h + 1) * ML_DK].astype(F32) * (ML_DK ** -0.5)
        kb = kf.astype(BF16)
        vh = v_ref[0, :, h * ML_DV:(h + 1) * ML_DV]
        s_mat = _dot_nt(qh, kb) * jnp.exp(dmat - m_t)
        c_prev = c_st[h]
        n_prev = n_st[h, 0:1, :]
        num = _dot(s_mat.astype(BF16), vh) + _dot(qh, c_prev.astype(BF16)) * w_inter
        den = (jnp.sum(s_mat, axis=-1, keepdims=True)
               + jnp.sum(qh.astype(F32) * n_prev, axis=-1, keepdims=True) * w_inter)
        den = jnp.maximum(jnp.abs(den), jnp.exp(-m_t))
        out = num / den
        kw_t = (kf.T * w_end).astype(BF16)
        c_chunk = _dot(kw_t, vh)
        n_chunk = _dot(jnp.broadcast_to(w_end, (8, n)).astype(BF16), kb)
        m_new = jnp.maximum(tot + m_prev, m_loc)
        a_sc = jnp.exp(tot + m_prev - m_new)
        b_sc = jnp.exp(m_loc - m_new)
        c_st[h] = c_prev * a_sc + c_chunk * b_sc
        n_st[h] = n_st[h] * a_sc + n_chunk * b_sc
        m_st[h] = jnp.broadcast_to(m_new, m_st.shape[1:])
        if rev:
            out = out + acc_ref[0, :, h * ML_DV:(h + 1) * ML_DV].astype(F32)
        o_ref[0, :, h * ML_DV:(h + 1) * ML_DV] = out.astype(o_ref.dtype)


def _mlstm_scan(q, k, v, gates, ib_row, fb_row, acc, *, rev, n_ctx):
    bsz, s, _ = q.shape
    n = SCAN_CHUNK
    n_all = s // n
    cidx = functools.partial(_scan_chunk_index, rev=rev, n_ctx=n_ctx, n_all=n_all)
    tok = lambda b, c: (b, cidx(c), 0)
    qw, vw = ML_HEADS * ML_DK, ML_HEADS * ML_DV
    in_specs = [pl.BlockSpec((1, n, qw), tok), pl.BlockSpec((1, n, qw), tok),
                pl.BlockSpec((1, n, vw), tok), pl.BlockSpec((1, n, LANES), tok),
                pl.BlockSpec((1, LANES), lambda b, c: (0, 0)),
                pl.BlockSpec((1, LANES), lambda b, c: (0, 0))]
    args = [q, k, v, gates, ib_row, fb_row]
    if rev:
        in_specs.append(pl.BlockSpec((1, n, vw), tok))
        args.append(acc)
    return pl.pallas_call(
        functools.partial(_mlstm_kernel, rev=rev, d=1 if rev else 0),
        out_shape=jax.ShapeDtypeStruct((bsz, s, vw), F32),
        grid=(bsz, n_all), in_specs=in_specs,
        out_specs=pl.BlockSpec((1, n, vw), tok),
        scratch_shapes=[pltpu.VMEM((ML_HEADS, ML_DK, ML_DV), F32),
                        pltpu.VMEM((ML_HEADS, 8, ML_DK), F32),
                        pltpu.VMEM((ML_HEADS, 8, LANES), F32)],
        compiler_params=_params(2), name="mlstm_scan_bwd" if rev else "mlstm_scan_fwd",
    )(*args)


def _post_kernel(*refs, mode):
    if mode == "mix":
        (x_ref, y_ref, z_ref, hm_ref, og_ref, snw_ref, mnw_ref, wa_ref, wb_ref,
         mod_ref, n2_ref, wr_ref, br_ref, xo_ref, h2_ref, lg_ref) = refs
        y = _rms(y_ref[0].astype(F32) * _silu(z_ref[0].astype(F32)), snw_ref[...])
        og = og_ref[0].astype(F32)
        o = _dot(y.astype(BF16), wa_ref[...])
        parts = []
        for h in range(ML_HEADS):
            sl = slice(h * ML_DV, (h + 1) * ML_DV)
            parts.append(_rms(hm_ref[0, :, sl], mnw_ref[:, sl]) * jax.nn.sigmoid(og[:, sl]))
        o = o + _dot(jnp.concatenate(parts, axis=-1).astype(BF16), wb_ref[...])
    else:
        (x_ref, a_ref, wa_ref, mod_ref, n2_ref, wr_ref, br_ref, xo_ref, h2_ref, lg_ref) = refs
        o = _dot(a_ref[0], wa_ref[...])
    x_new = x_ref[0] + mod_ref[0, 2:3, :] * o
    xo_ref[0] = x_new
    h2 = _rms(x_new, n2_ref[...]) * (1.0 + mod_ref[0, 4:5, :]) + mod_ref[0, 3:4, :]
    h2_ref[0] = h2
    lg_ref[0] = _dot_hi(h2, wr_ref[...]) + br_ref[...]


def _post_call(mode, x, acts, rows, mats, mod_l, norm2_w, w_router, b_router, x_tile_off, n_tiles, ctx_tiles, ctx_row):
    bsz, _, d = x.shape
    tm = ROW_TILE
    s_out = n_tiles * tm
    tok = lambda b, i: (b, i, 0)
    const = lambda b, i: (0, 0)
    mod_idx = lambda b, i: (jnp.where(i + x_tile_off < ctx_tiles, ctx_row, b), 0, 0)
    in_specs = [pl.BlockSpec((1, tm, d), lambda b, i: (b, i + x_tile_off, 0))]
    in_specs += [pl.BlockSpec((1, tm, a.shape[2]), tok) for a in acts]
    in_specs += [pl.BlockSpec(r.shape, const) for r in rows]
    in_specs += [pl.BlockSpec(m.shape, const) for m in mats]
    in_specs += [pl.BlockSpec((1, 6, d), mod_idx), pl.BlockSpec((1, d), const),
                 pl.BlockSpec(w_router.shape, const), pl.BlockSpec((1, LANES), const)]
    out_shape = [jax.ShapeDtypeStruct((bsz, s_out, d), F32),
                 jax.ShapeDtypeStruct((bsz, s_out, d), F32),
                 jax.ShapeDtypeStruct((bsz, s_out, LANES), F32)]
    out_specs = [pl.BlockSpec((1, tm, d), tok), pl.BlockSpec((1, tm, d), tok),
                 pl.BlockSpec((1, tm, LANES), tok)]
    return pl.pallas_call(
        functools.partial(_post_kernel, mode=mode),
        out_shape=out_shape, grid=(bsz, n_tiles), in_specs=in_specs, out_specs=out_specs,
        compiler_params=_params(2), name="post_" + mode,
    )(x, *acts, *rows, *mats, mod_l, norm2_w.reshape(1, d), w_router, b_router)


def _router_kernel(lg_ref, route_ref, cnt_ref, all_ref, off_ref, *, moe_tile):
    phase = pl.program_id(0)
    i = pl.program_id(1)
    tm = lg_ref.shape[0]
    rows = pl.ds(pl.multiple_of(i * tm, tm), tm)

    @pl.when((phase == 0) & (i == 0))
    def _():
        cnt_ref[...] = jnp.zeros_like(cnt_ref)

    @pl.when(phase == 0)
    def _():
        all_ref[rows, :] = _route_fields(lg_ref[...], cnt_ref)

    @pl.when((phase == 1) & (i == 0))
    def _():
        tiles = jnp.ceil(cnt_ref[...] * (1.0 / moe_tile))
        r = lax.broadcasted_iota(jnp.int32, (LANES, LANES), 0)
        c = lax.broadcasted_iota(jnp.int32, (LANES, LANES), 1)
        earlier = jnp.where(r < c, 1.0, 0.0).astype(BF16)
        off_ref[...] = _dot(tiles.astype(BF16), earlier) * float(moe_tile)

    @pl.when(phase == 1)
    def _():
        f = all_ref[rows, :]
        lane = lax.broadcasted_iota(jnp.int32, f.shape, 1).astype(F32)
        off = off_ref[0:1, :]
        pos1 = jnp.sum(jnp.where(lane == f[:, 0:1], off, 0.0), axis=-1, keepdims=True) + f[:, 4:5]
        pos2 = jnp.sum(jnp.where(lane == f[:, 1:2], off, 0.0), axis=-1, keepdims=True) + f[:, 5:6]
        route_ref[...] = jnp.where(lane == 6.0, pos1, jnp.where(lane == 7.0, pos2, f))


def _route_fields(lg, cnt_ref):
    tm = lg.shape[0]
    lane = lax.broadcasted_iota(jnp.int32, lg.shape, 1).astype(F32)
    big = float(LANES)
    is_g = (lane >= ROUTE_G) & (lane < ROUTE_G + MOE_GROUPS)
    lgg = jnp.where(is_g, lg, -jnp.inf)
    g_max = jnp.max(lgg, axis=-1, keepdims=True)
    g_idx = jnp.min(jnp.where(lgg == g_max, lane - ROUTE_G, big), axis=-1, keepdims=True)
    g_prob = 1.0 / jnp.sum(jnp.exp(lgg - g_max), axis=-1, keepdims=True)
    lo = g_idx * MOE_EXPERTS
    le = jnp.where((lane >= lo) & (lane < lo + MOE_EXPERTS), lg, -jnp.inf)
    l1 = jnp.max(le, axis=-1, keepdims=True)
    i1 = jnp.min(jnp.where(le == l1, lane, big), axis=-1, keepdims=True)
    le2 = jnp.where(lane == i1, -jnp.inf, le)
    l2 = jnp.max(le2, axis=-1, keepdims=True)
    i2 = jnp.min(jnp.where(le2 == l2, lane, big), axis=-1, keepdims=True)
    r = jnp.exp(l2 - l1)
    w1 = g_prob / (1.0 + r)
    w2 = w1 * r
    oh1 = jnp.where(lane == i1, 1.0, 0.0)
    oh2 = jnp.where(lane == i2, 1.0, 0.0)
    oh = oh1 + oh2
    row = lax.broadcasted_iota(jnp.int32, (tm, tm), 0)
    col = lax.broadcasted_iota(jnp.int32, (tm, tm), 1)
    before = jnp.where(col < row, 1.0, 0.0).astype(BF16)
    prefix = _dot(before, oh.astype(BF16)) + cnt_ref[0:1, :]
    rank1 = jnp.sum(prefix * oh1, axis=-1, keepdims=True)
    rank2 = jnp.sum(prefix * oh2, axis=-1, keepdims=True)
    cnt_ref[...] = cnt_ref[...] + jnp.sum(oh, axis=0, keepdims=True)
    fields = (i1, i2, w1, w2, rank1, rank2)
    out = jnp.zeros_like(lg)
    for j, f in enumerate(fields):
        out = jnp.where(lane == float(j), f, out)
    return out


def _router(logits):
    t = logits.shape[0]
    tm = ROW_TILE
    return pl.pallas_call(
        functools.partial(_router_kernel, moe_tile=MOE_TILE),
        out_shape=[jax.ShapeDtypeStruct((t, LANES), F32), jax.ShapeDtypeStruct((8, LANES), F32)],
        grid=(2, t // tm),
        in_specs=[pl.BlockSpec((tm, LANES), lambda p, i: (i * (1 - p), 0))],
        out_specs=[pl.BlockSpec((tm, LANES), lambda p, i: (i * p, 0)), pl.BlockSpec((8, LANES), lambda p, i: (0, 0))],
        scratch_shapes=[pltpu.VMEM((t, LANES), F32), pltpu.VMEM((8, LANES), F32)],
        compiler_params=_params(2), name="router",
    )(logits)


def _moe_kernel(te_ref, nt_ref, x_ref, wg_ref, wu_ref, wd_ref, o_ref, wgb, wub, wdb):
    i = pl.program_id(0)
    valid = i < nt_ref[0]
    fresh = (i == 0) | (te_ref[i] != te_ref[jnp.maximum(i - 1, 0)])

    @pl.when(valid & fresh)
    def _():
        wgb[...] = wg_ref[0].astype(BF16)
        wub[...] = wu_ref[0].astype(BF16)
        wdb[...] = wd_ref[0].astype(BF16)

    @pl.when(valid)
    def _():
        x = x_ref[...].astype(BF16)
        act = _silu(_dot(x, wgb[...])) * _dot(x, wub[...])
        o_ref[...] = _dot(act.astype(BF16), wdb[...]).astype(o_ref.dtype)

    @pl.when(jnp.logical_not(valid))
    def _():
        o_ref[...] = jnp.zeros_like(o_ref)


def _moe_experts(x_sorted, tile_expert, n_tiles_used, wg, wu, wd):
    tm = MOE_TILE
    rows, d = x_sorted.shape
    ff = wg.shape[2]
    grid_spec = pltpu.PrefetchScalarGridSpec(
        num_scalar_prefetch=2, grid=(rows // tm,),
        in_specs=[pl.BlockSpec((tm, d), lambda i, te, nt: (i, 0)),
                  pl.BlockSpec((1, d, ff), lambda i, te, nt: (te[i], 0, 0)),
                  pl.BlockSpec((1, d, ff), lambda i, te, nt: (te[i], 0, 0)),
                  pl.BlockSpec((1, ff, d), lambda i, te, nt: (te[i], 0, 0))],
        out_specs=pl.BlockSpec((tm, d), lambda i, te, nt: (i, 0)),
        scratch_shapes=[pltpu.VMEM((d, ff), BF16), pltpu.VMEM((d, ff), BF16), pltpu.VMEM((ff, d), BF16)])
    return pl.pallas_call(
        _moe_kernel, out_shape=jax.ShapeDtypeStruct((rows, d), F32), grid_spec=grid_spec,
        compiler_params=_params(1), name="moe_experts",
    )(tile_expert, n_tiles_used, x_sorted, wg, wu, wd)


def _combine_kernel(x_ref, y1_ref, y2_ref, rt_ref, mod_ref, o_ref):
    f = rt_ref[0, :, 2:3] * y1_ref[0] + rt_ref[0, :, 3:4] * y2_ref[0]
    o_ref[0] = x_ref[0] + mod_ref[0, 5:6, :] * f


def _combine(x, y1, y2, route, mod_l, ctx_tiles, ctx_row):
    bsz, s, d = x.shape
    tm = ROW_TILE
    tok = lambda b, i: (b, i, 0)
    mod_idx = lambda b, i: (jnp.where(i < ctx_tiles, ctx_row, b), 0, 0)
    return pl.pallas_call(
        _combine_kernel, out_shape=jax.ShapeDtypeStruct(x.shape, F32), grid=(bsz, s // tm),
        in_specs=[pl.BlockSpec((1, tm, d), tok), pl.BlockSpec((1, tm, d), tok), pl.BlockSpec((1, tm, d), tok),
                  pl.BlockSpec((1, tm, LANES), tok), pl.BlockSpec((1, 6, d), mod_idx)],
        out_specs=pl.BlockSpec((1, tm, d), tok),
        compiler_params=_params(2), name="moe_combine",
    )(x, y1.reshape(x.shape), y2.reshape(x.shape), route.reshape(bsz, s, LANES), mod_l)


def _hier_moe(x, h2, logits, mod_l, layer, wg, wu, wd, ctx_tiles, ctx_row):
    bsz, s, d = h2.shape
    t = bsz * s
    tm = MOE_TILE
    route, counts = _router(logits.reshape(t, LANES))
    n_tiles = 2 * t // tm + N_EXPERTS
    tiles_per = (counts[0, :N_EXPERTS].astype(jnp.int32) + tm - 1) // tm
    tile_end = jnp.cumsum(tiles_per)
    tile_ids = jnp.arange(n_tiles, dtype=jnp.int32)
    tile_expert = jnp.minimum(jnp.sum((tile_end[None, :] <= tile_ids[:, None]).astype(jnp.int32), axis=1),
                              N_EXPERTS - 1) + layer * N_EXPERTS
    pos = route[:, 6:8].astype(jnp.int32)
    tok = jnp.arange(t, dtype=jnp.int32)
    src = jnp.zeros((n_tiles * tm,), jnp.int32).at[jnp.concatenate([pos[:, 0], pos[:, 1]])].set(
        jnp.concatenate([tok, tok]), unique_indices=True)
    x_sorted = jnp.take(h2.reshape(t, d), src, axis=0, mode="clip")
    y_sorted = _moe_experts(x_sorted, tile_expert, tile_end[-1:], wg.reshape(-1, d, wg.shape[-1]),
                            wu.reshape(-1, d, wu.shape[-1]), wd.reshape(-1, wd.shape[-2], d))
    y1 = jnp.take(y_sorted, pos[:, 0], axis=0, mode="clip")
    y2 = jnp.take(y_sorted, pos[:, 1], axis=0, mode="clip")
    return _combine(x, y1, y2, route, mod_l, ctx_tiles, ctx_row)


def _attn_kernel(q_ref, k_ref, v_ref, bias_ref, qw_ref, kw_ref, o_ref, qn_ref, kn_ref, *, ctx_len, rows):
    lane = lax.broadcasted_iota(jnp.int32, (1, LANES), 1)
    first = lane < NA_HEAD_DIM

    def head_norm(x, w):
        x2 = x * x
        s0 = jnp.sum(jnp.where(first, x2, 0.0), axis=-1, keepdims=True)
        s1 = jnp.sum(jnp.where(first, 0.0, x2), axis=-1, keepdims=True)
        ms = jnp.where(first, s0, s1) * (1.0 / NA_HEAD_DIM)
        return x * lax.rsqrt(ms + RMS_EPS) * w

    kn_ref[...] = head_norm(k_ref[0].astype(F32), kw_ref[...]).astype(BF16)
    qn_ref[...] = (head_norm(q_ref[0, ctx_len:, :].astype(F32), qw_ref[...]) * NA_HEAD_DIM ** -0.5).astype(BF16)
    n_groups = rows // ATTN_GROUP_ROWS
    n_q = ATTN_GROUP_ROWS * GRID_W
    n_loc = ATTN_KEY_ROWS * GRID_W
    k_ctx = kn_ref[0:ctx_len, :]
    v_ctx = v_ref[0, 0:ctx_len, :]

    def one_group(g):
        kind = jnp.where(g == 0, 0, jnp.where(g == n_groups - 1, 2, 1))
        kr0 = jnp.clip(g * ATTN_GROUP_ROWS - NA_KH // 2, 0, rows - ATTN_KEY_ROWS)
        q_rows = pl.ds(pl.multiple_of(g * n_q, n_q), n_q)
        q = qn_ref[q_rows, :]
        zero = jnp.zeros_like(q)
        q2 = jnp.concatenate([jnp.where(first, q, zero), jnp.where(first, zero, q)], axis=0)
        k_off = pl.multiple_of(ctx_len + kr0 * GRID_W, GRID_W)
        s_loc = _dot_nt(q2, kn_ref[pl.ds(k_off, n_loc), :])
        s_ctx = _dot_nt(q2, k_ctx)
        p_loc, p_ctx, inv = [], [], []
        for hh in range(2):
            sl = s_loc[hh * n_q:(hh + 1) * n_q] + bias_ref[hh, kind]
            sc = s_ctx[hh * n_q:(hh + 1) * n_q]
            m = jnp.maximum(jnp.max(sl, axis=-1, keepdims=True), jnp.max(sc, axis=-1, keepdims=True))
            el = jnp.exp(sl - m)
            ec = jnp.exp(sc - m)
            inv.append(1.0 / (jnp.sum(el, axis=-1, keepdims=True) + jnp.sum(ec, axis=-1, keepdims=True)))
            p_loc.append(el.astype(BF16))
            p_ctx.append(ec.astype(BF16))
        o = (_dot(jnp.concatenate(p_loc, axis=0), v_ref[0, pl.ds(k_off, n_loc), :])
             + _dot(jnp.concatenate(p_ctx, axis=0), v_ctx))
        o_ref[0, q_rows, :] = jnp.where(first, o[:n_q] * inv[0], o[n_q:] * inv[1]).astype(o_ref.dtype)

    def body(i, carry):
        for j in range(ATTN_GROUPS_PER_TRIP):
            one_group(i * ATTN_GROUPS_PER_TRIP + j)
        return carry

    lax.fori_loop(0, n_groups // ATTN_GROUPS_PER_TRIP, body, 0)


def _attn_group_layout(rows):
    n_groups = rows // ATTN_GROUP_ROWS
    assert rows % ATTN_GROUP_ROWS == 0 and rows >= ATTN_KEY_ROWS and n_groups >= 2
    u = np.arange(ATTN_GROUP_ROWS)[:, None]
    i = np.arange(ATTN_KEY_ROWS)[None, :]

    def layout(g):
        r = g * ATTN_GROUP_ROWS + u
        r0 = np.clip(r - NA_KH // 2, 0, rows - NA_KH)
        kr = np.clip(g * ATTN_GROUP_ROWS - NA_KH // 2, 0, rows - ATTN_KEY_ROWS) + i
        return (kr >= r0) & (kr < r0 + NA_KH), kr - r + NA_KH - 1

    kinds = [layout(0), layout(1), layout(n_groups - 1)]
    for g in range(1, n_groups - 1):
        valid, d = layout(g)
        assert (valid == kinds[1][0]).all() and (d[valid] == kinds[1][1][valid]).all()
    return np.stack([k[0] for k in kinds]), np.stack([k[1] for k in kinds])


def _bias_windows(rpb, rows):
    qc = np.arange(GRID_W)
    c0 = np.clip(qc - NA_KW // 2, 0, GRID_W - NA_KW)
    kc = np.arange(GRID_W)
    inwin = (kc[None, :] >= c0[:, None]) & (kc[None, :] < c0[:, None] + NA_KW)
    coff = kc[None, :] - qc[:, None] + NA_KW - 1
    pick = (coff[..., None] == np.arange(2 * NA_KW - 1)) & inwin[..., None]
    tab = jnp.einsum("hdo,qko->hdqk", rpb.astype(F32), jnp.asarray(pick, F32), precision=HIGHEST)
    tab = jnp.where(inwin[None, None], tab, NEG)
    valid, d = _attn_group_layout(rows)
    win = jnp.where(valid[None, :, :, :, None, None], tab[:, np.clip(d, 0, 2 * NA_KH - 2)], NEG)
    return win.transpose(0, 1, 2, 4, 3, 5).reshape(rpb.shape[0], 3, ATTN_GROUP_ROWS * GRID_W, ATTN_KEY_ROWS * GRID_W)


def _neighbourhood_attention(q, k, v, bias, qn_w, kn_w, ctx_len):
    bsz, s, w = q.shape
    seq = s - ctx_len
    rows = seq // GRID_W
    n_pairs = NA_HEADS // 2
    pair = lambda b, p: (b, 0, p)
    row2 = lambda b, p: (0, 0)
    return pl.pallas_call(
        functools.partial(_attn_kernel, ctx_len=ctx_len, rows=rows),
        out_shape=jax.ShapeDtypeStruct((bsz, seq, w), BF16),
        grid=(bsz, n_pairs),
        in_specs=[pl.BlockSpec((1, s, LANES), pair), pl.BlockSpec((1, s, LANES), pair),
                  pl.BlockSpec((1, s, LANES), pair),
                  pl.BlockSpec((2,) + bias.shape[1:], lambda b, p: (p, 0, 0, 0)),
                  pl.BlockSpec((1, LANES), row2), pl.BlockSpec((1, LANES), row2)],
        out_specs=pl.BlockSpec((1, seq, LANES), pair),
        scratch_shapes=[pltpu.VMEM((seq, LANES), BF16), pltpu.VMEM((s, LANES), BF16)],
        compiler_params=_params(2), name="neighbourhood_attention",
    )(q, k, v, bias, jnp.tile(qn_w, 2).reshape(1, LANES), jnp.tile(kn_w, 2).reshape(1, LANES))


def _pad_row(pieces, width=LANES):
    row = jnp.zeros((width,), F32)
    for off, vec in pieces:
        row = row.at[off:off + vec.shape[0]].set(vec.astype(F32))
    return row.reshape(1, width)


def _router_params(w_group, b_group, w_expert, b_expert):
    d = w_group.shape[0]
    w = jnp.zeros((d, LANES), F32).at[:, :N_EXPERTS].set(w_expert).at[:, ROUTE_G:ROUTE_G + MOE_GROUPS].set(w_group)
    return w, _pad_row([(0, b_expert), (ROUTE_G, b_group)])


def kernel(x, c, ctx, c_ctx, norm1_w, norm2_w, mod_w, mod_b, ab_w_in, ab_conv_w, ab_conv_b, ssd_a_log, ssd_dt_bias, ssd_d, ssd_norm_w, ml_i_bias, ml_f_bias, ml_norm_w, ab_w_out, na_w_qkv, na_q_norm, na_k_norm, na_rpb, na_w_out, moe_w_group, moe_b_group, moe_w_expert, moe_b_expert, moe_w_gate, moe_w_up, moe_w_down):
    bsz, seq, d = x.shape
    ctx_len = ctx.shape[1]
    depth = mod_w.shape[0]
    assert depth == 2 and ctx_len % ROW_TILE == 0 and seq % ROW_TILE == 0 and bsz < 8
    ctx_tiles = ctx_len // ROW_TILE
    lat_tiles = seq // ROW_TILE
    ctx_row = bsz

    cvec = jnp.zeros((8, d), F32).at[:bsz].set(c).at[bsz].set(c_ctx)
    mod = _mod_vectors(cvec, mod_w, mod_b)
    xs = jnp.concatenate([ctx, x], axis=1)

    ssd_w = SSD_HEADS * SSD_HEAD_DIM
    xbc_w = ssd_w + 2 * SSD_GROUPS * SSD_STATE
    qk_w, v_w = ML_HEADS * ML_DK, ML_HEADS * ML_DV
    sizes = (ssd_w, xbc_w, 2 * SSD_HEADS, qk_w, qk_w, v_w, v_w, 2 * ML_HEADS, 2 * ML_HEADS)
    w_z, w_xbc, w_dt, w_q, w_k, w_v, w_o, w_i, w_f = jnp.split(ab_w_in[0], np.cumsum(sizes)[:-1].tolist(), axis=1)
    w_gate = jnp.zeros((d, LANES), F32).at[:, :GATE_F + 2 * ML_HEADS].set(jnp.concatenate([w_dt, w_i, w_f], axis=1))
    weights = [w.astype(BF16) for w in (w_z, w_xbc, w_q, w_k, w_v, w_o)] + [w_gate]
    z, xbc, q, k, v, og, gates = _norm_mod_matmul(xs, norm1_w[0], mod[0], weights, [BF16] * 6 + [F32],
                                                  ctx_tiles, ctx_row)
    xbc = _conv_silu(xbc, ab_conv_w[0], ab_conv_b[0], ctx_len)
    n_ctx = ctx_len // SCAN_CHUNK
    a_neg = -jnp.exp(ssd_a_log[0].astype(F32))
    dsk_row = jnp.repeat(ssd_d[0].astype(F32), SSD_HEAD_DIM).reshape(1, ssd_w)
    y = None
    hm = None
    for dr in range(2):
        rev = dr == 1
        dtb_row = _pad_row([(GATE_DT + dr * SSD_HEADS, ssd_dt_bias[0, dr])])
        a_row = _pad_row([(GATE_DT + dr * SSD_HEADS, a_neg[dr])])
        y = _ssd_scan(xbc, gates, dtb_row, a_row, y if rev else dsk_row, rev=rev, n_ctx=n_ctx)
        ib_row = _pad_row([(GATE_I + dr * ML_HEADS, ml_i_bias[0, dr])])
        fb_row = _pad_row([(GATE_F + dr * ML_HEADS, ml_f_bias[0, dr])])
        hm = _mlstm_scan(q, k, v, gates, ib_row, fb_row, hm, rev=rev, n_ctx=n_ctx)
    w_r, b_r = _router_params(moe_w_group[0], moe_b_group[0], moe_w_expert[0], moe_b_expert[0])
    w_out = ab_w_out[0].astype(BF16)
    x1, h2, logits = _post_call(
        "mix", xs, [y, z, hm, og],
        [ssd_norm_w[0].reshape(1, ssd_w), ml_norm_w[0].reshape(1, v_w)], [w_out[:ssd_w], w_out[ssd_w:]],
        mod[0], norm2_w[0], w_r, b_r, 0, ctx_tiles + lat_tiles, ctx_tiles, ctx_row)
    xs = _hier_moe(x1, h2, logits, mod[0], 0, moe_w_gate, moe_w_up, moe_w_down, ctx_tiles, ctx_row)

    w_qkv = na_w_qkv[0].astype(BF16)
    na_w = NA_HEADS * NA_HEAD_DIM
    q, k, v = _norm_mod_matmul(xs, norm1_w[1], mod[1], [w_qkv[:, :na_w], w_qkv[:, na_w:2 * na_w], w_qkv[:, 2 * na_w:]],
                               [BF16] * 3, ctx_tiles, ctx_row)
    bias = _bias_windows(na_rpb[0], seq // GRID_W)
    attn = _neighbourhood_attention(q, k, v, bias, na_q_norm[0], na_k_norm[0], ctx_len)
    w_r, b_r = _router_params(moe_w_group[1], moe_b_group[1], moe_w_expert[1], moe_b_expert[1])
    x1, h2, logits = _post_call("attn", xs, [attn], [], [na_w_out[0].astype(BF16)],
                                mod[1], norm2_w[1], w_r, b_r, ctx_tiles, lat_tiles, ctx_tiles, ctx_row)
    return _hier_moe(x1, h2, logits, mod[1], 1, moe_w_gate, moe_w_up, moe_w_down, 0, ctx_row)
```

```python
import functools

import numpy as np
import jax
import jax.numpy as jnp
from jax import lax
from jax.experimental import pallas as pl
from jax.experimental.pallas import tpu as pltpu

F32 = jnp.float32
BF16 = jnp.bfloat16
HIGHEST = lax.Precision.HIGHEST

RMS_EPS = 1e-6
GRID_W = 64
SSD_HEADS = 16
SSD_HEAD_DIM = 64
SSD_GROUPS = 2
SSD_STATE = 128
ML_HEADS = 4
ML_DK = 128
ML_DV = 256
NA_HEADS = 16
NA_HEAD_DIM = 64
NA_KH = 8
NA_KW = 16
MOE_GROUPS = 4
MOE_EXPERTS = 8
N_EXPERTS = MOE_GROUPS * MOE_EXPERTS

LANES = 128
ROW_TILE = 256
SCAN_CHUNK = 128
MOE_TILE = 256
ATTN_GROUP_ROWS = 4
ATTN_KEY_ROWS = ATTN_GROUP_ROWS + NA_KH - 1
ATTN_GROUPS_PER_TRIP = 4
VMEM_LIMIT = 56 * 1024 * 1024

GATE_DT = 0
GATE_I = 2 * SSD_HEADS
GATE_F = GATE_I + 2 * ML_HEADS
ROUTE_G = N_EXPERTS
NEG = -1e30


def _params(n_axes):
    return pltpu.CompilerParams(dimension_semantics=("arbitrary",) * n_axes,
                                vmem_limit_bytes=VMEM_LIMIT)


def _silu(x):
    return x * jax.nn.sigmoid(x)


def _softplus(x):
    return jnp.maximum(x, 0.0) + jnp.log1p(jnp.exp(-jnp.abs(x)))


def _rms(x, w):
    return x * lax.rsqrt(jnp.mean(x * x, axis=-1, keepdims=True) + RMS_EPS) * w


def _dot(a, b):
    return jnp.dot(a, b, preferred_element_type=F32)


def _dot_nt(a, b):
    return lax.dot_general(a, b, (((1,), (1,)), ((), ())), preferred_element_type=F32)


def _dot_hi(a, b):
    return jnp.dot(a, b, precision=HIGHEST, preferred_element_type=F32)


def _mod_kernel(c_ref, w_ref, b_ref, o_ref):
    o_ref[0] = _dot_hi(_silu(c_ref[...]), w_ref[0]) + b_ref[0]


def _mod_vectors(cvec, mod_w, mod_b, tn=512):
    depth, d, n = mod_w.shape
    rows = cvec.shape[0]
    out = pl.pallas_call(
        _mod_kernel,
        out_shape=jax.ShapeDtypeStruct((depth, rows, n), F32),
        grid=(depth, n // tn),
        in_specs=[pl.BlockSpec((rows, d), lambda l, j: (0, 0)),
                  pl.BlockSpec((1, d, tn), lambda l, j: (l, 0, j)),
                  pl.BlockSpec((1, 1, tn), lambda l, j: (l, 0, j))],
        out_specs=pl.BlockSpec((1, rows, tn), lambda l, j: (l, 0, j)),
        compiler_params=_params(2),
        name="mod_vectors",
    )(cvec, mod_w, mod_b.reshape(depth, 1, n))
    return out.reshape(depth, rows, 6, d)


def _nmm_kernel(x_ref, nw_ref, mod_ref, *refs, n_out):
    w_refs, o_refs = refs[:n_out], refs[n_out:]
    h = _rms(x_ref[0], nw_ref[...])
    h = h * (1.0 + mod_ref[0, 1:2, :]) + mod_ref[0, 0:1, :]
    hb = h.astype(BF16)
    for w_ref, o_ref in zip(w_refs, o_refs):
        if w_ref.dtype == F32:
            o_ref[0] = _dot_hi(h, w_ref[...])
        else:
            o_ref[0] = _dot(hb, w_ref[...]).astype(o_ref.dtype)


def _norm_mod_matmul(xs, norm_w, mod_l, weights, out_dtypes, ctx_tiles, ctx_row):
    bsz, s, d = xs.shape
    tm = ROW_TILE
    mod_idx = lambda b, i: (jnp.where(i < ctx_tiles, ctx_row, b), 0, 0)
    in_specs = [pl.BlockSpec((1, tm, d), lambda b, i: (b, i, 0)),
                pl.BlockSpec((1, d), lambda b, i: (0, 0)),
                pl.BlockSpec((1, 6, d), mod_idx)]
    in_specs += [pl.BlockSpec(w.shape, lambda b, i: (0, 0)) for w in weights]
    out_shape = [jax.ShapeDtypeStruct((bsz, s, w.shape[1]), dt) for w, dt in zip(weights, out_dtypes)]
    out_specs = [pl.BlockSpec((1, tm, w.shape[1]), lambda b, i: (b, i, 0)) for w in weights]
    return pl.pallas_call(
        functools.partial(_nmm_kernel, n_out=len(weights)),
        out_shape=out_shape, grid=(bsz, s // tm), in_specs=in_specs, out_specs=out_specs,
        compiler_params=_params(2), name="norm_mod_matmul",
    )(xs, norm_w.reshape(1, d), mod_l, *weights)


def _conv_kernel(u_ref, w_ref, b_ref, o_ref, *, ctx_len):
    u = u_ref[0].astype(F32)
    s = u.shape[0]
    t = lax.broadcasted_iota(jnp.int32, u.shape, 0)
    prev = jnp.where((t == 0) | (t == ctx_len), 0.0, pltpu.roll(u, 1, axis=0))
    nxt = jnp.where((t == ctx_len - 1) | (t == s - 1), 0.0, pltpu.roll(u, s - 1, axis=0))
    y = prev * w_ref[0:1, :] + u * w_ref[1:2, :] + nxt * w_ref[2:3, :] + b_ref[...]
    o_ref[0] = _silu(y).astype(o_ref.dtype)


def _conv_silu(u, conv_w, conv_b, ctx_len, tc=LANES):
    bsz, s, ch = u.shape
    return pl.pallas_call(
        functools.partial(_conv_kernel, ctx_len=ctx_len),
        out_shape=jax.ShapeDtypeStruct(u.shape, u.dtype),
        grid=(bsz, ch // tc),
        in_specs=[pl.BlockSpec((1, s, tc), lambda b, j: (b, 0, j)),
                  pl.BlockSpec((3, tc), lambda b, j: (0, j)),
                  pl.BlockSpec((1, tc), lambda b, j: (0, j))],
        out_specs=pl.BlockSpec((1, s, tc), lambda b, j: (b, 0, j)),
        compiler_params=_params(2), name="conv_silu",
    )(u, conv_w, conv_b.reshape(1, ch))


def _scan_chunk_index(c, rev, n_ctx, n_all):
    if not rev:
        return c
    return jnp.where(c < n_ctx, n_ctx - 1 - c, n_ctx + n_all - 1 - c)


def _tri(n, rev):
    row = lax.broadcasted_iota(jnp.int32, (n, n), 0)
    col = lax.broadcasted_iota(jnp.int32, (n, n), 1)
    return (col >= row) if rev else (col <= row)


def _ssd_kernel(*refs, rev, off):
    if rev:
        xs_ref, bc_ref, g_ref, dtb_ref, a_ref, acc_ref, o_ref, st_ref = refs
    else:
        xs_ref, bc_ref, g_ref, dtb_ref, a_ref, dsk_ref, o_ref, st_ref = refs

    @pl.when(pl.program_id(1) == 0)
    def _():
        st_ref[...] = jnp.zeros_like(st_ref)

    n = xs_ref.shape[1]
    last = 0 if rev else n - 1
    tri = _tri(n, rev)
    dt = _softplus(g_ref[0] + dtb_ref[...])
    log_a = dt * a_ref[...]
    cs = _dot_hi(tri.astype(F32), log_a)
    cs_t = cs.T
    xs = xs_ref[0].astype(F32)
    gw = SSD_GROUPS * SSD_STATE
    heads_per_group = SSD_HEADS // SSD_GROUPS
    for g in range(SSD_GROUPS):
        b_g = bc_ref[0, :, g * SSD_STATE:(g + 1) * SSD_STATE]
        c_g = bc_ref[0, :, gw + g * SSD_STATE:gw + (g + 1) * SSD_STATE]
        cb = _dot_nt(c_g, b_g)
        b_t = b_g.astype(F32).T
        for e in range(heads_per_group):
            h = g * heads_per_group + e
            col = off + h
            a_col = cs[:, col:col + 1]
            a_row = cs_t[col:col + 1, :]
            tot = cs_t[col:col + 1, last:last + 1]
            decay = jnp.exp(jnp.where(tri, a_col - a_row, -jnp.inf))
            p0 = h * SSD_HEAD_DIM
            xdt = (xs[:, p0:p0 + SSD_HEAD_DIM] * dt[:, col:col + 1]).astype(BF16)
            state = st_ref[h]
            y = _dot((cb * decay).astype(BF16), xdt)
            y = y + _dot(c_g, state.astype(BF16)) * jnp.exp(a_col)
            bw_t = (b_t * jnp.exp(tot - a_row)).astype(BF16)
            st_ref[h] = state * jnp.exp(tot) + _dot(bw_t, xdt)
            if rev:
                y = y + acc_ref[0, :, p0:p0 + SSD_HEAD_DIM].astype(F32)
            else:
                y = y + dsk_ref[:, p0:p0 + SSD_HEAD_DIM] * xs[:, p0:p0 + SSD_HEAD_DIM]
            o_ref[0, :, p0:p0 + SSD_HEAD_DIM] = y.astype(o_ref.dtype)


def _ssd_scan(xbc, gates, dtb_row, a_row, extra, *, rev, n_ctx):
    bsz, s, _ = xbc.shape
    n = SCAN_CHUNK
    n_all = s // n
    w = SSD_HEADS * SSD_HEAD_DIM
    bcw = 2 * SSD_GROUPS * SSD_STATE
    cidx = functools.partial(_scan_chunk_index, rev=rev, n_ctx=n_ctx, n_all=n_all)
    tok = lambda b, c: (b, cidx(c), 0)
    in_specs = [pl.BlockSpec((1, n, w), tok),
                pl.BlockSpec((1, n, bcw), lambda b, c: (b, cidx(c), w // bcw)),
                pl.BlockSpec((1, n, LANES), tok),
                pl.BlockSpec((1, LANES), lambda b, c: (0, 0)),
                pl.BlockSpec((1, LANES), lambda b, c: (0, 0))]
    if rev:
        in_specs.append(pl.BlockSpec((1, n, w), tok))
    else:
        in_specs.append(pl.BlockSpec((1, w), lambda b, c: (0, 0)))
    return pl.pallas_call(
        functools.partial(_ssd_kernel, rev=rev, off=GATE_DT + (SSD_HEADS if rev else 0)),
        out_shape=jax.ShapeDtypeStruct((bsz, s, w), BF16),
        grid=(bsz, n_all), in_specs=in_specs,
        out_specs=pl.BlockSpec((1, n, w), tok),
        scratch_shapes=[pltpu.VMEM((SSD_HEADS, SSD_STATE, SSD_HEAD_DIM), F32)],
        compiler_params=_params(2), name="ssd_scan_bwd" if rev else "ssd_scan_fwd",
    )(xbc, xbc, gates, dtb_row, a_row, extra)


def _mlstm_kernel(*refs, rev, d):
    if rev:
        q_ref, k_ref, v_ref, g_ref, ib_ref, fb_ref, acc_ref, o_ref, c_st, n_st, m_st = refs
    else:
        q_ref, k_ref, v_ref, g_ref, ib_ref, fb_ref, o_ref, c_st, n_st, m_st = refs

    @pl.when(pl.program_id(1) == 0)
    def _():
        c_st[...] = jnp.zeros_like(c_st)
        n_st[...] = jnp.zeros_like(n_st)
        m_st[...] = jnp.zeros_like(m_st)

    n = q_ref.shape[1]
    last = 0 if rev else n - 1
    tri = _tri(n, rev)
    g = g_ref[0]
    log_i = g + ib_ref[...]
    log_f = -_softplus(-(g + fb_ref[...]))
    cs = _dot_hi(tri.astype(F32), log_f)
    cs_t = cs.T
    li_t = log_i.T
    for h in range(ML_HEADS):
        ci = GATE_I + ML_HEADS * d + h
        cf = GATE_F + ML_HEADS * d + h
        b_col = cs[:, cf:cf + 1]
        b_row = cs_t[cf:cf + 1, :]
        li_row = li_t[ci:ci + 1, :]
        tot = cs_t[cf:cf + 1, last:last + 1]
        dmat = jnp.where(tri, b_col - b_row + li_row, -jnp.inf)
        m_intra = jnp.max(dmat, axis=-1, keepdims=True)
        a_end = tot - b_row + li_row
        m_loc = jnp.max(a_end, axis=-1, keepdims=True)
        w_end = jnp.exp(a_end - m_loc)
        m_prev = m_st[h, 0:1, 0:1]
        m_inter = b_col + m_prev
        m_t = jnp.maximum(m_intra, m_inter)
        w_inter = jnp.exp(m_inter - m_t)
        qh = q_ref[0, :, h * ML_DK:(h + 1) * ML_DK]
        kf = k_ref[0, :, h * ML_DK:(h + 1) * ML_DK].astype(F32) * (ML_DK ** -0.5)
        kb = kf.astype(BF16)
        vh = v_ref[0, :, h * ML_DV:(h + 1) * ML_DV]
        s_mat = _dot_nt(qh, kb) * jnp.exp(dmat - m_t)
        c_prev = c_st[h]
        n_prev = n_st[h, 0:1, :]
        num = _dot(s_mat.astype(BF16), vh) + _dot(qh, c_prev.astype(BF16)) * w_inter
        den = (jnp.sum(s_mat, axis=-1, keepdims=True)
               + jnp.sum(qh.astype(F32) * n_prev, axis=-1, keepdims=True) * w_inter)
        den = jnp.maximum(jnp.abs(den), jnp.exp(-m_t))
        out = num / den
        kw_t = (kf.T * w_end).astype(BF16)
        c_chunk = _dot(kw_t, vh)
        n_chunk = _dot(jnp.broadcast_to(w_end, (8, n)).astype(BF16), kb)
        m_new = jnp.maximum(tot + m_prev, m_loc)
        a_sc = jnp.exp(tot + m_prev - m_new)
        b_sc = jnp.exp(m_loc - m_new)
        c_st[h] = c_prev * a_sc + c_chunk * b_sc
        n_st[h] = n_st[h] * a_sc + n_chunk * b_sc
        m_st[h] = jnp.broadcast_to(m_new, m_st.shape[1:])
        if rev:
            out = out + acc_ref[0, :, h * ML_DV:(h + 1) * ML_DV].astype(F32)
        o_ref[0, :, h * ML_DV:(h + 1) * ML_DV] = out.astype(o_ref.dtype)


def _mlstm_scan(q, k, v, gates, ib_row, fb_row, acc, *, rev, n_ctx):
    bsz, s, _ = q.shape
    n = SCAN_CHUNK
    n_all = s // n
    cidx = functools.partial(_scan_chunk_index, rev=rev, n_ctx=n_ctx, n_all=n_all)
    tok = lambda b, c: (b, cidx(c), 0)
    qw, vw = ML_HEADS * ML_DK, ML_HEADS * ML_DV
    in_specs = [pl.BlockSpec((1, n, qw), tok), pl.BlockSpec((1, n, qw), tok),
                pl.BlockSpec((1, n, vw), tok), pl.BlockSpec((1, n, LANES), tok),
                pl.BlockSpec((1, LANES), lambda b, c: (0, 0)),
                pl.BlockSpec((1, LANES), lambda b, c: (0, 0))]
    args = [q, k, v, gates, ib_row, fb_row]
    if rev:
        in_specs.append(pl.BlockSpec((1, n, vw), tok))
        args.append(acc)
    return pl.pallas_call(
        functools.partial(_mlstm_kernel, rev=rev, d=1 if rev else 0),
        out_shape=jax.ShapeDtypeStruct((bsz, s, vw), F32),
        grid=(bsz, n_all), in_specs=in_specs,
        out_specs=pl.BlockSpec((1, n, vw), tok),
        scratch_shapes=[pltpu.VMEM((ML_HEADS, ML_DK, ML_DV), F32),
                        pltpu.VMEM((ML_HEADS, 8, ML_DK), F32),
                        pltpu.VMEM((ML_HEADS, 8, LANES), F32)],
        compiler_params=_params(2), name="mlstm_scan_bwd" if rev else "mlstm_scan_fwd",
    )(*args)


def _post_kernel(*refs, mode):
    if mode == "mix":
        (x_ref, y_ref, z_ref, hm_ref, og_ref, snw_ref, mnw_ref, wa_ref, wb_ref,
         mod_ref, n2_ref, wr_ref, br_ref, xo_ref, h2_ref, lg_ref) = refs
        y = _rms(y_ref[0].astype(F32) * _silu(z_ref[0].astype(F32)), snw_ref[...])
        og = og_ref[0].astype(F32)
        o = _dot(y.astype(BF16), wa_ref[...])
        parts = []
        for h in range(ML_HEADS):
            sl = slice(h * ML_DV, (h + 1) * ML_DV)
            parts.append(_rms(hm_ref[0, :, sl], mnw_ref[:, sl]) * jax.nn.sigmoid(og[:, sl]))
        o = o + _dot(jnp.concatenate(parts, axis=-1).astype(BF16), wb_ref[...])
    else:
        (x_ref, a_ref, wa_ref, mod_ref, n2_ref, wr_ref, br_ref, xo_ref, h2_ref, lg_ref) = refs
        o = _dot(a_ref[0], wa_ref[...])
    x_new = x_ref[0] + mod_ref[0, 2:3, :] * o
    xo_ref[0] = x_new
    h2 = _rms(x_new, n2_ref[...]) * (1.0 + mod_ref[0, 4:5, :]) + mod_ref[0, 3:4, :]
    h2_ref[0] = h2
    lg_ref[0] = _dot_hi(h2, wr_ref[...]) + br_ref[...]


def _post_call(mode, x, acts, rows, mats, mod_l, norm2_w, w_router, b_router, x_tile_off, n_tiles, ctx_tiles, ctx_row):
    bsz, _, d = x.shape
    tm = ROW_TILE
    s_out = n_tiles * tm
    tok = lambda b, i: (b, i, 0)
    const = lambda b, i: (0, 0)
    mod_idx = lambda b, i: (jnp.where(i + x_tile_off < ctx_tiles, ctx_row, b), 0, 0)
    in_specs = [pl.BlockSpec((1, tm, d), lambda b, i: (b, i + x_tile_off, 0))]
    in_specs += [pl.BlockSpec((1, tm, a.shape[2]), tok) for a in acts]
    in_specs += [pl.BlockSpec(r.shape, const) for r in rows]
    in_specs += [pl.BlockSpec(m.shape, const) for m in mats]
    in_specs += [pl.BlockSpec((1, 6, d), mod_idx), pl.BlockSpec((1, d), const),
                 pl.BlockSpec(w_router.shape, const), pl.BlockSpec((1, LANES), const)]
    out_shape = [jax.ShapeDtypeStruct((bsz, s_out, d), F32),
                 jax.ShapeDtypeStruct((bsz, s_out, d), F32),
                 jax.ShapeDtypeStruct((bsz, s_out, LANES), F32)]
    out_specs = [pl.BlockSpec((1, tm, d), tok), pl.BlockSpec((1, tm, d), tok),
                 pl.BlockSpec((1, tm, LANES), tok)]
    return pl.pallas_call(
        functools.partial(_post_kernel, mode=mode),
        out_shape=out_shape, grid=(bsz, n_tiles), in_specs=in_specs, out_specs=out_specs,
        compiler_params=_params(2), name="post_" + mode,
    )(x, *acts, *rows, *mats, mod_l, norm2_w.reshape(1, d), w_router, b_router)


def _router_kernel(lg_ref, route_ref, cnt_ref, all_ref, off_ref, *, moe_tile):
    phase = pl.program_id(0)
    i = pl.program_id(1)
    tm = lg_ref.shape[0]
    rows = pl.ds(pl.multiple_of(i * tm, tm), tm)

    @pl.when((phase == 0) & (i == 0))
    def _():
        cnt_ref[...] = jnp.zeros_like(cnt_ref)

    @pl.when(phase == 0)
    def _():
        all_ref[rows, :] = _route_fields(lg_ref[...], cnt_ref)

    @pl.when((phase == 1) & (i == 0))
    def _():
        tiles = jnp.ceil(cnt_ref[...] * (1.0 / moe_tile))
        r = lax.broadcasted_iota(jnp.int32, (LANES, LANES), 0)
        c = lax.broadcasted_iota(jnp.int32, (LANES, LANES), 1)
        earlier = jnp.where(r < c, 1.0, 0.0).astype(BF16)
        off_ref[...] = _dot(tiles.astype(BF16), earlier) * float(moe_tile)

    @pl.when(phase == 1)
    def _():
        f = all_ref[rows, :]
        lane = lax.broadcasted_iota(jnp.int32, f.shape, 1).astype(F32)
        off = off_ref[0:1, :]
        pos1 = jnp.sum(jnp.where(lane == f[:, 0:1], off, 0.0), axis=-1, keepdims=True) + f[:, 4:5]
        pos2 = jnp.sum(jnp.where(lane == f[:, 1:2], off, 0.0), axis=-1, keepdims=True) + f[:, 5:6]
        route_ref[...] = jnp.where(lane == 6.0, pos1, jnp.where(lane == 7.0, pos2, f))


def _route_fields(lg, cnt_ref):
    tm = lg.shape[0]
    lane = lax.broadcasted_iota(jnp.int32, lg.shape, 1).astype(F32)
    big = float(LANES)
    is_g = (lane >= ROUTE_G) & (lane < ROUTE_G + MOE_GROUPS)
    lgg = jnp.where(is_g, lg, -jnp.inf)
    g_max = jnp.max(lgg, axis=-1, keepdims=True)
    g_idx = jnp.min(jnp.where(lgg == g_max, lane - ROUTE_G, big), axis=-1, keepdims=True)
    g_prob = 1.0 / jnp.sum(jnp.exp(lgg - g_max), axis=-1, keepdims=True)
    lo = g_idx * MOE_EXPERTS
    le = jnp.where((lane >= lo) & (lane < lo + MOE_EXPERTS), lg, -jnp.inf)
    l1 = jnp.max(le, axis=-1, keepdims=True)
    i1 = jnp.min(jnp.where(le == l1, lane, big), axis=-1, keepdims=True)
    le2 = jnp.where(lane == i1, -jnp.inf, le)
    l2 = jnp.max(le2, axis=-1, keepdims=True)
    i2 = jnp.min(jnp.where(le2 == l2, lane, big), axis=-1, keepdims=True)
    r = jnp.exp(l2 - l1)
    w1 = g_prob / (1.0 + r)
    w2 = w1 * r
    oh1 = jnp.where(lane == i1, 1.0, 0.0)
    oh2 = jnp.where(lane == i2, 1.0, 0.0)
    oh = oh1 + oh2
    row = lax.broadcasted_iota(jnp.int32, (tm, tm), 0)
    col = lax.broadcasted_iota(jnp.int32, (tm, tm), 1)
    before = jnp.where(col < row, 1.0, 0.0).astype(BF16)
    prefix = _dot(before, oh.astype(BF16)) + cnt_ref[0:1, :]
    rank1 = jnp.sum(prefix * oh1, axis=-1, keepdims=True)
    rank2 = jnp.sum(prefix * oh2, axis=-1, keepdims=True)
    cnt_ref[...] = cnt_ref[...] + jnp.sum(oh, axis=0, keepdims=True)
    fields = (i1, i2, w1, w2, rank1, rank2)
    out = jnp.zeros_like(lg)
    for j, f in enumerate(fields):
        out = jnp.where(lane == float(j), f, out)
    return out


def _router(logits):
    t = logits.shape[0]
    tm = ROW_TILE
    return pl.pallas_call(
        functools.partial(_router_kernel, moe_tile=MOE_TILE),
        out_shape=[jax.ShapeDtypeStruct((t, LANES), F32), jax.ShapeDtypeStruct((8, LANES), F32)],
        grid=(2, t // tm),
        in_specs=[pl.BlockSpec((tm, LANES), lambda p, i: (i * (1 - p), 0))],
        out_specs=[pl.BlockSpec((tm, LANES), lambda p, i: (i * p, 0)), pl.BlockSpec((8, LANES), lambda p, i: (0, 0))],
        scratch_shapes=[pltpu.VMEM((t, LANES), F32), pltpu.VMEM((8, LANES), F32)],
        compiler_params=_params(2), name="router",
    )(logits)


def _moe_kernel(te_ref, nt_ref, x_ref, wg_ref, wu_ref, wd_ref, o_ref, wgb, wub, wdb):
    i = pl.program_id(0)
    valid = i < nt_ref[0]
    fresh = (i == 0) | (te_ref[i] != te_ref[jnp.maximum(i - 1, 0)])

    @pl.when(valid & fresh)
    def _():
        wgb[...] = wg_ref[0].astype(BF16)
        wub[...] = wu_ref[0].astype(BF16)
        wdb[...] = wd_ref[0].astype(BF16)

    @pl.when(valid)
    def _():
        x = x_ref[...].astype(BF16)
        act = _silu(_dot(x, wgb[...])) * _dot(x, wub[...])
        o_ref[...] = _dot(act.astype(BF16), wdb[...]).astype(o_ref.dtype)

    @pl.when(jnp.logical_not(valid))
    def _():
        o_ref[...] = jnp.zeros_like(o_ref)


def _moe_experts(x_sorted, tile_expert, n_tiles_used, wg, wu, wd):
    tm = MOE_TILE
    rows, d = x_sorted.shape
    ff = wg.shape[2]
    grid_spec = pltpu.PrefetchScalarGridSpec(
        num_scalar_prefetch=2, grid=(rows // tm,),
        in_specs=[pl.BlockSpec((tm, d), lambda i, te, nt: (i, 0)),
                  pl.BlockSpec((1, d, ff), lambda i, te, nt: (te[i], 0, 0)),
                  pl.BlockSpec((1, d, ff), lambda i, te, nt: (te[i], 0, 0)),
                  pl.BlockSpec((1, ff, d), lambda i, te, nt: (te[i], 0, 0))],
        out_specs=pl.BlockSpec((tm, d), lambda i, te, nt: (i, 0)),
        scratch_shapes=[pltpu.VMEM((d, ff), BF16), pltpu.VMEM((d, ff), BF16), pltpu.VMEM((ff, d), BF16)])
    return pl.pallas_call(
        _moe_kernel, out_shape=jax.ShapeDtypeStruct((rows, d), F32), grid_spec=grid_spec,
        compiler_params=_params(1), name="moe_experts",
    )(tile_expert, n_tiles_used, x_sorted, wg, wu, wd)


def _combine_kernel(x_ref, y1_ref, y2_ref, rt_ref, mod_ref, o_ref):
    f = rt_ref[0, :, 2:3] * y1_ref[0] + rt_ref[0, :, 3:4] * y2_ref[0]
    o_ref[0] = x_ref[0] + mod_ref[0, 5:6, :] * f


def _combine(x, y1, y2, route, mod_l, ctx_tiles, ctx_row):
    bsz, s, d = x.shape
    tm = ROW_TILE
    tok = lambda b, i: (b, i, 0)
    mod_idx = lambda b, i: (jnp.where(i < ctx_tiles, ctx_row, b), 0, 0)
    return pl.pallas_call(
        _combine_kernel, out_shape=jax.ShapeDtypeStruct(x.shape, F32), grid=(bsz, s // tm),
        in_specs=[pl.BlockSpec((1, tm, d), tok), pl.BlockSpec((1, tm, d), tok), pl.BlockSpec((1, tm, d), tok),
                  pl.BlockSpec((1, tm, LANES), tok), pl.BlockSpec((1, 6, d), mod_idx)],
        out_specs=pl.BlockSpec((1, tm, d), tok),
        compiler_params=_params(2), name="moe_combine",
    )(x, y1.reshape(x.shape), y2.reshape(x.shape), route.reshape(bsz, s, LANES), mod_l)


def _hier_moe(x, h2, logits, mod_l, layer, wg, wu, wd, ctx_tiles, ctx_row):
    bsz, s, d = h2.shape
    t = bsz * s
    tm = MOE_TILE
    route, counts = _router(logits.reshape(t, LANES))
    n_tiles = 2 * t // tm + N_EXPERTS
    tiles_per = (counts[0, :N_EXPERTS].astype(jnp.int32) + tm - 1) // tm
    tile_end = jnp.cumsum(tiles_per)
    tile_ids = jnp.arange(n_tiles, dtype=jnp.int32)
    tile_expert = jnp.minimum(jnp.sum((tile_end[None, :] <= tile_ids[:, None]).astype(jnp.int32), axis=1),
                              N_EXPERTS - 1) + layer * N_EXPERTS
    pos = route[:, 6:8].astype(jnp.int32)
    tok = jnp.arange(t, dtype=jnp.int32)
    src = (jnp.arange(n_tiles * tm, dtype=jnp.int32) % t).at[jnp.concatenate([pos[:, 0], pos[:, 1]])].set(
        jnp.concatenate([tok, tok]), unique_indices=True)
    x_sorted = jnp.take(h2.reshape(t, d), src, axis=0, mode="clip")
    y_sorted = _moe_experts(x_sorted, tile_expert, tile_end[-1:], wg.reshape(-1, d, wg.shape[-1]),
                            wu.reshape(-1, d, wu.shape[-1]), wd.reshape(-1, wd.shape[-2], d))
    y1 = jnp.take(y_sorted, pos[:, 0], axis=0, mode="clip")
    y2 = jnp.take(y_sorted, pos[:, 1], axis=0, mode="clip")
    return _combine(x, y1, y2, route, mod_l, ctx_tiles, ctx_row)


def _attn_kernel(q_ref, k_ref, v_ref, bias_ref, qw_ref, kw_ref, o_ref, qn_ref, kn_ref, *, ctx_len, rows):
    lane = lax.broadcasted_iota(jnp.int32, (1, LANES), 1)
    first = lane < NA_HEAD_DIM

    def head_norm(x, w):
        x2 = x * x
        s0 = jnp.sum(jnp.where(first, x2, 0.0), axis=-1, keepdims=True)
        s1 = jnp.sum(jnp.where(first, 0.0, x2), axis=-1, keepdims=True)
        ms = jnp.where(first, s0, s1) * (1.0 / NA_HEAD_DIM)
        return x * lax.rsqrt(ms + RMS_EPS) * w

    kn_ref[...] = head_norm(k_ref[0].astype(F32), kw_ref[...]).astype(BF16)
    qn_ref[...] = (head_norm(q_ref[0, ctx_len:, :].astype(F32), qw_ref[...]) * NA_HEAD_DIM ** -0.5).astype(BF16)
    n_groups = rows // ATTN_GROUP_ROWS
    n_q = ATTN_GROUP_ROWS * GRID_W
    n_loc = ATTN_KEY_ROWS * GRID_W
    k_ctx = kn_ref[0:ctx_len, :]
    v_ctx = v_ref[0, 0:ctx_len, :]

    def one_group(g):
        kind = jnp.where(g == 0, 0, jnp.where(g == n_groups - 1, 2, 1))
        kr0 = jnp.clip(g * ATTN_GROUP_ROWS - NA_KH // 2, 0, rows - ATTN_KEY_ROWS)
        q_rows = pl.ds(pl.multiple_of(g * n_q, n_q), n_q)
        q = qn_ref[q_rows, :]
        zero = jnp.zeros_like(q)
        q2 = jnp.concatenate([jnp.where(first, q, zero), jnp.where(first, zero, q)], axis=0)
        k_off = pl.multiple_of(ctx_len + kr0 * GRID_W, GRID_W)
        s_loc = _dot_nt(q2, kn_ref[pl.ds(k_off, n_loc), :])
        s_ctx = _dot_nt(q2, k_ctx)
        p_loc, p_ctx, inv = [], [], []
        for hh in range(2):
            sl = s_loc[hh * n_q:(hh + 1) * n_q] + bias_ref[hh, kind]
            sc = s_ctx[hh * n_q:(hh + 1) * n_q]
            m = jnp.maximum(jnp.max(sl, axis=-1, keepdims=True), jnp.max(sc, axis=-1, keepdims=True))
            el = jnp.exp(sl - m)
            ec = jnp.exp(sc - m)
            inv.append(1.0 / (jnp.sum(el, axis=-1, keepdims=True) + jnp.sum(ec, axis=-1, keepdims=True)))
            p_loc.append(el.astype(BF16))
            p_ctx.append(ec.astype(BF16))
        o = (_dot(jnp.concatenate(p_loc, axis=0), v_ref[0, pl.ds(k_off, n_loc), :])
             + _dot(jnp.concatenate(p_ctx, axis=0), v_ctx))
        o_ref[0, q_rows, :] = jnp.where(first, o[:n_q] * inv[0], o[n_q:] * inv[1]).astype(o_ref.dtype)

    def body(i, carry):
        for j in range(ATTN_GROUPS_PER_TRIP):
            one_group(i * ATTN_GROUPS_PER_TRIP + j)
        return carry

    lax.fori_loop(0, n_groups // ATTN_GROUPS_PER_TRIP, body, 0)


def _attn_group_layout(rows):
    n_groups = rows // ATTN_GROUP_ROWS
    assert rows % ATTN_GROUP_ROWS == 0 and rows >= ATTN_KEY_ROWS and n_groups >= 2
    u = np.arange(ATTN_GROUP_ROWS)[:, None]
    i = np.arange(ATTN_KEY_ROWS)[None, :]

    def layout(g):
        r = g * ATTN_GROUP_ROWS + u
        r0 = np.clip(r - NA_KH // 2, 0, rows - NA_KH)
        kr = np.clip(g * ATTN_GROUP_ROWS - NA_KH // 2, 0, rows - ATTN_KEY_ROWS) + i
        return (kr >= r0) & (kr < r0 + NA_KH), kr - r + NA_KH - 1

    kinds = [layout(0), layout(1), layout(n_groups - 1)]
    for g in range(1, n_groups - 1):
        valid, d = layout(g)
        assert (valid == kinds[1][0]).all() and (d[valid] == kinds[1][1][valid]).all()
    return np.stack([k[0] for k in kinds]), np.stack([k[1] for k in kinds])


def _bias_windows(rpb, rows):
    qc = np.arange(GRID_W)
    c0 = np.clip(qc - NA_KW // 2, 0, GRID_W - NA_KW)
    kc = np.arange(GRID_W)
    inwin = (kc[None, :] >= c0[:, None]) & (kc[None, :] < c0[:, None] + NA_KW)
    coff = kc[None, :] - qc[:, None] + NA_KW - 1
    pick = (coff[..., None] == np.arange(2 * NA_KW - 1)) & inwin[..., None]
    tab = jnp.einsum("hdo,qko->hqdk", rpb.astype(F32), jnp.asarray(pick, F32), precision=HIGHEST)
    tab = jnp.where(inwin[None, :, None, :], tab, NEG)
    valid, d = _attn_group_layout(rows)
    blocks = []
    for kind in range(valid.shape[0]):
        for u in range(ATTN_GROUP_ROWS):
            keys = np.nonzero(valid[kind, u])[0]
            lo, hi = int(keys[0]), int(keys[-1])
            d_lo = int(d[kind, u, lo])
            assert (keys == np.arange(lo, hi + 1)).all() and (d[kind, u, lo:hi + 1] == np.arange(d_lo, d_lo + hi + 1 - lo)).all()
            blocks.append(jnp.pad(tab[:, :, d_lo:d_lo + hi + 1 - lo], ((0, 0), (0, 0), (lo, ATTN_KEY_ROWS - 1 - hi), (0, 0)),
                                  constant_values=NEG))
    win = jnp.stack(blocks, axis=1)
    return win.reshape(rpb.shape[0], valid.shape[0], ATTN_GROUP_ROWS * GRID_W, ATTN_KEY_ROWS * GRID_W)


def _neighbourhood_attention(q, k, v, bias, qn_w, kn_w, ctx_len):
    bsz, s, w = q.shape
    seq = s - ctx_len
    rows = seq // GRID_W
    n_pairs = NA_HEADS // 2
    pair = lambda b, p: (b, 0, p)
    row2 = lambda b, p: (0, 0)
    return pl.pallas_call(
        functools.partial(_attn_kernel, ctx_len=ctx_len, rows=rows),
        out_shape=jax.ShapeDtypeStruct((bsz, seq, w), BF16),
        grid=(bsz, n_pairs),
        in_specs=[pl.BlockSpec((1, s, LANES), pair), pl.BlockSpec((1, s, LANES), pair),
                  pl.BlockSpec((1, s, LANES), pair),
                  pl.BlockSpec((2,) + bias.shape[1:], lambda b, p: (p, 0, 0, 0)),
                  pl.BlockSpec((1, LANES), row2), pl.BlockSpec((1, LANES), row2)],
        out_specs=pl.BlockSpec((1, seq, LANES), pair),
        scratch_shapes=[pltpu.VMEM((seq, LANES), BF16), pltpu.VMEM((s, LANES), BF16)],
        compiler_params=_params(2), name="neighbourhood_attention",
    )(q, k, v, bias, jnp.tile(qn_w, 2).reshape(1, LANES), jnp.tile(kn_w, 2).reshape(1, LANES))


def _pad_row(pieces, width=LANES):
    row = jnp.zeros((width,), F32)
    for off, vec in pieces:
        row = row.at[off:off + vec.shape[0]].set(vec.astype(F32))
    return row.reshape(1, width)


def _router_params(w_group, b_group, w_expert, b_expert):
    d = w_group.shape[0]
    w = jnp.zeros((d, LANES), F32).at[:, :N_EXPERTS].set(w_expert).at[:, ROUTE_G:ROUTE_G + MOE_GROUPS].set(w_group)
    return w, _pad_row([(0, b_expert), (ROUTE_G, b_group)])


def kernel(x, c, ctx, c_ctx, norm1_w, norm2_w, mod_w, mod_b, ab_w_in, ab_conv_w, ab_conv_b, ssd_a_log, ssd_dt_bias, ssd_d, ssd_norm_w, ml_i_bias, ml_f_bias, ml_norm_w, ab_w_out, na_w_qkv, na_q_norm, na_k_norm, na_rpb, na_w_out, moe_w_group, moe_b_group, moe_w_expert, moe_b_expert, moe_w_gate, moe_w_up, moe_w_down):
    bsz, seq, d = x.shape
    ctx_len = ctx.shape[1]
    depth = mod_w.shape[0]
    assert depth == 2 and ctx_len % ROW_TILE == 0 and seq % ROW_TILE == 0 and bsz < 8
    ctx_tiles = ctx_len // ROW_TILE
    lat_tiles = seq // ROW_TILE
    ctx_row = bsz

    cvec = jnp.zeros((8, d), F32).at[:bsz].set(c).at[bsz].set(c_ctx)
    mod = _mod_vectors(cvec, mod_w, mod_b)
    xs = jnp.concatenate([ctx, x], axis=1)

    ssd_w = SSD_HEADS * SSD_HEAD_DIM
    xbc_w = ssd_w + 2 * SSD_GROUPS * SSD_STATE
    qk_w, v_w = ML_HEADS * ML_DK, ML_HEADS * ML_DV
    sizes = (ssd_w, xbc_w, 2 * SSD_HEADS, qk_w, qk_w, v_w, v_w, 2 * ML_HEADS, 2 * ML_HEADS)
    w_z, w_xbc, w_dt, w_q, w_k, w_v, w_o, w_i, w_f = jnp.split(ab_w_in[0], np.cumsum(sizes)[:-1].tolist(), axis=1)
    w_gate = jnp.zeros((d, LANES), F32).at[:, :GATE_F + 2 * ML_HEADS].set(jnp.concatenate([w_dt, w_i, w_f], axis=1))
    weights = [w.astype(BF16) for w in (w_z, w_xbc, w_q, w_k, w_v, w_o)] + [w_gate]
    z, xbc, q, k, v, og, gates = _norm_mod_matmul(xs, norm1_w[0], mod[0], weights, [BF16] * 6 + [F32],
                                                  ctx_tiles, ctx_row)
    xbc = _conv_silu(xbc, ab_conv_w[0], ab_conv_b[0], ctx_len)
    n_ctx = ctx_len // SCAN_CHUNK
    a_neg = -jnp.exp(ssd_a_log[0].astype(F32))
    dsk_row = jnp.repeat(ssd_d[0].astype(F32), SSD_HEAD_DIM).reshape(1, ssd_w)
    y = None
    hm = None
    for dr in range(2):
        rev = dr == 1
        dtb_row = _pad_row([(GATE_DT + dr * SSD_HEADS, ssd_dt_bias[0, dr])])
        a_row = _pad_row([(GATE_DT + dr * SSD_HEADS, a_neg[dr])])
        y = _ssd_scan(xbc, gates, dtb_row, a_row, y if rev else dsk_row, rev=rev, n_ctx=n_ctx)
        ib_row = _pad_row([(GATE_I + dr * ML_HEADS, ml_i_bias[0, dr])])
        fb_row = _pad_row([(GATE_F + dr * ML_HEADS, ml_f_bias[0, dr])])
        hm = _mlstm_scan(q, k, v, gates, ib_row, fb_row, hm, rev=rev, n_ctx=n_ctx)
    w_r, b_r = _router_params(moe_w_group[0], moe_b_group[0], moe_w_expert[0], moe_b_expert[0])
    w_out = ab_w_out[0].astype(BF16)
    x1, h2, logits = _post_call(
        "mix", xs, [y, z, hm, og],
        [ssd_norm_w[0].reshape(1, ssd_w), ml_norm_w[0].reshape(1, v_w)], [w_out[:ssd_w], w_out[ssd_w:]],
        mod[0], norm2_w[0], w_r, b_r, 0, ctx_tiles + lat_tiles, ctx_tiles, ctx_row)
    xs = _hier_moe(x1, h2, logits, mod[0], 0, moe_w_gate, moe_w_up, moe_w_down, ctx_tiles, ctx_row)

    w_qkv = na_w_qkv[0].astype(BF16)
    na_w = NA_HEADS * NA_HEAD_DIM
    q, k, v = _norm_mod_matmul(xs, norm1_w[1], mod[1], [w_qkv[:, :na_w], w_qkv[:, na_w:2 * na_w], w_qkv[:, 2 * na_w:]],
                               [BF16] * 3, ctx_tiles, ctx_row)
    bias = _bias_windows(na_rpb[0], seq // GRID_W)
    attn = _neighbourhood_attention(q, k, v, bias, na_q_norm[0], na_k_norm[0], ctx_len)
    w_r, b_r = _router_params(moe_w_group[1], moe_b_group[1], moe_w_expert[1], moe_b_expert[1])
    x1, h2, logits = _post_call("attn", xs, [attn], [], [na_w_out[0].astype(BF16)],
                                mod[1], norm2_w[1], w_r, b_r, ctx_tiles, lat_tiles, ctx_tiles, ctx_row)
    return _hier_moe(x1, h2, logits, mod[1], 1, moe_w_gate, moe_w_up, moe_w_down, 0, ctx_row)
```

```python
import functools

import numpy as np
import jax
import jax.numpy as jnp
from jax import lax
from jax.experimental import pallas as pl
from jax.experimental.pallas import tpu as pltpu

F32 = jnp.float32
BF16 = jnp.bfloat16
HIGHEST = lax.Precision.HIGHEST

RMS_EPS = 1e-6
GRID_W = 64
SSD_HEADS = 16
SSD_HEAD_DIM = 64
SSD_GROUPS = 2
SSD_STATE = 128
ML_HEADS = 4
ML_DK = 128
ML_DV = 256
NA_HEADS = 16
NA_HEAD_DIM = 64
NA_KH = 8
NA_KW = 16
MOE_GROUPS = 4
MOE_EXPERTS = 8
N_EXPERTS = MOE_GROUPS * MOE_EXPERTS

LANES = 128
ROW_TILE = 256
SCAN_CHUNK = 128
MOE_TILE = 256
ATTN_GROUP_ROWS = 4
ATTN_KEY_ROWS = ATTN_GROUP_ROWS + NA_KH - 1
ATTN_GROUPS_PER_TRIP = 4
VMEM_LIMIT = 56 * 1024 * 1024

GATE_DT = 0
GATE_I = 2 * SSD_HEADS
GATE_F = GATE_I + 2 * ML_HEADS
ROUTE_G = N_EXPERTS
NEG = -1e30


def _params(n_axes):
    return pltpu.CompilerParams(dimension_semantics=("arbitrary",) * n_axes,
                                vmem_limit_bytes=VMEM_LIMIT)


def _silu(x):
    return x * jax.nn.sigmoid(x)


def _softplus(x):
    return jnp.maximum(x, 0.0) + jnp.log1p(jnp.exp(-jnp.abs(x)))


def _rms(x, w):
    return x * lax.rsqrt(jnp.mean(x * x, axis=-1, keepdims=True) + RMS_EPS) * w


def _dot(a, b):
    return jnp.dot(a, b, preferred_element_type=F32)


def _dot_nt(a, b):
    return lax.dot_general(a, b, (((1,), (1,)), ((), ())), preferred_element_type=F32)


def _dot_hi(a, b):
    return jnp.dot(a, b, precision=HIGHEST, preferred_element_type=F32)


def _split_bf16(x, terms):
    parts = []
    for _ in range(terms - 1):
        p = x.astype(BF16)
        parts.append(p)
        x = x - p.astype(F32)
    parts.append(x.astype(BF16))
    return parts


def _split_weight(w):
    return jnp.stack(_split_bf16(w.astype(F32), 2))


def _dot_split(a, w2_ref):
    a_hi, a_lo = _split_bf16(a, 2)
    return _dot(a_hi, w2_ref[0]) + _dot(a_lo, w2_ref[0]) + _dot(a_hi, w2_ref[1])


def _cumsum_dot(tri, x):
    tri = jnp.where(tri, 1.0, 0.0).astype(BF16)
    hi, mid, lo = _split_bf16(x, 3)
    return _dot(tri, hi) + _dot(tri, mid) + _dot(tri, lo)


def _mod_kernel(c_ref, w_ref, b_ref, o_ref):
    o_ref[0] = _dot_hi(_silu(c_ref[...]), w_ref[0]) + b_ref[0]


def _mod_vectors(cvec, mod_w, mod_b, tn=512):
    depth, d, n = mod_w.shape
    rows = cvec.shape[0]
    out = pl.pallas_call(
        _mod_kernel,
        out_shape=jax.ShapeDtypeStruct((depth, rows, n), F32),
        grid=(depth, n // tn),
        in_specs=[pl.BlockSpec((rows, d), lambda l, j: (0, 0)),
                  pl.BlockSpec((1, d, tn), lambda l, j: (l, 0, j)),
                  pl.BlockSpec((1, 1, tn), lambda l, j: (l, 0, j))],
        out_specs=pl.BlockSpec((1, rows, tn), lambda l, j: (l, 0, j)),
        compiler_params=_params(2),
        name="mod_vectors",
    )(cvec, mod_w, mod_b.reshape(depth, 1, n))
    return out.reshape(depth, rows, 6, d)


def _nmm_kernel(x_ref, nw_ref, mod_ref, *refs, n_out):
    w_refs, o_refs = refs[:n_out], refs[n_out:]
    h = _rms(x_ref[0], nw_ref[...])
    h = h * (1.0 + mod_ref[0, 1:2, :]) + mod_ref[0, 0:1, :]
    hb = h.astype(BF16)
    for w_ref, o_ref in zip(w_refs, o_refs):
        if len(w_ref.shape) == 3:
            o_ref[0] = _dot_split(h, w_ref)
        else:
            o_ref[0] = _dot(hb, w_ref[...]).astype(o_ref.dtype)


def _norm_mod_matmul(xs, norm_w, mod_l, weights, out_dtypes, ctx_tiles, ctx_row):
    bsz, s, d = xs.shape
    tm = ROW_TILE
    mod_idx = lambda b, i: (jnp.where(i < ctx_tiles, ctx_row, b), 0, 0)
    in_specs = [pl.BlockSpec((1, tm, d), lambda b, i: (b, i, 0)),
                pl.BlockSpec((1, d), lambda b, i: (0, 0)),
                pl.BlockSpec((1, 6, d), mod_idx)]
    in_specs += [pl.BlockSpec(w.shape, lambda b, i, nd=w.ndim: (0,) * nd) for w in weights]
    out_shape = [jax.ShapeDtypeStruct((bsz, s, w.shape[-1]), dt) for w, dt in zip(weights, out_dtypes)]
    out_specs = [pl.BlockSpec((1, tm, w.shape[-1]), lambda b, i: (b, i, 0)) for w in weights]
    return pl.pallas_call(
        functools.partial(_nmm_kernel, n_out=len(weights)),
        out_shape=out_shape, grid=(bsz, s // tm), in_specs=in_specs, out_specs=out_specs,
        compiler_params=_params(2), name="norm_mod_matmul",
    )(xs, norm_w.reshape(1, d), mod_l, *weights)


def _conv_kernel(u_ref, w_ref, b_ref, o_ref, *, ctx_len):
    u = u_ref[0].astype(F32)
    s = u.shape[0]
    t = lax.broadcasted_iota(jnp.int32, u.shape, 0)
    prev = jnp.where((t == 0) | (t == ctx_len), 0.0, pltpu.roll(u, 1, axis=0))
    nxt = jnp.where((t == ctx_len - 1) | (t == s - 1), 0.0, pltpu.roll(u, s - 1, axis=0))
    y = prev * w_ref[0:1, :] + u * w_ref[1:2, :] + nxt * w_ref[2:3, :] + b_ref[...]
    o_ref[0] = _silu(y).astype(o_ref.dtype)


def _conv_silu(u, conv_w, conv_b, ctx_len, tc=LANES):
    bsz, s, ch = u.shape
    return pl.pallas_call(
        functools.partial(_conv_kernel, ctx_len=ctx_len),
        out_shape=jax.ShapeDtypeStruct(u.shape, u.dtype),
        grid=(bsz, ch // tc),
        in_specs=[pl.BlockSpec((1, s, tc), lambda b, j: (b, 0, j)),
                  pl.BlockSpec((3, tc), lambda b, j: (0, j)),
                  pl.BlockSpec((1, tc), lambda b, j: (0, j))],
        out_specs=pl.BlockSpec((1, s, tc), lambda b, j: (b, 0, j)),
        compiler_params=_params(2), name="conv_silu",
    )(u, conv_w, conv_b.reshape(1, ch))


def _scan_chunk_index(c, rev, n_ctx, n_all):
    if not rev:
        return c
    return jnp.where(c < n_ctx, n_ctx - 1 - c, n_ctx + n_all - 1 - c)


def _tri(n, rev):
    row = lax.broadcasted_iota(jnp.int32, (n, n), 0)
    col = lax.broadcasted_iota(jnp.int32, (n, n), 1)
    return (col >= row) if rev else (col <= row)


def _ssd_kernel(*refs, rev, off):
    if rev:
        xs_ref, bc_ref, g_ref, dtb_ref, a_ref, ex_ref, acc_ref, o_ref, st_ref = refs
    else:
        xs_ref, bc_ref, g_ref, dtb_ref, a_ref, ex_ref, dsk_ref, o_ref, st_ref = refs

    @pl.when(pl.program_id(1) == 0)
    def _():
        st_ref[...] = jnp.zeros_like(st_ref)

    n = xs_ref.shape[1]
    last = 0 if rev else n - 1
    tri = _tri(n, rev)
    dt = _softplus(g_ref[0] + dtb_ref[...])
    log_a = dt * a_ref[...]
    cs = _cumsum_dot(tri, log_a)
    cs_t = cs.T
    dt_hi = dt.astype(BF16)
    dt_lo = (dt - dt_hi.astype(F32)).astype(BF16)
    dt_full = _dot(dt_hi, ex_ref[...]) + _dot(dt_lo, ex_ref[...])
    xs = xs_ref[0].astype(F32)
    xdt = (xs * dt_full).astype(BF16)
    lo_half = lax.broadcasted_iota(jnp.int32, (1, LANES), 1) < SSD_HEAD_DIM
    gw = SSD_GROUPS * SSD_STATE
    heads_per_group = SSD_HEADS // SSD_GROUPS
    for g in range(SSD_GROUPS):
        b_g = bc_ref[0, :, g * SSD_STATE:(g + 1) * SSD_STATE]
        c_g = bc_ref[0, :, gw + g * SSD_STATE:gw + (g + 1) * SSD_STATE]
        cb = _dot_nt(c_g, b_g)
        b_t = b_g.astype(F32).T
        for e in range(0, heads_per_group, 2):
            h0 = g * heads_per_group + e
            pair = h0 // 2
            sl = slice(pair * LANES, (pair + 1) * LANES)
            x_pair = xdt[:, sl]
            zero = jnp.zeros_like(x_pair)
            y, upd, a_bc, tots = None, None, [], []
            for j in range(2):
                col = off + h0 + j
                x_j = jnp.where(lo_half, x_pair, zero) if j == 0 else jnp.where(lo_half, zero, x_pair)
                a_b = jnp.broadcast_to(cs[:, col:col + 1], (n, LANES))
                a_row = cs_t[col:col + 1, :]
                tot = cs_t[col:col + 1, last:last + 1]
                decay = jnp.exp(jnp.where(tri, a_b - a_row, -jnp.inf))
                y_j = _dot((cb * decay).astype(BF16), x_j)
                upd_j = _dot((b_t * jnp.exp(tot - a_row)).astype(BF16), x_j)
                y = y_j if y is None else y + y_j
                upd = upd_j if upd is None else upd + upd_j
                a_bc.append(a_b)
                tots.append(tot)
            state = st_ref[pair]
            y = y + _dot(c_g, state.astype(BF16)) * jnp.exp(jnp.where(lo_half, a_bc[0], a_bc[1]))
            st_ref[pair] = state * jnp.exp(jnp.where(lo_half, tots[0], tots[1])) + upd
            if rev:
                y = y + acc_ref[0, :, sl].astype(F32)
            else:
                y = y + dsk_ref[:, sl] * xs[:, sl]
            o_ref[0, :, sl] = y.astype(o_ref.dtype)


def _ssd_scan(xbc, gates, dtb_row, a_row, extra, *, rev, n_ctx):
    bsz, s, _ = xbc.shape
    n = SCAN_CHUNK
    n_all = s // n
    w = SSD_HEADS * SSD_HEAD_DIM
    bcw = 2 * SSD_GROUPS * SSD_STATE
    assert n == LANES and 2 * SSD_HEAD_DIM == LANES and (SSD_HEADS // SSD_GROUPS) % 2 == 0
    off = GATE_DT + (SSD_HEADS if rev else 0)
    cidx = functools.partial(_scan_chunk_index, rev=rev, n_ctx=n_ctx, n_all=n_all)
    tok = lambda b, c: (b, cidx(c), 0)
    const = lambda b, c: (0, 0)
    expand = jnp.asarray(np.arange(LANES)[:, None] == off + np.arange(w)[None, :] // SSD_HEAD_DIM, BF16)
    in_specs = [pl.BlockSpec((1, n, w), tok),
                pl.BlockSpec((1, n, bcw), lambda b, c: (b, cidx(c), w // bcw)),
                pl.BlockSpec((1, n, LANES), tok),
                pl.BlockSpec((1, LANES), const),
                pl.BlockSpec((1, LANES), const),
                pl.BlockSpec((LANES, w), const)]
    if rev:
        in_specs.append(pl.BlockSpec((1, n, w), tok))
    else:
        in_specs.append(pl.BlockSpec((1, w), const))
    return pl.pallas_call(
        functools.partial(_ssd_kernel, rev=rev, off=off),
        out_shape=jax.ShapeDtypeStruct((bsz, s, w), BF16),
        grid=(bsz, n_all), in_specs=in_specs,
        out_specs=pl.BlockSpec((1, n, w), tok),
        scratch_shapes=[pltpu.VMEM((SSD_HEADS // 2, SSD_STATE, LANES), F32)],
        compiler_params=_params(2), name="ssd_scan_bwd" if rev else "ssd_scan_fwd",
    )(xbc, xbc, gates, dtb_row, a_row, expand, extra)


def _mlstm_kernel(*refs, rev, d):
    if rev:
        q_ref, k_ref, v_ref, g_ref, ib_ref, fb_ref, acc_ref, o_ref, c_st, n_st, m_st = refs
    else:
        q_ref, k_ref, v_ref, g_ref, ib_ref, fb_ref, o_ref, c_st, n_st, m_st = refs

    @pl.when(pl.program_id(1) == 0)
    def _():
        c_st[...] = jnp.zeros_like(c_st)
        n_st[...] = jnp.zeros_like(n_st)
        m_st[...] = jnp.zeros_like(m_st)

    n = q_ref.shape[1]
    last = 0 if rev else n - 1
    tri = _tri(n, rev)
    g = g_ref[0]
    log_i = g + ib_ref[...]
    log_f = -_softplus(-(g + fb_ref[...]))
    cs = _cumsum_dot(tri, log_f)
    cs_t = cs.T
    li_t = log_i.T
    for h in range(ML_HEADS):
        ci = GATE_I + ML_HEADS * d + h
        cf = GATE_F + ML_HEADS * d + h
        b_col = cs[:, cf:cf + 1]
        b_row = cs_t[cf:cf + 1, :]
        li_row = li_t[ci:ci + 1, :]
        tot = cs_t[cf:cf + 1, last:last + 1]
        dmat = jnp.where(tri, b_col - b_row + li_row, -jnp.inf)
        m_intra = jnp.max(dmat, axis=-1, keepdims=True)
        a_end = tot - b_row + li_row
        m_loc = jnp.max(a_end, axis=-1, keepdims=True)
        w_end = jnp.exp(a_end - m_loc)
        m_prev = m_st[h, 0:1, 0:1]
        m_inter = b_col + m_prev
        m_t = jnp.maximum(m_intra, m_inter)
        w_inter = jnp.exp(m_inter - m_t)
        qh = q_ref[0, :, h * ML_DK:(h + 1) * ML_DK]
        kf = k_ref[0, :, h * ML_DK:(h + 1) * ML_DK].astype(F32) * (ML_DK ** -0.5)
        kb = kf.astype(BF16)
        vh = v_ref[0, :, h * ML_DV:(h + 1) * ML_DV]
        s_mat = _dot_nt(qh, kb) * jnp.exp(dmat - m_t)
        c_prev = c_st[h]
        n_prev = n_st[h, 0:1, :]
        num = _dot(s_mat.astype(BF16), vh) + _dot(qh, c_prev.astype(BF16)) * w_inter
        den = (jnp.sum(s_mat, axis=-1, keepdims=True)
               + jnp.sum(qh.astype(F32) * n_prev, axis=-1, keepdims=True) * w_inter)
        den = jnp.maximum(jnp.abs(den), jnp.exp(-m_t))
        out = num / den
        kw_t = (kf.T * w_end).astype(BF16)
        c_chunk = _dot(kw_t, vh)
        n_chunk = _dot(jnp.broadcast_to(w_end, (8, n)).astype(BF16), kb)
        m_new = jnp.maximum(tot + m_prev, m_loc)
        a_sc = jnp.exp(tot + m_prev - m_new)
        b_sc = jnp.exp(m_loc - m_new)
        c_st[h] = c_prev * a_sc + c_chunk * b_sc
        n_st[h] = n_st[h] * a_sc + n_chunk * b_sc
        m_st[h] = jnp.broadcast_to(m_new, m_st.shape[1:])
        if rev:
            out = out + acc_ref[0, :, h * ML_DV:(h + 1) * ML_DV].astype(F32)
        o_ref[0, :, h * ML_DV:(h + 1) * ML_DV] = out.astype(o_ref.dtype)


def _mlstm_scan(q, k, v, gates, ib_row, fb_row, acc, *, rev, n_ctx):
    bsz, s, _ = q.shape
    n = SCAN_CHUNK
    n_all = s // n
    cidx = functools.partial(_scan_chunk_index, rev=rev, n_ctx=n_ctx, n_all=n_all)
    tok = lambda b, c: (b, cidx(c), 0)
    qw, vw = ML_HEADS * ML_DK, ML_HEADS * ML_DV
    in_specs = [pl.BlockSpec((1, n, qw), tok), pl.BlockSpec((1, n, qw), tok),
                pl.BlockSpec((1, n, vw), tok), pl.BlockSpec((1, n, LANES), tok),
                pl.BlockSpec((1, LANES), lambda b, c: (0, 0)),
                pl.BlockSpec((1, LANES), lambda b, c: (0, 0))]
    args = [q, k, v, gates, ib_row, fb_row]
    if rev:
        in_specs.append(pl.BlockSpec((1, n, vw), tok))
        args.append(acc)
    return pl.pallas_call(
        functools.partial(_mlstm_kernel, rev=rev, d=1 if rev else 0),
        out_shape=jax.ShapeDtypeStruct((bsz, s, vw), F32),
        grid=(bsz, n_all), in_specs=in_specs,
        out_specs=pl.BlockSpec((1, n, vw), tok),
        scratch_shapes=[pltpu.VMEM((ML_HEADS, ML_DK, ML_DV), F32),
                        pltpu.VMEM((ML_HEADS, 8, ML_DK), F32),
                        pltpu.VMEM((ML_HEADS, 8, LANES), F32)],
        compiler_params=_params(2), name="mlstm_scan_bwd" if rev else "mlstm_scan_fwd",
    )(*args)


def _post_kernel(*refs, mode):
    if mode == "mix":
        (x_ref, y_ref, z_ref, hm_ref, og_ref, snw_ref, mnw_ref, wa_ref, wb_ref,
         mod_ref, n2_ref, wr_ref, br_ref, xo_ref, h2_ref, lg_ref) = refs
        y = _rms(y_ref[0].astype(F32) * _silu(z_ref[0].astype(F32)), snw_ref[...])
        og = og_ref[0].astype(F32)
        o = _dot(y.astype(BF16), wa_ref[...])
        parts = []
        for h in range(ML_HEADS):
            sl = slice(h * ML_DV, (h + 1) * ML_DV)
            parts.append(_rms(hm_ref[0, :, sl], mnw_ref[:, sl]) * jax.nn.sigmoid(og[:, sl]))
        o = o + _dot(jnp.concatenate(parts, axis=-1).astype(BF16), wb_ref[...])
    else:
        (x_ref, a_ref, wa_ref, mod_ref, n2_ref, wr_ref, br_ref, xo_ref, h2_ref, lg_ref) = refs
        o = _dot(a_ref[0], wa_ref[...])
    x_new = x_ref[0] + mod_ref[0, 2:3, :] * o
    xo_ref[0] = x_new
    h2 = _rms(x_new, n2_ref[...]) * (1.0 + mod_ref[0, 4:5, :]) + mod_ref[0, 3:4, :]
    h2_ref[0] = h2
    lg_ref[0] = _dot_split(h2, wr_ref) + br_ref[...]


def _post_call(mode, x, acts, rows, mats, mod_l, norm2_w, w_router, b_router, x_tile_off, n_tiles, ctx_tiles, ctx_row):
    bsz, _, d = x.shape
    tm = ROW_TILE
    s_out = n_tiles * tm
    tok = lambda b, i: (b, i, 0)
    const = lambda b, i: (0, 0)
    mod_idx = lambda b, i: (jnp.where(i + x_tile_off < ctx_tiles, ctx_row, b), 0, 0)
    in_specs = [pl.BlockSpec((1, tm, d), lambda b, i: (b, i + x_tile_off, 0))]
    in_specs += [pl.BlockSpec((1, tm, a.shape[2]), tok) for a in acts]
    in_specs += [pl.BlockSpec(r.shape, const) for r in rows]
    in_specs += [pl.BlockSpec(m.shape, const) for m in mats]
    in_specs += [pl.BlockSpec((1, 6, d), mod_idx), pl.BlockSpec((1, d), const),
                 pl.BlockSpec(w_router.shape, lambda b, i: (0, 0, 0)), pl.BlockSpec((1, LANES), const)]
    out_shape = [jax.ShapeDtypeStruct((bsz, s_out, d), F32),
                 jax.ShapeDtypeStruct((bsz, s_out, d), F32),
                 jax.ShapeDtypeStruct((bsz, s_out, LANES), F32)]
    out_specs = [pl.BlockSpec((1, tm, d), tok), pl.BlockSpec((1, tm, d), tok),
                 pl.BlockSpec((1, tm, LANES), tok)]
    return pl.pallas_call(
        functools.partial(_post_kernel, mode=mode),
        out_shape=out_shape, grid=(bsz, n_tiles), in_specs=in_specs, out_specs=out_specs,
        compiler_params=_params(2), name="post_" + mode,
    )(x, *acts, *rows, *mats, mod_l, norm2_w.reshape(1, d), w_router, b_router)


def _router_kernel(lg_ref, route_ref, cnt_ref, all_ref, off_ref, *, moe_tile):
    phase = pl.program_id(0)
    i = pl.program_id(1)
    tm = lg_ref.shape[0]
    rows = pl.ds(pl.multiple_of(i * tm, tm), tm)

    @pl.when((phase == 0) & (i == 0))
    def _():
        cnt_ref[...] = jnp.zeros_like(cnt_ref)

    @pl.when(phase == 0)
    def _():
        all_ref[rows, :] = _route_fields(lg_ref[...], cnt_ref)

    @pl.when((phase == 1) & (i == 0))
    def _():
        tiles = jnp.ceil(cnt_ref[...] * (1.0 / moe_tile))
        r = lax.broadcasted_iota(jnp.int32, (LANES, LANES), 0)
        c = lax.broadcasted_iota(jnp.int32, (LANES, LANES), 1)
        earlier = jnp.where(r < c, 1.0, 0.0).astype(BF16)
        off_ref[...] = _dot(tiles.astype(BF16), earlier) * float(moe_tile)

    @pl.when(phase == 1)
    def _():
        f = all_ref[rows, :]
        lane = lax.broadcasted_iota(jnp.int32, f.shape, 1).astype(F32)
        off = off_ref[0:1, :]
        pos1 = jnp.sum(jnp.where(lane == f[:, 0:1], off, 0.0), axis=-1, keepdims=True) + f[:, 4:5]
        pos2 = jnp.sum(jnp.where(lane == f[:, 1:2], off, 0.0), axis=-1, keepdims=True) + f[:, 5:6]
        route_ref[...] = jnp.where(lane == 6.0, pos1, jnp.where(lane == 7.0, pos2, f))


def _route_fields(lg, cnt_ref):
    tm = lg.shape[0]
    lane = lax.broadcasted_iota(jnp.int32, lg.shape, 1).astype(F32)
    big = float(LANES)
    is_g = (lane >= ROUTE_G) & (lane < ROUTE_G + MOE_GROUPS)
    lgg = jnp.where(is_g, lg, -jnp.inf)
    g_max = jnp.max(lgg, axis=-1, keepdims=True)
    g_idx = jnp.min(jnp.where(lgg == g_max, lane - ROUTE_G, big), axis=-1, keepdims=True)
    g_prob = 1.0 / jnp.sum(jnp.exp(lgg - g_max), axis=-1, keepdims=True)
    lo = g_idx * MOE_EXPERTS
    le = jnp.where((lane >= lo) & (lane < lo + MOE_EXPERTS), lg, -jnp.inf)
    l1 = jnp.max(le, axis=-1, keepdims=True)
    i1 = jnp.min(jnp.where(le == l1, lane, big), axis=-1, keepdims=True)
    le2 = jnp.where(lane == i1, -jnp.inf, le)
    l2 = jnp.max(le2, axis=-1, keepdims=True)
    i2 = jnp.min(jnp.where(le2 == l2, lane, big), axis=-1, keepdims=True)
    r = jnp.exp(l2 - l1)
    w1 = g_prob / (1.0 + r)
    w2 = w1 * r
    oh1 = jnp.where(lane == i1, 1.0, 0.0)
    oh2 = jnp.where(lane == i2, 1.0, 0.0)
    oh = oh1 + oh2
    row = lax.broadcasted_iota(jnp.int32, (tm, tm), 0)
    col = lax.broadcasted_iota(jnp.int32, (tm, tm), 1)
    before = jnp.where(col < row, 1.0, 0.0).astype(BF16)
    prefix = _dot(before, oh.astype(BF16)) + cnt_ref[0:1, :]
    rank1 = jnp.sum(prefix * oh1, axis=-1, keepdims=True)
    rank2 = jnp.sum(prefix * oh2, axis=-1, keepdims=True)
    cnt_ref[...] = cnt_ref[...] + jnp.sum(oh, axis=0, keepdims=True)
    fields = (i1, i2, w1, w2, rank1, rank2)
    out = jnp.zeros_like(lg)
    for j, f in enumerate(fields):
        out = jnp.where(lane == float(j), f, out)
    return out


def _router(logits):
    t = logits.shape[0]
    tm = ROW_TILE
    return pl.pallas_call(
        functools.partial(_router_kernel, moe_tile=MOE_TILE),
        out_shape=[jax.ShapeDtypeStruct((t, LANES), F32), jax.ShapeDtypeStruct((8, LANES), F32)],
        grid=(2, t // tm),
        in_specs=[pl.BlockSpec((tm, LANES), lambda p, i: (i * (1 - p), 0))],
        out_specs=[pl.BlockSpec((tm, LANES), lambda p, i: (i * p, 0)), pl.BlockSpec((8, LANES), lambda p, i: (0, 0))],
        scratch_shapes=[pltpu.VMEM((t, LANES), F32), pltpu.VMEM((8, LANES), F32)],
        compiler_params=_params(2), name="router",
    )(logits)


def _moe_kernel(te_ref, nt_ref, x_ref, wg_ref, wu_ref, wd_ref, o_ref, wgb, wub, wdb):
    i = pl.program_id(0)
    valid = i < nt_ref[0]
    fresh = (i == 0) | (te_ref[i] != te_ref[jnp.maximum(i - 1, 0)])

    @pl.when(valid & fresh)
    def _():
        wgb[...] = wg_ref[0].astype(BF16)
        wub[...] = wu_ref[0].astype(BF16)
        wdb[...] = wd_ref[0].astype(BF16)

    @pl.when(valid)
    def _():
        x = x_ref[...].astype(BF16)
        act = _silu(_dot(x, wgb[...])) * _dot(x, wub[...])
        o_ref[...] = _dot(act.astype(BF16), wdb[...]).astype(o_ref.dtype)

    @pl.when(jnp.logical_not(valid))
    def _():
        o_ref[...] = jnp.zeros_like(o_ref)


def _moe_experts(x_sorted, tile_expert, n_tiles_used, wg, wu, wd):
    tm = MOE_TILE
    rows, d = x_sorted.shape
    ff = wg.shape[2]
    grid_spec = pltpu.PrefetchScalarGridSpec(
        num_scalar_prefetch=2, grid=(rows // tm,),
        in_specs=[pl.BlockSpec((tm, d), lambda i, te, nt: (i, 0)),
                  pl.BlockSpec((1, d, ff), lambda i, te, nt: (te[i], 0, 0)),
                  pl.BlockSpec((1, d, ff), lambda i, te, nt: (te[i], 0, 0)),
                  pl.BlockSpec((1, ff, d), lambda i, te, nt: (te[i], 0, 0))],
        out_specs=pl.BlockSpec((tm, d), lambda i, te, nt: (i, 0)),
        scratch_shapes=[pltpu.VMEM((d, ff), BF16), pltpu.VMEM((d, ff), BF16), pltpu.VMEM((ff, d), BF16)])
    return pl.pallas_call(
        _moe_kernel, out_shape=jax.ShapeDtypeStruct((rows, d), F32), grid_spec=grid_spec,
        compiler_params=_params(1), name="moe_experts",
    )(tile_expert, n_tiles_used, x_sorted, wg, wu, wd)


def _combine_kernel(x_ref, y1_ref, y2_ref, rt_ref, mod_ref, o_ref):
    f = rt_ref[0, :, 2:3] * y1_ref[0] + rt_ref[0, :, 3:4] * y2_ref[0]
    o_ref[0] = x_ref[0] + mod_ref[0, 5:6, :] * f


def _combine(x, y1, y2, route, mod_l, ctx_tiles, ctx_row):
    bsz, s, d = x.shape
    tm = ROW_TILE
    tok = lambda b, i: (b, i, 0)
    mod_idx = lambda b, i: (jnp.where(i < ctx_tiles, ctx_row, b), 0, 0)
    return pl.pallas_call(
        _combine_kernel, out_shape=jax.ShapeDtypeStruct(x.shape, F32), grid=(bsz, s // tm),
        in_specs=[pl.BlockSpec((1, tm, d), tok), pl.BlockSpec((1, tm, d), tok), pl.BlockSpec((1, tm, d), tok),
                  pl.BlockSpec((1, tm, LANES), tok), pl.BlockSpec((1, 6, d), mod_idx)],
        out_specs=pl.BlockSpec((1, tm, d), tok),
        compiler_params=_params(2), name="moe_combine",
    )(x, y1.reshape(x.shape), y2.reshape(x.shape), route.reshape(bsz, s, LANES), mod_l)


def _hier_moe(x, h2, logits, mod_l, layer, wg, wu, wd, ctx_tiles, ctx_row):
    bsz, s, d = h2.shape
    t = bsz * s
    tm = MOE_TILE
    route, counts = _router(logits.reshape(t, LANES))
    n_tiles = 2 * t // tm + N_EXPERTS
    tiles_per = (counts[0, :N_EXPERTS].astype(jnp.int32) + tm - 1) // tm
    tile_end = jnp.cumsum(tiles_per)
    tile_ids = jnp.arange(n_tiles, dtype=jnp.int32)
    tile_expert = jnp.minimum(jnp.sum((tile_end[None, :] <= tile_ids[:, None]).astype(jnp.int32), axis=1),
                              N_EXPERTS - 1) + layer * N_EXPERTS
    pos = route[:, 6:8].astype(jnp.int32)
    tok = jnp.arange(t, dtype=jnp.int32)
    src = (jnp.arange(n_tiles * tm, dtype=jnp.int32) % t).at[jnp.concatenate([pos[:, 0], pos[:, 1]])].set(
        jnp.concatenate([tok, tok]), unique_indices=True)
    x_sorted = jnp.take(h2.reshape(t, d), src, axis=0, mode="clip")
    y_sorted = _moe_experts(x_sorted, tile_expert, tile_end[-1:], wg.reshape(-1, d, wg.shape[-1]),
                            wu.reshape(-1, d, wu.shape[-1]), wd.reshape(-1, wd.shape[-2], d))
    y1 = jnp.take(y_sorted, pos[:, 0], axis=0, mode="clip")
    y2 = jnp.take(y_sorted, pos[:, 1], axis=0, mode="clip")
    return _combine(x, y1, y2, route, mod_l, ctx_tiles, ctx_row)


def _attn_kernel(q_ref, k_ref, v_ref, bias_ref, qw_ref, kw_ref, o_ref, qn_ref, kn_ref, *, ctx_len, rows):
    lane = lax.broadcasted_iota(jnp.int32, (1, LANES), 1)
    first = lane < NA_HEAD_DIM

    def head_norm(x, w):
        x2 = x * x
        s0 = jnp.sum(jnp.where(first, x2, 0.0), axis=-1, keepdims=True)
        s1 = jnp.sum(jnp.where(first, 0.0, x2), axis=-1, keepdims=True)
        ms = jnp.where(first, s0, s1) * (1.0 / NA_HEAD_DIM)
        return x * lax.rsqrt(ms + RMS_EPS) * w

    kn_ref[...] = head_norm(k_ref[0].astype(F32), kw_ref[...]).astype(BF16)
    qn_ref[...] = (head_norm(q_ref[0, ctx_len:, :].astype(F32), qw_ref[...]) * NA_HEAD_DIM ** -0.5).astype(BF16)
    n_groups = rows // ATTN_GROUP_ROWS
    n_q = ATTN_GROUP_ROWS * GRID_W
    n_loc = ATTN_KEY_ROWS * GRID_W
    k_ctx = kn_ref[0:ctx_len, :]
    v_ctx = v_ref[0, 0:ctx_len, :]

    def one_group(g):
        kind = jnp.where(g == 0, 0, jnp.where(g == n_groups - 1, 2, 1))
        kr0 = jnp.clip(g * ATTN_GROUP_ROWS - NA_KH // 2, 0, rows - ATTN_KEY_ROWS)
        q_rows = pl.ds(pl.multiple_of(g * n_q, n_q), n_q)
        q = qn_ref[q_rows, :]
        zero = jnp.zeros_like(q)
        q2 = jnp.concatenate([jnp.where(first, q, zero), jnp.where(first, zero, q)], axis=0)
        k_off = pl.multiple_of(ctx_len + kr0 * GRID_W, GRID_W)
        s_loc = _dot_nt(q2, kn_ref[pl.ds(k_off, n_loc), :])
        s_ctx = _dot_nt(q2, k_ctx)
        p_loc, p_ctx, inv = [], [], []
        for hh in range(2):
            sl = s_loc[hh * n_q:(hh + 1) * n_q] + bias_ref[hh, kind]
            sc = s_ctx[hh * n_q:(hh + 1) * n_q]
            m = jnp.maximum(jnp.max(sl, axis=-1, keepdims=True), jnp.max(sc, axis=-1, keepdims=True))
            el = jnp.exp(sl - m)
            ec = jnp.exp(sc - m)
            inv.append(1.0 / (jnp.sum(el, axis=-1, keepdims=True) + jnp.sum(ec, axis=-1, keepdims=True)))
            p_loc.append(el.astype(BF16))
            p_ctx.append(ec.astype(BF16))
        o = (_dot(jnp.concatenate(p_loc, axis=0), v_ref[0, pl.ds(k_off, n_loc), :])
             + _dot(jnp.concatenate(p_ctx, axis=0), v_ctx))
        o_ref[0, q_rows, :] = jnp.where(first, o[:n_q] * inv[0], o[n_q:] * inv[1]).astype(o_ref.dtype)

    def body(i, carry):
        for j in range(ATTN_GROUPS_PER_TRIP):
            one_group(i * ATTN_GROUPS_PER_TRIP + j)
        return carry

    lax.fori_loop(0, n_groups // ATTN_GROUPS_PER_TRIP, body, 0)


def _attn_group_layout(rows):
    n_groups = rows // ATTN_GROUP_ROWS
    assert rows % ATTN_GROUP_ROWS == 0 and rows >= ATTN_KEY_ROWS and n_groups >= 2
    u = np.arange(ATTN_GROUP_ROWS)[:, None]
    i = np.arange(ATTN_KEY_ROWS)[None, :]

    def layout(g):
        r = g * ATTN_GROUP_ROWS + u
        r0 = np.clip(r - NA_KH // 2, 0, rows - NA_KH)
        kr = np.clip(g * ATTN_GROUP_ROWS - NA_KH // 2, 0, rows - ATTN_KEY_ROWS) + i
        return (kr >= r0) & (kr < r0 + NA_KH), kr - r + NA_KH - 1

    kinds = [layout(0), layout(1), layout(n_groups - 1)]
    for g in range(1, n_groups - 1):
        valid, d = layout(g)
        assert (valid == kinds[1][0]).all() and (d[valid] == kinds[1][1][valid]).all()
    return np.stack([k[0] for k in kinds]), np.stack([k[1] for k in kinds])


def _bias_windows(rpb, rows):
    qc = np.arange(GRID_W)
    c0 = np.clip(qc - NA_KW // 2, 0, GRID_W - NA_KW)
    kc = np.arange(GRID_W)
    inwin = (kc[None, :] >= c0[:, None]) & (kc[None, :] < c0[:, None] + NA_KW)
    coff = kc[None, :] - qc[:, None] + NA_KW - 1
    pick = (coff[..., None] == np.arange(2 * NA_KW - 1)) & inwin[..., None]
    tab = jnp.einsum("hdo,qko->dhqk", rpb.astype(F32), jnp.asarray(pick, F32), precision=HIGHEST)
    tab = jnp.where(inwin[None, None], tab, NEG)
    valid, d = _attn_group_layout(rows)
    masked = jnp.full(tab.shape[1:], NEG, F32)
    blocks = []
    for kind in range(valid.shape[0]):
        for u in range(ATTN_GROUP_ROWS):
            blocks.append(jnp.concatenate(
                [tab[int(d[kind, u, i])] if valid[kind, u, i] else masked for i in range(ATTN_KEY_ROWS)], axis=-1))
    win = jnp.stack(blocks, axis=1)
    return win.reshape(rpb.shape[0], valid.shape[0], ATTN_GROUP_ROWS * GRID_W, ATTN_KEY_ROWS * GRID_W)


def _neighbourhood_attention(q, k, v, bias, qn_w, kn_w, ctx_len):
    bsz, s, w = q.shape
    seq = s - ctx_len
    rows = seq // GRID_W
    n_pairs = NA_HEADS // 2
    pair = lambda b, p: (b, 0, p)
    row2 = lambda b, p: (0, 0)
    return pl.pallas_call(
        functools.partial(_attn_kernel, ctx_len=ctx_len, rows=rows),
        out_shape=jax.ShapeDtypeStruct((bsz, seq, w), BF16),
        grid=(bsz, n_pairs),
        in_specs=[pl.BlockSpec((1, s, LANES), pair), pl.BlockSpec((1, s, LANES), pair),
                  pl.BlockSpec((1, s, LANES), pair),
                  pl.BlockSpec((2,) + bias.shape[1:], lambda b, p: (p, 0, 0, 0)),
                  pl.BlockSpec((1, LANES), row2), pl.BlockSpec((1, LANES), row2)],
        out_specs=pl.BlockSpec((1, seq, LANES), pair),
        scratch_shapes=[pltpu.VMEM((seq, LANES), BF16), pltpu.VMEM((s, LANES), BF16)],
        compiler_params=_params(2), name="neighbourhood_attention",
    )(q, k, v, bias, jnp.tile(qn_w, 2).reshape(1, LANES), jnp.tile(kn_w, 2).reshape(1, LANES))


def _pad_row(pieces, width=LANES):
    row = jnp.zeros((width,), F32)
    for off, vec in pieces:
        row = row.at[off:off + vec.shape[0]].set(vec.astype(F32))
    return row.reshape(1, width)


def _router_params(w_group, b_group, w_expert, b_expert):
    d = w_group.shape[0]
    w = jnp.zeros((d, LANES), F32).at[:, :N_EXPERTS].set(w_expert).at[:, ROUTE_G:ROUTE_G + MOE_GROUPS].set(w_group)
    return _split_weight(w), _pad_row([(0, b_expert), (ROUTE_G, b_group)])


def kernel(x, c, ctx, c_ctx, norm1_w, norm2_w, mod_w, mod_b, ab_w_in, ab_conv_w, ab_conv_b, ssd_a_log, ssd_dt_bias, ssd_d, ssd_norm_w, ml_i_bias, ml_f_bias, ml_norm_w, ab_w_out, na_w_qkv, na_q_norm, na_k_norm, na_rpb, na_w_out, moe_w_group, moe_b_group, moe_w_expert, moe_b_expert, moe_w_gate, moe_w_up, moe_w_down):
    bsz, seq, d = x.shape
    ctx_len = ctx.shape[1]
    depth = mod_w.shape[0]
    assert depth == 2 and ctx_len % ROW_TILE == 0 and seq % ROW_TILE == 0 and bsz < 8
    ctx_tiles = ctx_len // ROW_TILE
    lat_tiles = seq // ROW_TILE
    ctx_row = bsz

    cvec = jnp.zeros((8, d), F32).at[:bsz].set(c).at[bsz].set(c_ctx)
    mod = _mod_vectors(cvec, mod_w, mod_b)
    xs = jnp.concatenate([ctx, x], axis=1)

    ssd_w = SSD_HEADS * SSD_HEAD_DIM
    xbc_w = ssd_w + 2 * SSD_GROUPS * SSD_STATE
    qk_w, v_w = ML_HEADS * ML_DK, ML_HEADS * ML_DV
    sizes = (ssd_w, xbc_w, 2 * SSD_HEADS, qk_w, qk_w, v_w, v_w, 2 * ML_HEADS, 2 * ML_HEADS)
    w_z, w_xbc, w_dt, w_q, w_k, w_v, w_o, w_i, w_f = jnp.split(ab_w_in[0], np.cumsum(sizes)[:-1].tolist(), axis=1)
    w_gate = jnp.zeros((d, LANES), F32).at[:, :GATE_F + 2 * ML_HEADS].set(jnp.concatenate([w_dt, w_i, w_f], axis=1))
    weights = [w.astype(BF16) for w in (w_z, w_xbc, w_q, w_k, w_v, w_o)] + [_split_weight(w_gate)]
    z, xbc, q, k, v, og, gates = _norm_mod_matmul(xs, norm1_w[0], mod[0], weights, [BF16] * 6 + [F32],
                                                  ctx_tiles, ctx_row)
    xbc = _conv_silu(xbc, ab_conv_w[0], ab_conv_b[0], ctx_len)
    n_ctx = ctx_len // SCAN_CHUNK
    a_neg = -jnp.exp(ssd_a_log[0].astype(F32))
    dsk_row = jnp.repeat(ssd_d[0].astype(F32), SSD_HEAD_DIM).reshape(1, ssd_w)
    y = None
    hm = None
    for dr in range(2):
        rev = dr == 1
        dtb_row = _pad_row([(GATE_DT + dr * SSD_HEADS, ssd_dt_bias[0, dr])])
        a_row = _pad_row([(GATE_DT + dr * SSD_HEADS, a_neg[dr])])
        y = _ssd_scan(xbc, gates, dtb_row, a_row, y if rev else dsk_row, rev=rev, n_ctx=n_ctx)
        ib_row = _pad_row([(GATE_I + dr * ML_HEADS, ml_i_bias[0, dr])])
        fb_row = _pad_row([(GATE_F + dr * ML_HEADS, ml_f_bias[0, dr])])
        hm = _mlstm_scan(q, k, v, gates, ib_row, fb_row, hm, rev=rev, n_ctx=n_ctx)
    w_r, b_r = _router_params(moe_w_group[0], moe_b_group[0], moe_w_expert[0], moe_b_expert[0])
    w_out = ab_w_out[0].astype(BF16)
    x1, h2, logits = _post_call(
        "mix", xs, [y, z, hm, og],
        [ssd_norm_w[0].reshape(1, ssd_w), ml_norm_w[0].reshape(1, v_w)], [w_out[:ssd_w], w_out[ssd_w:]],
        mod[0], norm2_w[0], w_r, b_r, 0, ctx_tiles + lat_tiles, ctx_tiles, ctx_row)
    xs = _hier_moe(x1, h2, logits, mod[0], 0, moe_w_gate, moe_w_up, moe_w_down, ctx_tiles, ctx_row)

    w_qkv = na_w_qkv[0].astype(BF16)
    na_w = NA_HEADS * NA_HEAD_DIM
    q, k, v = _norm_mod_matmul(xs, norm1_w[1], mod[1], [w_qkv[:, :na_w], w_qkv[:, na_w:2 * na_w], w_qkv[:, 2 * na_w:]],
                               [BF16] * 3, ctx_tiles, ctx_row)
    bias = _bias_windows(na_rpb[0], seq // GRID_W)
    attn = _neighbourhood_attention(q, k, v, bias, na_q_norm[0], na_k_norm[0], ctx_len)
    w_r, b_r = _router_params(moe_w_group[1], moe_b_group[1], moe_w_expert[1], moe_b_expert[1])
    x1, h2, logits = _post_call("attn", xs, [attn], [], [na_w_out[0].astype(BF16)],
                                mod[1], norm2_w[1], w_r, b_r, ctx_tiles, lat_tiles, ctx_tiles, ctx_row)
    return _hier_moe(x1, h2, logits, mod[1], 1, moe_w_gate, moe_w_up, moe_w_down, 0, ctx_row)
```

```python
import functools

import numpy as np
import jax
import jax.numpy as jnp
from jax import lax
from jax.experimental import pallas as pl
from jax.experimental.pallas import tpu as pltpu

F32 = jnp.float32
BF16 = jnp.bfloat16
HIGHEST = lax.Precision.HIGHEST

RMS_EPS = 1e-6
GRID_W = 64
SSD_HEADS = 16
SSD_HEAD_DIM = 64
SSD_GROUPS = 2
SSD_STATE = 128
ML_HEADS = 4
ML_DK = 128
ML_DV = 256
NA_HEADS = 16
NA_HEAD_DIM = 64
NA_KH = 8
NA_KW = 16
MOE_GROUPS = 4
MOE_EXPERTS = 8
N_EXPERTS = MOE_GROUPS * MOE_EXPERTS

LANES = 128
ROW_TILE = 256
SCAN_CHUNK = 128
MOE_TILE = 256
ATTN_GROUP_ROWS = 4
ATTN_KEY_ROWS = ATTN_GROUP_ROWS + NA_KH - 1
ATTN_GROUPS_PER_TRIP = 4
VMEM_LIMIT = 56 * 1024 * 1024

GATE_DT = 0
GATE_I = 2 * SSD_HEADS
GATE_F = GATE_I + 2 * ML_HEADS
ROUTE_G = N_EXPERTS
NEG = -1e30


def _params(n_axes):
    return pltpu.CompilerParams(dimension_semantics=("arbitrary",) * n_axes,
                                vmem_limit_bytes=VMEM_LIMIT)


def _silu(x):
    return x * jax.nn.sigmoid(x)


def _softplus(x):
    return jnp.maximum(x, 0.0) + jnp.log1p(jnp.exp(-jnp.abs(x)))


def _rms(x, w):
    return x * lax.rsqrt(jnp.mean(x * x, axis=-1, keepdims=True) + RMS_EPS) * w


def _dot(a, b):
    return jnp.dot(a, b, preferred_element_type=F32)


def _dot_nt(a, b):
    return lax.dot_general(a, b, (((1,), (1,)), ((), ())), preferred_element_type=F32)


def _dot_hi(a, b):
    return jnp.dot(a, b, precision=HIGHEST, preferred_element_type=F32)


def _split_bf16(x, terms):
    parts = []
    for _ in range(terms - 1):
        p = x.astype(BF16)
        parts.append(p)
        x = x - p.astype(F32)
    parts.append(x.astype(BF16))
    return parts


def _split_weight(w):
    return jnp.stack(_split_bf16(w.astype(F32), 2))


def _dot_split(a, w2_ref):
    a_hi, a_lo = _split_bf16(a, 2)
    return _dot(a_hi, w2_ref[0]) + _dot(a_lo, w2_ref[0]) + _dot(a_hi, w2_ref[1])


def _cumsum_dot(tri, x):
    tri = jnp.where(tri, 1.0, 0.0).astype(BF16)
    hi, mid, lo = _split_bf16(x, 3)
    return _dot(tri, hi) + _dot(tri, mid) + _dot(tri, lo)


def _mod_kernel(c_ref, w_ref, b_ref, o_ref):
    o_ref[0] = _dot_hi(_silu(c_ref[...]), w_ref[0]) + b_ref[0]


def _mod_vectors(cvec, mod_w, mod_b, tn=512):
    depth, d, n = mod_w.shape
    rows = cvec.shape[0]
    out = pl.pallas_call(
        _mod_kernel,
        out_shape=jax.ShapeDtypeStruct((depth, rows, n), F32),
        grid=(depth, n // tn),
        in_specs=[pl.BlockSpec((rows, d), lambda l, j: (0, 0)),
                  pl.BlockSpec((1, d, tn), lambda l, j: (l, 0, j)),
                  pl.BlockSpec((1, 1, tn), lambda l, j: (l, 0, j))],
        out_specs=pl.BlockSpec((1, rows, tn), lambda l, j: (l, 0, j)),
        compiler_params=_params(2),
        name="mod_vectors",
    )(cvec, mod_w, mod_b.reshape(depth, 1, n))
    return out.reshape(depth, rows, 6, d)


def _nmm_kernel(x_ref, *refs, n_out, pending_moe):
    x = x_ref[0]
    if pending_moe:
        y1_ref, y2_ref, rt_ref, pmod_ref = refs[:4]
        refs = refs[4:]
        x = x + pmod_ref[0, 5:6, :] * (rt_ref[0, :, 2:3] * y1_ref[0] + rt_ref[0, :, 3:4] * y2_ref[0])
        refs[-1][0] = x
        refs = refs[:-1]
    nw_ref, mod_ref = refs[:2]
    refs = refs[2:]
    w_refs, o_refs = refs[:n_out], refs[n_out:]
    h = _rms(x, nw_ref[...])
    h = h * (1.0 + mod_ref[0, 1:2, :]) + mod_ref[0, 0:1, :]
    hb = h.astype(BF16)
    for w_ref, o_ref in zip(w_refs, o_refs):
        if len(w_ref.shape) == 3:
            o_ref[0] = _dot_split(h, w_ref)
        else:
            o_ref[0] = _dot(hb, w_ref[...]).astype(o_ref.dtype)


def _norm_mod_matmul(xs, norm_w, mod_l, weights, out_dtypes, ctx_tiles, ctx_row, pending_moe=None):
    bsz, s, d = xs.shape
    tm = ROW_TILE
    tok = lambda b, i: (b, i, 0)
    mod_idx = lambda b, i: (jnp.where(i < ctx_tiles, ctx_row, b), 0, 0)
    args = [xs]
    in_specs = [pl.BlockSpec((1, tm, d), tok)]
    if pending_moe is not None:
        y1, y2, route, mod_prev = pending_moe
        args += [y1.reshape(xs.shape), y2.reshape(xs.shape), route.reshape(bsz, s, LANES), mod_prev]
        in_specs += [pl.BlockSpec((1, tm, d), tok), pl.BlockSpec((1, tm, d), tok),
                     pl.BlockSpec((1, tm, LANES), tok), pl.BlockSpec((1, 6, d), mod_idx)]
    args += [norm_w.reshape(1, d), mod_l, *weights]
    in_specs += [pl.BlockSpec((1, d), lambda b, i: (0, 0)), pl.BlockSpec((1, 6, d), mod_idx)]
    in_specs += [pl.BlockSpec(w.shape, lambda b, i, nd=w.ndim: (0,) * nd) for w in weights]
    out_shape = [jax.ShapeDtypeStruct((bsz, s, w.shape[-1]), dt) for w, dt in zip(weights, out_dtypes)]
    out_specs = [pl.BlockSpec((1, tm, w.shape[-1]), tok) for w in weights]
    if pending_moe is not None:
        out_shape.append(jax.ShapeDtypeStruct(xs.shape, F32))
        out_specs.append(pl.BlockSpec((1, tm, d), tok))
    return pl.pallas_call(
        functools.partial(_nmm_kernel, n_out=len(weights), pending_moe=pending_moe is not None),
        out_shape=out_shape, grid=(bsz, s // tm), in_specs=in_specs, out_specs=out_specs,
        compiler_params=_params(2), name="norm_mod_matmul",
    )(*args)


def _conv_kernel(u_ref, w_ref, b_ref, o_ref, *, ctx_len):
    u = u_ref[0].astype(F32)
    s = u.shape[0]
    t = lax.broadcasted_iota(jnp.int32, u.shape, 0)
    prev = jnp.where((t == 0) | (t == ctx_len), 0.0, pltpu.roll(u, 1, axis=0))
    nxt = jnp.where((t == ctx_len - 1) | (t == s - 1), 0.0, pltpu.roll(u, s - 1, axis=0))
    y = prev * w_ref[0:1, :] + u * w_ref[1:2, :] + nxt * w_ref[2:3, :] + b_ref[...]
    o_ref[0] = _silu(y).astype(o_ref.dtype)


def _conv_silu(u, conv_w, conv_b, ctx_len, tc=LANES):
    bsz, s, ch = u.shape
    return pl.pallas_call(
        functools.partial(_conv_kernel, ctx_len=ctx_len),
        out_shape=jax.ShapeDtypeStruct(u.shape, u.dtype),
        grid=(bsz, ch // tc),
        in_specs=[pl.BlockSpec((1, s, tc), lambda b, j: (b, 0, j)),
                  pl.BlockSpec((3, tc), lambda b, j: (0, j)),
                  pl.BlockSpec((1, tc), lambda b, j: (0, j))],
        out_specs=pl.BlockSpec((1, s, tc), lambda b, j: (b, 0, j)),
        compiler_params=_params(2), name="conv_silu",
    )(u, conv_w, conv_b.reshape(1, ch))


def _scan_chunk_index(c, rev, n_ctx, n_all):
    if not rev:
        return c
    return jnp.where(c < n_ctx, n_ctx - 1 - c, n_ctx + n_all - 1 - c)


def _tri(n, rev):
    row = lax.broadcasted_iota(jnp.int32, (n, n), 0)
    col = lax.broadcasted_iota(jnp.int32, (n, n), 1)
    return (col >= row) if rev else (col <= row)


def _ssd_kernel(*refs, rev, off):
    if rev:
        xs_ref, bc_ref, g_ref, dtb_ref, a_ref, ex_ref, acc_ref, o_ref, st_ref = refs
    else:
        xs_ref, bc_ref, g_ref, dtb_ref, a_ref, ex_ref, dsk_ref, o_ref, st_ref = refs

    @pl.when(pl.program_id(1) == 0)
    def _():
        st_ref[...] = jnp.zeros_like(st_ref)

    n = xs_ref.shape[1]
    last = 0 if rev else n - 1
    tri = _tri(n, rev)
    dt = _softplus(g_ref[0] + dtb_ref[...])
    log_a = dt * a_ref[...]
    cs = _cumsum_dot(tri, log_a)
    cs_t = cs.T
    dt_hi = dt.astype(BF16)
    dt_lo = (dt - dt_hi.astype(F32)).astype(BF16)
    dt_full = _dot(dt_hi, ex_ref[...]) + _dot(dt_lo, ex_ref[...])
    xs = xs_ref[0].astype(F32)
    xdt = (xs * dt_full).astype(BF16)
    lo_half = lax.broadcasted_iota(jnp.int32, (1, LANES), 1) < SSD_HEAD_DIM
    gw = SSD_GROUPS * SSD_STATE
    heads_per_group = SSD_HEADS // SSD_GROUPS
    for g in range(SSD_GROUPS):
        b_g = bc_ref[0, :, g * SSD_STATE:(g + 1) * SSD_STATE]
        c_g = bc_ref[0, :, gw + g * SSD_STATE:gw + (g + 1) * SSD_STATE]
        cb = _dot_nt(c_g, b_g)
        b_t = b_g.astype(F32).T
        for e in range(0, heads_per_group, 2):
            h0 = g * heads_per_group + e
            pair = h0 // 2
            sl = slice(pair * LANES, (pair + 1) * LANES)
            x_pair = xdt[:, sl]
            zero = jnp.zeros_like(x_pair)
            y, upd, a_bc, tots = None, None, [], []
            for j in range(2):
                col = off + h0 + j
                x_j = jnp.where(lo_half, x_pair, zero) if j == 0 else jnp.where(lo_half, zero, x_pair)
                a_b = jnp.broadcast_to(cs[:, col:col + 1], (n, LANES))
                a_row = cs_t[col:col + 1, :]
                tot = cs_t[col:col + 1, last:last + 1]
                decay = jnp.exp(jnp.where(tri, a_b - a_row, -jnp.inf))
                y_j = _dot((cb * decay).astype(BF16), x_j)
                upd_j = _dot((b_t * jnp.exp(tot - a_row)).astype(BF16), x_j)
                y = y_j if y is None else y + y_j
                upd = upd_j if upd is None else upd + upd_j
                a_bc.append(a_b)
                tots.append(tot)
            state = st_ref[pair]
            y = y + _dot(c_g, state.astype(BF16)) * jnp.exp(jnp.where(lo_half, a_bc[0], a_bc[1]))
            st_ref[pair] = state * jnp.exp(jnp.where(lo_half, tots[0], tots[1])) + upd
            if rev:
                y = y + acc_ref[0, :, sl].astype(F32)
            else:
                y = y + dsk_ref[:, sl] * xs[:, sl]
            o_ref[0, :, sl] = y.astype(o_ref.dtype)


def _ssd_scan(xbc, gates, dtb_row, a_row, extra, *, rev, n_ctx):
    bsz, s, _ = xbc.shape
    n = SCAN_CHUNK
    n_all = s // n
    w = SSD_HEADS * SSD_HEAD_DIM
    bcw = 2 * SSD_GROUPS * SSD_STATE
    assert n == LANES and 2 * SSD_HEAD_DIM == LANES and (SSD_HEADS // SSD_GROUPS) % 2 == 0
    off = GATE_DT + (SSD_HEADS if rev else 0)
    cidx = functools.partial(_scan_chunk_index, rev=rev, n_ctx=n_ctx, n_all=n_all)
    tok = lambda b, c: (b, cidx(c), 0)
    const = lambda b, c: (0, 0)
    expand = jnp.asarray(np.arange(LANES)[:, None] == off + np.arange(w)[None, :] // SSD_HEAD_DIM, BF16)
    in_specs = [pl.BlockSpec((1, n, w), tok),
                pl.BlockSpec((1, n, bcw), lambda b, c: (b, cidx(c), w // bcw)),
                pl.BlockSpec((1, n, LANES), tok),
                pl.BlockSpec((1, LANES), const),
                pl.BlockSpec((1, LANES), const),
                pl.BlockSpec((LANES, w), const)]
    if rev:
        in_specs.append(pl.BlockSpec((1, n, w), tok))
    else:
        in_specs.append(pl.BlockSpec((1, w), const))
    return pl.pallas_call(
        functools.partial(_ssd_kernel, rev=rev, off=off),
        out_shape=jax.ShapeDtypeStruct((bsz, s, w), BF16),
        grid=(bsz, n_all), in_specs=in_specs,
        out_specs=pl.BlockSpec((1, n, w), tok),
        scratch_shapes=[pltpu.VMEM((SSD_HEADS // 2, SSD_STATE, LANES), F32)],
        compiler_params=_params(2), name="ssd_scan_bwd" if rev else "ssd_scan_fwd",
    )(xbc, xbc, gates, dtb_row, a_row, expand, extra)


def _mlstm_kernel(*refs, rev, d):
    if rev:
        q_ref, k_ref, v_ref, g_ref, ib_ref, fb_ref, acc_ref, o_ref, c_st, n_st, m_st = refs
    else:
        q_ref, k_ref, v_ref, g_ref, ib_ref, fb_ref, o_ref, c_st, n_st, m_st = refs

    @pl.when(pl.program_id(1) == 0)
    def _():
        c_st[...] = jnp.zeros_like(c_st)
        n_st[...] = jnp.zeros_like(n_st)
        m_st[...] = jnp.zeros_like(m_st)

    n = q_ref.shape[1]
    last = 0 if rev else n - 1
    tri = _tri(n, rev)
    g = g_ref[0]
    log_i = g + ib_ref[...]
    log_f = -_softplus(-(g + fb_ref[...]))
    cs = _cumsum_dot(tri, log_f)
    cs_t = cs.T
    li_t = log_i.T
    for h in range(ML_HEADS):
        ci = GATE_I + ML_HEADS * d + h
        cf = GATE_F + ML_HEADS * d + h
        b_col = cs[:, cf:cf + 1]
        b_row = cs_t[cf:cf + 1, :]
        li_row = li_t[ci:ci + 1, :]
        tot = cs_t[cf:cf + 1, last:last + 1]
        dmat = jnp.where(tri, b_col - b_row + li_row, -jnp.inf)
        m_intra = jnp.max(dmat, axis=-1, keepdims=True)
        a_end = tot - b_row + li_row
        m_loc = jnp.max(a_end, axis=-1, keepdims=True)
        w_end = jnp.exp(a_end - m_loc)
        m_prev = m_st[h, 0:1, 0:1]
        m_inter = b_col + m_prev
        m_t = jnp.maximum(m_intra, m_inter)
        w_inter = jnp.exp(m_inter - m_t)
        qh = q_ref[0, :, h * ML_DK:(h + 1) * ML_DK]
        kf = k_ref[0, :, h * ML_DK:(h + 1) * ML_DK].astype(F32) * (ML_DK ** -0.5)
        kb = kf.astype(BF16)
        vh = v_ref[0, :, h * ML_DV:(h + 1) * ML_DV]
        s_mat = _dot_nt(qh, kb) * jnp.exp(dmat - m_t)
        c_prev = c_st[h]
        n_prev = n_st[h, 0:1, :]
        num = _dot(s_mat.astype(BF16), vh) + _dot(qh, c_prev.astype(BF16)) * w_inter
        den = (jnp.sum(s_mat, axis=-1, keepdims=True)
               + jnp.sum(qh.astype(F32) * n_prev, axis=-1, keepdims=True) * w_inter)
        den = jnp.maximum(jnp.abs(den), jnp.exp(-m_t))
        out = num / den
        kw_t = (kf.T * w_end).astype(BF16)
        c_chunk = _dot(kw_t, vh)
        n_chunk = _dot(jnp.broadcast_to(w_end, (8, n)).astype(BF16), kb)
        m_new = jnp.maximum(tot + m_prev, m_loc)
        a_sc = jnp.exp(tot + m_prev - m_new)
        b_sc = jnp.exp(m_loc - m_new)
        c_st[h] = c_prev * a_sc + c_chunk * b_sc
        n_st[h] = n_st[h] * a_sc + n_chunk * b_sc
        m_st[h] = jnp.broadcast_to(m_new, m_st.shape[1:])
        if rev:
            out = out + acc_ref[0, :, h * ML_DV:(h + 1) * ML_DV].astype(F32)
        o_ref[0, :, h * ML_DV:(h + 1) * ML_DV] = out.astype(o_ref.dtype)


def _mlstm_scan(q, k, v, gates, ib_row, fb_row, acc, *, rev, n_ctx):
    bsz, s, _ = q.shape
    n = SCAN_CHUNK
    n_all = s // n
    cidx = functools.partial(_scan_chunk_index, rev=rev, n_ctx=n_ctx, n_all=n_all)
    tok = lambda b, c: (b, cidx(c), 0)
    qw, vw = ML_HEADS * ML_DK, ML_HEADS * ML_DV
    in_specs = [pl.BlockSpec((1, n, qw), tok), pl.BlockSpec((1, n, qw), tok),
                pl.BlockSpec((1, n, vw), tok), pl.BlockSpec((1, n, LANES), tok),
                pl.BlockSpec((1, LANES), lambda b, c: (0, 0)),
                pl.BlockSpec((1, LANES), lambda b, c: (0, 0))]
    args = [q, k, v, gates, ib_row, fb_row]
    if rev:
        in_specs.append(pl.BlockSpec((1, n, vw), tok))
        args.append(acc)
    return pl.pallas_call(
        functools.partial(_mlstm_kernel, rev=rev, d=1 if rev else 0),
        out_shape=jax.ShapeDtypeStruct((bsz, s, vw), F32),
        grid=(bsz, n_all), in_specs=in_specs,
        out_specs=pl.BlockSpec((1, n, vw), tok),
        scratch_shapes=[pltpu.VMEM((ML_HEADS, ML_DK, ML_DV), F32),
                        pltpu.VMEM((ML_HEADS, 8, ML_DK), F32),
                        pltpu.VMEM((ML_HEADS, 8, LANES), F32)],
        compiler_params=_params(2), name="mlstm_scan_bwd" if rev else "mlstm_scan_fwd",
    )(*args)


def _post_kernel(*refs, mode):
    if mode == "mix":
        (x_ref, y_ref, z_ref, hm_ref, og_ref, snw_ref, mnw_ref, wa_ref, wb_ref,
         mod_ref, n2_ref, wr_ref, br_ref, xo_ref, h2_ref, lg_ref) = refs
        y = _rms(y_ref[0].astype(F32) * _silu(z_ref[0].astype(F32)), snw_ref[...])
        og = og_ref[0].astype(F32)
        o = _dot(y.astype(BF16), wa_ref[...])
        parts = []
        for h in range(ML_HEADS):
            sl = slice(h * ML_DV, (h + 1) * ML_DV)
            parts.append(_rms(hm_ref[0, :, sl], mnw_ref[:, sl]) * jax.nn.sigmoid(og[:, sl]))
        o = o + _dot(jnp.concatenate(parts, axis=-1).astype(BF16), wb_ref[...])
    else:
        (x_ref, a_ref, wa_ref, mod_ref, n2_ref, wr_ref, br_ref, xo_ref, h2_ref, lg_ref) = refs
        o = _dot(a_ref[0], wa_ref[...])
    x_new = x_ref[0] + mod_ref[0, 2:3, :] * o
    xo_ref[0] = x_new
    h2 = _rms(x_new, n2_ref[...]) * (1.0 + mod_ref[0, 4:5, :]) + mod_ref[0, 3:4, :]
    h2_ref[0] = h2
    lg_ref[0] = _dot_split(h2, wr_ref) + br_ref[...]


def _post_call(mode, x, acts, rows, mats, mod_l, norm2_w, w_router, b_router, x_tile_off, n_tiles, ctx_tiles, ctx_row):
    bsz, _, d = x.shape
    tm = ROW_TILE
    s_out = n_tiles * tm
    tok = lambda b, i: (b, i, 0)
    const = lambda b, i: (0, 0)
    mod_idx = lambda b, i: (jnp.where(i + x_tile_off < ctx_tiles, ctx_row, b), 0, 0)
    in_specs = [pl.BlockSpec((1, tm, d), lambda b, i: (b, i + x_tile_off, 0))]
    in_specs += [pl.BlockSpec((1, tm, a.shape[2]), tok) for a in acts]
    in_specs += [pl.BlockSpec(r.shape, const) for r in rows]
    in_specs += [pl.BlockSpec(m.shape, const) for m in mats]
    in_specs += [pl.BlockSpec((1, 6, d), mod_idx), pl.BlockSpec((1, d), const),
                 pl.BlockSpec(w_router.shape, lambda b, i: (0, 0, 0)), pl.BlockSpec((1, LANES), const)]
    out_shape = [jax.ShapeDtypeStruct((bsz, s_out, d), F32),
                 jax.ShapeDtypeStruct((bsz, s_out, d), F32),
                 jax.ShapeDtypeStruct((bsz, s_out, LANES), F32)]
    out_specs = [pl.BlockSpec((1, tm, d), tok), pl.BlockSpec((1, tm, d), tok),
                 pl.BlockSpec((1, tm, LANES), tok)]
    return pl.pallas_call(
        functools.partial(_post_kernel, mode=mode),
        out_shape=out_shape, grid=(bsz, n_tiles), in_specs=in_specs, out_specs=out_specs,
        compiler_params=_params(2), name="post_" + mode,
    )(x, *acts, *rows, *mats, mod_l, norm2_w.reshape(1, d), w_router, b_router)


def _router_kernel(lg_ref, route_ref, cnt_ref, src_ref, all_ref, off_ref, inv_ref, *, moe_tile, n_tokens):
    phase = pl.program_id(0)
    i = pl.program_id(1)
    tm = lg_ref.shape[0]
    rows = pl.ds(pl.multiple_of(i * tm, tm), tm)

    @pl.when((phase == 0) & (i == 0))
    def _():
        cnt_ref[...] = jnp.zeros_like(cnt_ref)

    @pl.when(phase == 0)
    def _():
        all_ref[rows, :] = _route_fields(lg_ref[...], cnt_ref)

    @pl.when((phase == 1) & (i == 0))
    def _():
        tiles = jnp.ceil(cnt_ref[...] * (1.0 / moe_tile))
        r = lax.broadcasted_iota(jnp.int32, (LANES, LANES), 0)
        c = lax.broadcasted_iota(jnp.int32, (LANES, LANES), 1)
        earlier = jnp.where(r < c, 1.0, 0.0).astype(BF16)
        off_ref[...] = _dot(tiles.astype(BF16), earlier) * float(moe_tile)

    @pl.when(phase == 1)
    def _():
        f = all_ref[rows, :]
        lane = lax.broadcasted_iota(jnp.int32, f.shape, 1).astype(F32)
        off = off_ref[0:1, :]
        pos1 = jnp.sum(jnp.where(lane == f[:, 0:1], off, 0.0), axis=-1, keepdims=True) + f[:, 4:5]
        pos2 = jnp.sum(jnp.where(lane == f[:, 1:2], off, 0.0), axis=-1, keepdims=True) + f[:, 5:6]
        route_ref[...] = jnp.where(lane == 6.0, pos1, jnp.where(lane == 7.0, pos2, f))

        n_blk = inv_ref.shape[0]
        pos_t = jnp.where(lane == 0.0, pos1, jnp.where(lane == 1.0, pos2, 0.0)).T
        blk = lax.broadcasted_iota(jnp.int32, (n_blk, tm), 0).astype(F32)
        tok = (i * tm + lax.broadcasted_iota(jnp.int32, (tm, 1), 0)).astype(F32)
        tok_hi = jnp.floor(tok * (1.0 / LANES))
        tok_lo = tok - tok_hi * LANES
        lhs, rhs = [], []
        for k, pos in enumerate((pos1, pos2)):
            blk_of = jnp.floor(pos_t[k:k + 1, :] * (1.0 / LANES))
            lhs.append(jnp.where(blk == blk_of, 1.0, 0.0).astype(BF16))
            hit = lane == pos - jnp.floor(pos * (1.0 / LANES)) * LANES
            rhs.append(jnp.concatenate([jnp.where(hit, tok_hi, 0.0), jnp.where(hit, tok_lo, 0.0),
                                        jnp.where(hit, 1.0, 0.0)], axis=1).astype(BF16))
        upd = _dot(jnp.concatenate(lhs, axis=1), jnp.concatenate(rhs, axis=0))

        @pl.when(i == 0)
        def _():
            inv_ref[...] = upd

        @pl.when(i > 0)
        def _():
            inv_ref[...] = inv_ref[...] + upd

    @pl.when((phase == 1) & (i == pl.num_programs(1) - 1))
    def _():
        acc = inv_ref[...]
        n_blk = acc.shape[0]
        slot = (lax.broadcasted_iota(jnp.int32, (n_blk, LANES), 0) * LANES
                + lax.broadcasted_iota(jnp.int32, (n_blk, LANES), 1)).astype(F32)
        spare = slot - n_tokens * jnp.floor((slot + 0.5) * (1.0 / n_tokens))
        src = jnp.where(acc[:, 2 * LANES:] > 0.0, acc[:, :LANES] * LANES + acc[:, LANES:2 * LANES], spare)
        src_ref[...] = src.astype(jnp.int32)


def _route_fields(lg, cnt_ref):
    tm = lg.shape[0]
    lane = lax.broadcasted_iota(jnp.int32, lg.shape, 1).astype(F32)
    big = float(LANES)
    is_g = (lane >= ROUTE_G) & (lane < ROUTE_G + MOE_GROUPS)
    lgg = jnp.where(is_g, lg, -jnp.inf)
    g_max = jnp.max(lgg, axis=-1, keepdims=True)
    g_idx = jnp.min(jnp.where(lgg == g_max, lane - ROUTE_G, big), axis=-1, keepdims=True)
    g_prob = 1.0 / jnp.sum(jnp.exp(lgg - g_max), axis=-1, keepdims=True)
    lo = g_idx * MOE_EXPERTS
    le = jnp.where((lane >= lo) & (lane < lo + MOE_EXPERTS), lg, -jnp.inf)
    l1 = jnp.max(le, axis=-1, keepdims=True)
    i1 = jnp.min(jnp.where(le == l1, lane, big), axis=-1, keepdims=True)
    le2 = jnp.where(lane == i1, -jnp.inf, le)
    l2 = jnp.max(le2, axis=-1, keepdims=True)
    i2 = jnp.min(jnp.where(le2 == l2, lane, big), axis=-1, keepdims=True)
    r = jnp.exp(l2 - l1)
    w1 = g_prob / (1.0 + r)
    w2 = w1 * r
    oh1 = jnp.where(lane == i1, 1.0, 0.0)
    oh2 = jnp.where(lane == i2, 1.0, 0.0)
    oh = oh1 + oh2
    row = lax.broadcasted_iota(jnp.int32, (tm, tm), 0)
    col = lax.broadcasted_iota(jnp.int32, (tm, tm), 1)
    before = jnp.where(col < row, 1.0, 0.0).astype(BF16)
    prefix = _dot(before, oh.astype(BF16)) + cnt_ref[0:1, :]
    rank1 = jnp.sum(prefix * oh1, axis=-1, keepdims=True)
    rank2 = jnp.sum(prefix * oh2, axis=-1, keepdims=True)
    cnt_ref[...] = cnt_ref[...] + jnp.sum(oh, axis=0, keepdims=True)
    fields = (i1, i2, w1, w2, rank1, rank2)
    out = jnp.zeros_like(lg)
    for j, f in enumerate(fields):
        out = jnp.where(lane == float(j), f, out)
    return out


def _router(logits, n_sorted):
    t = logits.shape[0]
    tm = ROW_TILE
    n_blk = n_sorted // LANES
    fixed = lambda p, i: (0, 0)
    return pl.pallas_call(
        functools.partial(_router_kernel, moe_tile=MOE_TILE, n_tokens=t),
        out_shape=[jax.ShapeDtypeStruct((t, LANES), F32), jax.ShapeDtypeStruct((8, LANES), F32),
                   jax.ShapeDtypeStruct((n_blk, LANES), jnp.int32)],
        grid=(2, t // tm),
        in_specs=[pl.BlockSpec((tm, LANES), lambda p, i: (i * (1 - p), 0))],
        out_specs=[pl.BlockSpec((tm, LANES), lambda p, i: (i * p, 0)), pl.BlockSpec((8, LANES), fixed),
                   pl.BlockSpec((n_blk, LANES), fixed)],
        scratch_shapes=[pltpu.VMEM((t, LANES), F32), pltpu.VMEM((8, LANES), F32),
                        pltpu.VMEM((n_blk, 3 * LANES), F32)],
        compiler_params=_params(2), name="router",
    )(logits)


def _moe_kernel(te_ref, nt_ref, x_ref, wg_ref, wu_ref, wd_ref, o_ref, wgb, wub, wdb):
    i = pl.program_id(0)
    valid = i < nt_ref[0]
    fresh = (i == 0) | (te_ref[i] != te_ref[jnp.maximum(i - 1, 0)])

    @pl.when(valid & fresh)
    def _():
        wgb[...] = wg_ref[0].astype(BF16)
        wub[...] = wu_ref[0].astype(BF16)
        wdb[...] = wd_ref[0].astype(BF16)

    @pl.when(valid)
    def _():
        x = x_ref[...].astype(BF16)
        act = _silu(_dot(x, wgb[...])) * _dot(x, wub[...])
        o_ref[...] = _dot(act.astype(BF16), wdb[...]).astype(o_ref.dtype)

    @pl.when(jnp.logical_not(valid))
    def _():
        o_ref[...] = jnp.zeros_like(o_ref)


def _moe_experts(x_sorted, tile_expert, n_tiles_used, wg, wu, wd):
    tm = MOE_TILE
    rows, d = x_sorted.shape
    ff = wg.shape[2]
    grid_spec = pltpu.PrefetchScalarGridSpec(
        num_scalar_prefetch=2, grid=(rows // tm,),
        in_specs=[pl.BlockSpec((tm, d), lambda i, te, nt: (i, 0)),
                  pl.BlockSpec((1, d, ff), lambda i, te, nt: (te[i], 0, 0)),
                  pl.BlockSpec((1, d, ff), lambda i, te, nt: (te[i], 0, 0)),
                  pl.BlockSpec((1, ff, d), lambda i, te, nt: (te[i], 0, 0))],
        out_specs=pl.BlockSpec((tm, d), lambda i, te, nt: (i, 0)),
        scratch_shapes=[pltpu.VMEM((d, ff), BF16), pltpu.VMEM((d, ff), BF16), pltpu.VMEM((ff, d), BF16)])
    return pl.pallas_call(
        _moe_kernel, out_shape=jax.ShapeDtypeStruct((rows, d), F32), grid_spec=grid_spec,
        compiler_params=_params(1), name="moe_experts",
    )(tile_expert, n_tiles_used, x_sorted, wg, wu, wd)


def _combine_kernel(x_ref, y1_ref, y2_ref, rt_ref, mod_ref, o_ref):
    f = rt_ref[0, :, 2:3] * y1_ref[0] + rt_ref[0, :, 3:4] * y2_ref[0]
    o_ref[0] = x_ref[0] + mod_ref[0, 5:6, :] * f


def _combine(x, y1, y2, route, mod_l, ctx_tiles, ctx_row):
    bsz, s, d = x.shape
    tm = ROW_TILE
    tok = lambda b, i: (b, i, 0)
    mod_idx = lambda b, i: (jnp.where(i < ctx_tiles, ctx_row, b), 0, 0)
    return pl.pallas_call(
        _combine_kernel, out_shape=jax.ShapeDtypeStruct(x.shape, F32), grid=(bsz, s // tm),
        in_specs=[pl.BlockSpec((1, tm, d), tok), pl.BlockSpec((1, tm, d), tok), pl.BlockSpec((1, tm, d), tok),
                  pl.BlockSpec((1, tm, LANES), tok), pl.BlockSpec((1, 6, d), mod_idx)],
        out_specs=pl.BlockSpec((1, tm, d), tok),
        compiler_params=_params(2), name="moe_combine",
    )(x, y1.reshape(x.shape), y2.reshape(x.shape), route.reshape(bsz, s, LANES), mod_l)


def _hier_moe(h2, logits, layer, wg, wu, wd):
    bsz, s, d = h2.shape
    t = bsz * s
    tm = MOE_TILE
    n_tiles = 2 * t // tm + N_EXPERTS
    route, counts, src = _router(logits.reshape(t, LANES), n_tiles * tm)
    tiles_per = (counts[0, :N_EXPERTS].astype(jnp.int32) + tm - 1) // tm
    tile_end = jnp.cumsum(tiles_per)
    tile_ids = jnp.arange(n_tiles, dtype=jnp.int32)
    tile_expert = jnp.minimum(jnp.sum((tile_end[None, :] <= tile_ids[:, None]).astype(jnp.int32), axis=1),
                              N_EXPERTS - 1) + layer * N_EXPERTS
    pos = route[:, 6:8].astype(jnp.int32)
    x_sorted = jnp.take(h2.reshape(t, d), src.reshape(-1), axis=0, mode="clip")
    y_sorted = _moe_experts(x_sorted, tile_expert, tile_end[-1:], wg.reshape(-1, d, wg.shape[-1]),
                            wu.reshape(-1, d, wu.shape[-1]), wd.reshape(-1, wd.shape[-2], d))
    y1 = jnp.take(y_sorted, pos[:, 0], axis=0, mode="clip")
    y2 = jnp.take(y_sorted, pos[:, 1], axis=0, mode="clip")
    return y1, y2, route


def _attn_kernel(q_ref, k_ref, v_ref, bias_ref, qw_ref, kw_ref, o_ref, qn_ref, kn_ref, *, ctx_len, rows):
    lane = lax.broadcasted_iota(jnp.int32, (1, LANES), 1)
    first = lane < NA_HEAD_DIM

    r_head = lax.broadcasted_iota(jnp.int32, (LANES, LANES), 0) // NA_HEAD_DIM
    c_head = lax.broadcasted_iota(jnp.int32, (LANES, LANES), 1) // NA_HEAD_DIM
    same_head = jnp.where(r_head == c_head, 1.0, 0.0).astype(BF16)

    def head_norm(x, w):
        ms = _dot((x * x).astype(BF16), same_head) * (1.0 / NA_HEAD_DIM)
        return x * lax.rsqrt(ms + RMS_EPS) * w

    kn_ref[...] = head_norm(k_ref[0].astype(F32), kw_ref[...]).astype(BF16)
    qn_ref[...] = (head_norm(q_ref[0, ctx_len:, :].astype(F32), qw_ref[...]) * NA_HEAD_DIM ** -0.5).astype(BF16)
    n_groups = rows // ATTN_GROUP_ROWS
    n_q = ATTN_GROUP_ROWS * GRID_W
    n_loc = ATTN_KEY_ROWS * GRID_W
    k_ctx = kn_ref[0:ctx_len, :]
    v_ctx = v_ref[0, 0:ctx_len, :]

    def one_group(g):
        kind = jnp.where(g == 0, 0, jnp.where(g == n_groups - 1, 2, 1))
        kr0 = jnp.clip(g * ATTN_GROUP_ROWS - NA_KH // 2, 0, rows - ATTN_KEY_ROWS)
        q_rows = pl.ds(pl.multiple_of(g * n_q, n_q), n_q)
        q = qn_ref[q_rows, :]
        zero = jnp.zeros_like(q)
        q2 = jnp.concatenate([jnp.where(first, q, zero), jnp.where(first, zero, q)], axis=0)
        k_off = pl.multiple_of(ctx_len + kr0 * GRID_W, GRID_W)
        s_loc = _dot_nt(q2, kn_ref[pl.ds(k_off, n_loc), :])
        s_ctx = _dot_nt(q2, k_ctx)
        p_loc, p_ctx, inv = [], [], []
        for hh in range(2):
            sl = s_loc[hh * n_q:(hh + 1) * n_q] + bias_ref[hh, kind]
            sc = s_ctx[hh * n_q:(hh + 1) * n_q]
            m = jnp.maximum(jnp.max(sl, axis=-1, keepdims=True), jnp.max(sc, axis=-1, keepdims=True))
            el = jnp.exp(sl - m)
            ec = jnp.exp(sc - m)
            inv.append(1.0 / (jnp.sum(el, axis=-1, keepdims=True) + jnp.sum(ec, axis=-1, keepdims=True)))
            p_loc.append(el.astype(BF16))
            p_ctx.append(ec.astype(BF16))
        o = (_dot(jnp.concatenate(p_loc, axis=0), v_ref[0, pl.ds(k_off, n_loc), :])
             + _dot(jnp.concatenate(p_ctx, axis=0), v_ctx))
        o_ref[0, q_rows, :] = jnp.where(first, o[:n_q] * inv[0], o[n_q:] * inv[1]).astype(o_ref.dtype)

    def body(i, carry):
        for j in range(ATTN_GROUPS_PER_TRIP):
            one_group(i * ATTN_GROUPS_PER_TRIP + j)
        return carry

    lax.fori_loop(0, n_groups // ATTN_GROUPS_PER_TRIP, body, 0)


def _attn_group_layout(rows):
    n_groups = rows // ATTN_GROUP_ROWS
    assert rows % ATTN_GROUP_ROWS == 0 and rows >= ATTN_KEY_ROWS and n_groups >= 2
    u = np.arange(ATTN_GROUP_ROWS)[:, None]
    i = np.arange(ATTN_KEY_ROWS)[None, :]

    def layout(g):
        r = g * ATTN_GROUP_ROWS + u
        r0 = np.clip(r - NA_KH // 2, 0, rows - NA_KH)
        kr = np.clip(g * ATTN_GROUP_ROWS - NA_KH // 2, 0, rows - ATTN_KEY_ROWS) + i
        return (kr >= r0) & (kr < r0 + NA_KH), kr - r + NA_KH - 1

    kinds = [layout(0), layout(1), layout(n_groups - 1)]
    for g in range(1, n_groups - 1):
        valid, d = layout(g)
        assert (valid == kinds[1][0]).all() and (d[valid] == kinds[1][1][valid]).all()
    return np.stack([k[0] for k in kinds]), np.stack([k[1] for k in kinds])


def _bias_windows(rpb, rows):
    qc = np.arange(GRID_W)
    c0 = np.clip(qc - NA_KW // 2, 0, GRID_W - NA_KW)
    kc = np.arange(GRID_W)
    inwin = (kc[None, :] >= c0[:, None]) & (kc[None, :] < c0[:, None] + NA_KW)
    coff = kc[None, :] - qc[:, None] + NA_KW - 1
    pick = (coff[..., None] == np.arange(2 * NA_KW - 1)) & inwin[..., None]
    tab = jnp.einsum("hdo,qko->dhqk", rpb.astype(F32), jnp.asarray(pick, F32), precision=HIGHEST)
    tab = jnp.where(inwin[None, None], tab, NEG)
    valid, d = _attn_group_layout(rows)
    masked = jnp.full(tab.shape[1:], NEG, F32)
    blocks = []
    for kind in range(valid.shape[0]):
        for u in range(ATTN_GROUP_ROWS):
            blocks.append(jnp.concatenate(
                [tab[int(d[kind, u, i])] if valid[kind, u, i] else masked for i in range(ATTN_KEY_ROWS)], axis=-1))
    win = jnp.stack(blocks, axis=1)
    return win.reshape(rpb.shape[0], valid.shape[0], ATTN_GROUP_ROWS * GRID_W, ATTN_KEY_ROWS * GRID_W)


def _neighbourhood_attention(q, k, v, bias, qn_w, kn_w, ctx_len):
    bsz, s, w = q.shape
    seq = s - ctx_len
    rows = seq // GRID_W
    n_pairs = NA_HEADS // 2
    pair = lambda b, p: (b, 0, p)
    row2 = lambda b, p: (0, 0)
    return pl.pallas_call(
        functools.partial(_attn_kernel, ctx_len=ctx_len, rows=rows),
        out_shape=jax.ShapeDtypeStruct((bsz, seq, w), BF16),
        grid=(bsz, n_pairs),
        in_specs=[pl.BlockSpec((1, s, LANES), pair), pl.BlockSpec((1, s, LANES), pair),
                  pl.BlockSpec((1, s, LANES), pair),
                  pl.BlockSpec((2,) + bias.shape[1:], lambda b, p: (p, 0, 0, 0)),
                  pl.BlockSpec((1, LANES), row2), pl.BlockSpec((1, LANES), row2)],
        out_specs=pl.BlockSpec((1, seq, LANES), pair),
        scratch_shapes=[pltpu.VMEM((seq, LANES), BF16), pltpu.VMEM((s, LANES), BF16)],
        compiler_params=_params(2), name="neighbourhood_attention",
    )(q, k, v, bias, jnp.tile(qn_w, 2).reshape(1, LANES), jnp.tile(kn_w, 2).reshape(1, LANES))


def _pad_row(pieces, width=LANES):
    row = jnp.zeros((width,), F32)
    for off, vec in pieces:
        row = row.at[off:off + vec.shape[0]].set(vec.astype(F32))
    return row.reshape(1, width)


def _router_params(w_group, b_group, w_expert, b_expert):
    d = w_group.shape[0]
    w = jnp.zeros((d, LANES), F32).at[:, :N_EXPERTS].set(w_expert).at[:, ROUTE_G:ROUTE_G + MOE_GROUPS].set(w_group)
    return _split_weight(w), _pad_row([(0, b_expert), (ROUTE_G, b_group)])


def kernel(x, c, ctx, c_ctx, norm1_w, norm2_w, mod_w, mod_b, ab_w_in, ab_conv_w, ab_conv_b, ssd_a_log, ssd_dt_bias, ssd_d, ssd_norm_w, ml_i_bias, ml_f_bias, ml_norm_w, ab_w_out, na_w_qkv, na_q_norm, na_k_norm, na_rpb, na_w_out, moe_w_group, moe_b_group, moe_w_expert, moe_b_expert, moe_w_gate, moe_w_up, moe_w_down):
    bsz, seq, d = x.shape
    ctx_len = ctx.shape[1]
    depth = mod_w.shape[0]
    assert depth == 2 and ctx_len % ROW_TILE == 0 and seq % ROW_TILE == 0 and bsz < 8
    ctx_tiles = ctx_len // ROW_TILE
    lat_tiles = seq // ROW_TILE
    ctx_row = bsz

    cvec = jnp.zeros((8, d), F32).at[:bsz].set(c).at[bsz].set(c_ctx)
    mod = _mod_vectors(cvec, mod_w, mod_b)
    xs = jnp.concatenate([ctx, x], axis=1)

    ssd_w = SSD_HEADS * SSD_HEAD_DIM
    xbc_w = ssd_w + 2 * SSD_GROUPS * SSD_STATE
    qk_w, v_w = ML_HEADS * ML_DK, ML_HEADS * ML_DV
    sizes = (ssd_w, xbc_w, 2 * SSD_HEADS, qk_w, qk_w, v_w, v_w, 2 * ML_HEADS, 2 * ML_HEADS)
    w_z, w_xbc, w_dt, w_q, w_k, w_v, w_o, w_i, w_f = jnp.split(ab_w_in[0], np.cumsum(sizes)[:-1].tolist(), axis=1)
    w_gate = jnp.zeros((d, LANES), F32).at[:, :GATE_F + 2 * ML_HEADS].set(jnp.concatenate([w_dt, w_i, w_f], axis=1))
    weights = [w.astype(BF16) for w in (w_z, w_xbc, w_q, w_k, w_v, w_o)] + [_split_weight(w_gate)]
    z, xbc, q, k, v, og, gates = _norm_mod_matmul(xs, norm1_w[0], mod[0], weights, [BF16] * 6 + [F32],
                                                  ctx_tiles, ctx_row)
    xbc = _conv_silu(xbc, ab_conv_w[0], ab_conv_b[0], ctx_len)
    n_ctx = ctx_len // SCAN_CHUNK
    a_neg = -jnp.exp(ssd_a_log[0].astype(F32))
    dsk_row = jnp.repeat(ssd_d[0].astype(F32), SSD_HEAD_DIM).reshape(1, ssd_w)
    y = None
    hm = None
    for dr in range(2):
        rev = dr == 1
        dtb_row = _pad_row([(GATE_DT + dr * SSD_HEADS, ssd_dt_bias[0, dr])])
        a_row = _pad_row([(GATE_DT + dr * SSD_HEADS, a_neg[dr])])
        y = _ssd_scan(xbc, gates, dtb_row, a_row, y if rev else dsk_row, rev=rev, n_ctx=n_ctx)
        ib_row = _pad_row([(GATE_I + dr * ML_HEADS, ml_i_bias[0, dr])])
        fb_row = _pad_row([(GATE_F + dr * ML_HEADS, ml_f_bias[0, dr])])
        hm = _mlstm_scan(q, k, v, gates, ib_row, fb_row, hm, rev=rev, n_ctx=n_ctx)
    w_r, b_r = _router_params(moe_w_group[0], moe_b_group[0], moe_w_expert[0], moe_b_expert[0])
    w_out = ab_w_out[0].astype(BF16)
    x1, h2, logits = _post_call(
        "mix", xs, [y, z, hm, og],
        [ssd_norm_w[0].reshape(1, ssd_w), ml_norm_w[0].reshape(1, v_w)], [w_out[:ssd_w], w_out[ssd_w:]],
        mod[0], norm2_w[0], w_r, b_r, 0, ctx_tiles + lat_tiles, ctx_tiles, ctx_row)
    moe0 = _hier_moe(h2, logits, 0, moe_w_gate, moe_w_up, moe_w_down)

    w_qkv = na_w_qkv[0].astype(BF16)
    na_w = NA_HEADS * NA_HEAD_DIM
    q, k, v, xs = _norm_mod_matmul(x1, norm1_w[1], mod[1],
                                   [w_qkv[:, :na_w], w_qkv[:, na_w:2 * na_w], w_qkv[:, 2 * na_w:]],
                                   [BF16] * 3, ctx_tiles, ctx_row, pending_moe=(*moe0, mod[0]))
    bias = _bias_windows(na_rpb[0], seq // GRID_W)
    attn = _neighbourhood_attention(q, k, v, bias, na_q_norm[0], na_k_norm[0], ctx_len)
    w_r, b_r = _router_params(moe_w_group[1], moe_b_group[1], moe_w_expert[1], moe_b_expert[1])
    x1, h2, logits = _post_call("attn", xs, [attn], [], [na_w_out[0].astype(BF16)],
                                mod[1], norm2_w[1], w_r, b_r, ctx_tiles, lat_tiles, ctx_tiles, ctx_row)
    y1, y2, route = _hier_moe(h2, logits, 1, moe_w_gate, moe_w_up, moe_w_down)
    return _combine(x1, y1, y2, route, mod[1], 0, ctx_row)
```

```python
import functools

import numpy as np
import jax
import jax.numpy as jnp
from jax import lax
from jax.experimental import pallas as pl
from jax.experimental.pallas import tpu as pltpu

F32 = jnp.float32
BF16 = jnp.bfloat16
HIGHEST = lax.Precision.HIGHEST

RMS_EPS = 1e-6
GRID_W = 64
SSD_HEADS = 16
SSD_HEAD_DIM = 64
SSD_GROUPS = 2
SSD_STATE = 128
ML_HEADS = 4
ML_DK = 128
ML_DV = 256
NA_HEADS = 16
NA_HEAD_DIM = 64
NA_KH = 8
NA_KW = 16
MOE_GROUPS = 4
MOE_EXPERTS = 8
N_EXPERTS = MOE_GROUPS * MOE_EXPERTS

LANES = 128
ROW_TILE = 256
SCAN_CHUNK = 128
ML_CHUNK = 256
MOE_TILE = 256
ATTN_GROUP_ROWS = 4
ATTN_KEY_ROWS = ATTN_GROUP_ROWS + NA_KH - 1
ATTN_GROUPS_PER_TRIP = 4
VMEM_LIMIT = 56 * 1024 * 1024

GATE_DT = 0
GATE_I = 2 * SSD_HEADS
GATE_F = GATE_I + 2 * ML_HEADS
ROUTE_G = N_EXPERTS
NEG = -1e30


def _params(n_axes):
    return pltpu.CompilerParams(dimension_semantics=("arbitrary",) * n_axes,
                                vmem_limit_bytes=VMEM_LIMIT)


def _silu(x):
    return x * jax.nn.sigmoid(x)


def _softplus(x):
    return jnp.maximum(x, 0.0) + jnp.log1p(jnp.exp(-jnp.abs(x)))


def _rms(x, w):
    return x * lax.rsqrt(jnp.mean(x * x, axis=-1, keepdims=True) + RMS_EPS) * w


def _dot(a, b):
    return jnp.dot(a, b, preferred_element_type=F32)


def _dot_nt(a, b):
    return lax.dot_general(a, b, (((1,), (1,)), ((), ())), preferred_element_type=F32)


def _dot_hi(a, b):
    return jnp.dot(a, b, precision=HIGHEST, preferred_element_type=F32)


def _split_bf16(x, terms):
    parts = []
    for _ in range(terms - 1):
        p = x.astype(BF16)
        parts.append(p)
        x = x - p.astype(F32)
    parts.append(x.astype(BF16))
    return parts


def _split_weight(w):
    return jnp.stack(_split_bf16(w.astype(F32), 2))


def _dot_split(a, w2_ref):
    a_hi, a_lo = _split_bf16(a, 2)
    return _dot(a_hi, w2_ref[0]) + _dot(a_lo, w2_ref[0]) + _dot(a_hi, w2_ref[1])


def _cumsum_dot(tri, x):
    tri = jnp.where(tri, 1.0, 0.0).astype(BF16)
    hi, mid, lo = _split_bf16(x, 3)
    return _dot(tri, hi) + _dot(tri, mid) + _dot(tri, lo)


def _mod_kernel(c_ref, w_ref, b_ref, o_ref):
    o_ref[0] = _dot_hi(_silu(c_ref[...]), w_ref[0]) + b_ref[0]


def _mod_vectors(cvec, mod_w, mod_b, tn=512):
    depth, d, n = mod_w.shape
    rows = cvec.shape[0]
    out = pl.pallas_call(
        _mod_kernel,
        out_shape=jax.ShapeDtypeStruct((depth, rows, n), F32),
        grid=(depth, n // tn),
        in_specs=[pl.BlockSpec((rows, d), lambda l, j: (0, 0)),
                  pl.BlockSpec((1, d, tn), lambda l, j: (l, 0, j)),
                  pl.BlockSpec((1, 1, tn), lambda l, j: (l, 0, j))],
        out_specs=pl.BlockSpec((1, rows, tn), lambda l, j: (l, 0, j)),
        compiler_params=_params(2),
        name="mod_vectors",
    )(cvec, mod_w, mod_b.reshape(depth, 1, n))
    return out.reshape(depth, rows, 6, d)


def _nmm_kernel(x_ref, *refs, n_out, pending_moe):
    x = x_ref[0]
    if pending_moe:
        y1_ref, y2_ref, rt_ref, pmod_ref = refs[:4]
        refs = refs[4:]
        x = x + pmod_ref[0, 5:6, :] * (rt_ref[0, :, 2:3] * y1_ref[0] + rt_ref[0, :, 3:4] * y2_ref[0])
        refs[-1][0] = x
        refs = refs[:-1]
    nw_ref, mod_ref = refs[:2]
    refs = refs[2:]
    w_refs, o_refs = refs[:n_out], refs[n_out:]
    h = _rms(x, nw_ref[...])
    h = h * (1.0 + mod_ref[0, 1:2, :]) + mod_ref[0, 0:1, :]
    hb = h.astype(BF16)
    for w_ref, o_ref in zip(w_refs, o_refs):
        if len(w_ref.shape) == 3:
            o_ref[0] = _dot_split(h, w_ref)
        else:
            o_ref[0] = _dot(hb, w_ref[...]).astype(o_ref.dtype)


def _norm_mod_matmul(xs, norm_w, mod_l, weights, out_dtypes, ctx_tiles, ctx_row, pending_moe=None):
    bsz, s, d = xs.shape
    tm = ROW_TILE
    tok = lambda b, i: (b, i, 0)
    mod_idx = lambda b, i: (jnp.where(i < ctx_tiles, ctx_row, b), 0, 0)
    args = [xs]
    in_specs = [pl.BlockSpec((1, tm, d), tok)]
    if pending_moe is not None:
        y1, y2, route, mod_prev = pending_moe
        args += [y1.reshape(xs.shape), y2.reshape(xs.shape), route.reshape(bsz, s, LANES), mod_prev]
        in_specs += [pl.BlockSpec((1, tm, d), tok), pl.BlockSpec((1, tm, d), tok),
                     pl.BlockSpec((1, tm, LANES), tok), pl.BlockSpec((1, 6, d), mod_idx)]
    args += [norm_w.reshape(1, d), mod_l, *weights]
    in_specs += [pl.BlockSpec((1, d), lambda b, i: (0, 0)), pl.BlockSpec((1, 6, d), mod_idx)]
    in_specs += [pl.BlockSpec(w.shape, lambda b, i, nd=w.ndim: (0,) * nd) for w in weights]
    out_shape = [jax.ShapeDtypeStruct((bsz, s, w.shape[-1]), dt) for w, dt in zip(weights, out_dtypes)]
    out_specs = [pl.BlockSpec((1, tm, w.shape[-1]), tok) for w in weights]
    if pending_moe is not None:
        out_shape.append(jax.ShapeDtypeStruct(xs.shape, F32))
        out_specs.append(pl.BlockSpec((1, tm, d), tok))
    return pl.pallas_call(
        functools.partial(_nmm_kernel, n_out=len(weights), pending_moe=pending_moe is not None),
        out_shape=out_shape, grid=(bsz, s // tm), in_specs=in_specs, out_specs=out_specs,
        compiler_params=_params(2), name="norm_mod_matmul",
    )(*args)


def _conv_kernel(u_ref, w_ref, b_ref, o_ref, *, ctx_len):
    u = u_ref[0].astype(F32)
    s = u.shape[0]
    t = lax.broadcasted_iota(jnp.int32, u.shape, 0)
    prev = jnp.where((t == 0) | (t == ctx_len), 0.0, pltpu.roll(u, 1, axis=0))
    nxt = jnp.where((t == ctx_len - 1) | (t == s - 1), 0.0, pltpu.roll(u, s - 1, axis=0))
    y = prev * w_ref[0:1, :] + u * w_ref[1:2, :] + nxt * w_ref[2:3, :] + b_ref[...]
    o_ref[0] = _silu(y).astype(o_ref.dtype)


def _conv_silu(u, conv_w, conv_b, ctx_len, tc=LANES):
    bsz, s, ch = u.shape
    return pl.pallas_call(
        functools.partial(_conv_kernel, ctx_len=ctx_len),
        out_shape=jax.ShapeDtypeStruct(u.shape, u.dtype),
        grid=(bsz, ch // tc),
        in_specs=[pl.BlockSpec((1, s, tc), lambda b, j: (b, 0, j)),
                  pl.BlockSpec((3, tc), lambda b, j: (0, j)),
                  pl.BlockSpec((1, tc), lambda b, j: (0, j))],
        out_specs=pl.BlockSpec((1, s, tc), lambda b, j: (b, 0, j)),
        compiler_params=_params(2), name="conv_silu",
    )(u, conv_w, conv_b.reshape(1, ch))


def _scan_chunk_index(c, rev, n_ctx, n_all):
    if not rev:
        return c
    return jnp.where(c < n_ctx, n_ctx - 1 - c, n_ctx + n_all - 1 - c)


def _tri(n, rev):
    row = lax.broadcasted_iota(jnp.int32, (n, n), 0)
    col = lax.broadcasted_iota(jnp.int32, (n, n), 1)
    return (col >= row) if rev else (col <= row)


def _ssd_kernel(*refs, rev, off):
    if rev:
        xs_ref, bc_ref, g_ref, dtb_ref, a_ref, ex_ref, acc_ref, o_ref, st_ref = refs
    else:
        xs_ref, bc_ref, g_ref, dtb_ref, a_ref, ex_ref, dsk_ref, o_ref, st_ref = refs

    @pl.when(pl.program_id(1) == 0)
    def _():
        st_ref[...] = jnp.zeros_like(st_ref)

    n = xs_ref.shape[1]
    last = 0 if rev else n - 1
    tri = _tri(n, rev)
    dt = _softplus(g_ref[0] + dtb_ref[...])
    log_a = dt * a_ref[...]
    cs = _cumsum_dot(tri, log_a)
    cs_t = cs.T
    dt_hi = dt.astype(BF16)
    dt_lo = (dt - dt_hi.astype(F32)).astype(BF16)
    dt_full = _dot(dt_hi, ex_ref[...]) + _dot(dt_lo, ex_ref[...])
    xs = xs_ref[0].astype(F32)
    xdt = (xs * dt_full).astype(BF16)
    lo_half = lax.broadcasted_iota(jnp.int32, (1, LANES), 1) < SSD_HEAD_DIM
    gw = SSD_GROUPS * SSD_STATE
    heads_per_group = SSD_HEADS // SSD_GROUPS
    for g in range(SSD_GROUPS):
        b_g = bc_ref[0, :, g * SSD_STATE:(g + 1) * SSD_STATE]
        c_g = bc_ref[0, :, gw + g * SSD_STATE:gw + (g + 1) * SSD_STATE]
        cb = _dot_nt(c_g, b_g)
        b_t = b_g.astype(F32).T
        for e in range(0, heads_per_group, 2):
            h0 = g * heads_per_group + e
            pair = h0 // 2
            sl = slice(pair * LANES, (pair + 1) * LANES)
            x_pair = xdt[:, sl]
            zero = jnp.zeros_like(x_pair)
            y, upd, a_bc, tots = None, None, [], []
            for j in range(2):
                col = off + h0 + j
                x_j = jnp.where(lo_half, x_pair, zero) if j == 0 else jnp.where(lo_half, zero, x_pair)
                a_b = jnp.broadcast_to(cs[:, col:col + 1], (n, LANES))
                a_row = cs_t[col:col + 1, :]
                tot = cs_t[col:col + 1, last:last + 1]
                decay = jnp.exp(jnp.where(tri, a_b - a_row, -jnp.inf))
                y_j = _dot((cb * decay).astype(BF16), x_j)
                upd_j = _dot((b_t * jnp.exp(tot - a_row)).astype(BF16), x_j)
                y = y_j if y is None else y + y_j
                upd = upd_j if upd is None else upd + upd_j
                a_bc.append(a_b)
                tots.append(tot)
            state = st_ref[pair]
            y = y + _dot(c_g, state.astype(BF16)) * jnp.exp(jnp.where(lo_half, a_bc[0], a_bc[1]))
            st_ref[pair] = state * jnp.exp(jnp.where(lo_half, tots[0], tots[1])) + upd
            if rev:
                y = y + acc_ref[0, :, sl].astype(F32)
            else:
                y = y + dsk_ref[:, sl] * xs[:, sl]
            o_ref[0, :, sl] = y.astype(o_ref.dtype)


def _ssd_scan(xbc, gates, dtb_row, a_row, extra, *, rev, n_ctx):
    bsz, s, _ = xbc.shape
    n = SCAN_CHUNK
    n_all = s // n
    w = SSD_HEADS * SSD_HEAD_DIM
    bcw = 2 * SSD_GROUPS * SSD_STATE
    assert n == LANES and 2 * SSD_HEAD_DIM == LANES and (SSD_HEADS // SSD_GROUPS) % 2 == 0
    off = GATE_DT + (SSD_HEADS if rev else 0)
    cidx = functools.partial(_scan_chunk_index, rev=rev, n_ctx=n_ctx, n_all=n_all)
    tok = lambda b, c: (b, cidx(c), 0)
    const = lambda b, c: (0, 0)
    expand = jnp.asarray(np.arange(LANES)[:, None] == off + np.arange(w)[None, :] // SSD_HEAD_DIM, BF16)
    in_specs = [pl.BlockSpec((1, n, w), tok),
                pl.BlockSpec((1, n, bcw), lambda b, c: (b, cidx(c), w // bcw)),
                pl.BlockSpec((1, n, LANES), tok),
                pl.BlockSpec((1, LANES), const),
                pl.BlockSpec((1, LANES), const),
                pl.BlockSpec((LANES, w), const)]
    if rev:
        in_specs.append(pl.BlockSpec((1, n, w), tok))
    else:
        in_specs.append(pl.BlockSpec((1, w), const))
    return pl.pallas_call(
        functools.partial(_ssd_kernel, rev=rev, off=off),
        out_shape=jax.ShapeDtypeStruct((bsz, s, w), BF16),
        grid=(bsz, n_all), in_specs=in_specs,
        out_specs=pl.BlockSpec((1, n, w), tok),
        scratch_shapes=[pltpu.VMEM((SSD_HEADS // 2, SSD_STATE, LANES), F32)],
        compiler_params=_params(2), name="ssd_scan_bwd" if rev else "ssd_scan_fwd",
    )(xbc, xbc, gates, dtb_row, a_row, expand, extra)


def _mlstm_kernel(*refs, rev, d):
    if rev:
        q_ref, k_ref, v_ref, g_ref, ib_ref, fb_ref, acc_ref, o_ref, c_st, n_st, m_st = refs
    else:
        q_ref, k_ref, v_ref, g_ref, ib_ref, fb_ref, o_ref, c_st, n_st, m_st = refs

    @pl.when(pl.program_id(1) == 0)
    def _():
        c_st[...] = jnp.zeros_like(c_st)
        n_st[...] = jnp.zeros_like(n_st)
        m_st[...] = jnp.zeros_like(m_st)

    n = q_ref.shape[1]
    last = 0 if rev else n - 1
    tri = _tri(n, rev)
    g = g_ref[0]
    log_i = g + ib_ref[...]
    log_f = -_softplus(-(g + fb_ref[...]))
    cs = _cumsum_dot(tri, log_f)
    cs_t = cs.T
    li_t = log_i.T
    for h in range(ML_HEADS):
        ci = GATE_I + ML_HEADS * d + h
        cf = GATE_F + ML_HEADS * d + h
        b_col = cs[:, cf:cf + 1]
        off_row = cs_t[cf:cf + 1, :] - li_t[ci:ci + 1, :]
        tot = cs_t[cf:cf + 1, last:last + 1]
        m_prev = m_st[h, 0:1, 0:1]
        pmax = jnp.max(jnp.where(tri, -off_row, -jnp.inf), axis=-1, keepdims=True)
        u_b = jnp.broadcast_to(-jnp.maximum(pmax, m_prev), (n, LANES))
        b_b = jnp.broadcast_to(b_col, (n, LANES))
        w_inter = jnp.exp(m_prev + u_b)
        a_end = tot - off_row
        m_loc = jnp.max(a_end, axis=-1, keepdims=True)
        w_end = jnp.exp(a_end - m_loc)
        qh = q_ref[0, :, h * ML_DK:(h + 1) * ML_DK]
        kf = k_ref[0, :, h * ML_DK:(h + 1) * ML_DK].astype(F32) * (ML_DK ** -0.5)
        kb = kf.astype(BF16)
        vh = v_ref[0, :, h * ML_DV:(h + 1) * ML_DV]
        s_mat = _dot_nt(qh, kb) * jnp.exp(jnp.where(tri, jnp.tile(u_b, (1, n // LANES)) - off_row, -jnp.inf))
        s_hi, s_lo = _split_bf16(s_mat, 2)
        c_prev = c_st[h]
        n_prev = n_st[h]
        ones = jnp.ones((n, LANES), BF16)
        den = _dot(s_hi, ones) + _dot(s_lo, ones) + _dot(qh, n_prev.astype(BF16)) * w_inter
        inv = 1.0 / jnp.maximum(jnp.abs(den), jnp.exp(u_b - b_b))
        reps = ML_DV // LANES
        out = (_dot(s_hi, vh) * jnp.tile(inv, (1, reps))
               + _dot(qh, c_prev.astype(BF16)) * jnp.tile(w_inter * inv, (1, reps)))
        kw_t = (kf.T * w_end).astype(BF16)
        c_chunk = _dot(kw_t, vh)
        n_chunk = _dot(kw_t, ones)
        m_new = jnp.maximum(tot + m_prev, m_loc)
        a_sc = jnp.exp(tot + m_prev - m_new)
        b_sc = jnp.exp(m_loc - m_new)
        c_st[h] = c_prev * a_sc + c_chunk * b_sc
        n_st[h] = n_prev * a_sc + n_chunk * b_sc
        m_st[h] = jnp.broadcast_to(m_new, m_st.shape[1:])
        if rev:
            out = out + acc_ref[0, :, h * ML_DV:(h + 1) * ML_DV].astype(F32)
        o_ref[0, :, h * ML_DV:(h + 1) * ML_DV] = out.astype(o_ref.dtype)


def _mlstm_scan(q, k, v, gates, ib_row, fb_row, acc, *, rev, ctx_len):
    bsz, s, _ = q.shape
    n = ML_CHUNK
    n_all = s // n
    n_ctx = ctx_len // n
    cidx = functools.partial(_scan_chunk_index, rev=rev, n_ctx=n_ctx, n_all=n_all)
    tok = lambda b, c: (b, cidx(c), 0)
    qw, vw = ML_HEADS * ML_DK, ML_HEADS * ML_DV
    in_specs = [pl.BlockSpec((1, n, qw), tok), pl.BlockSpec((1, n, qw), tok),
                pl.BlockSpec((1, n, vw), tok), pl.BlockSpec((1, n, LANES), tok),
                pl.BlockSpec((1, LANES), lambda b, c: (0, 0)),
                pl.BlockSpec((1, LANES), lambda b, c: (0, 0))]
    args = [q, k, v, gates, ib_row, fb_row]
    if rev:
        in_specs.append(pl.BlockSpec((1, n, vw), tok))
        args.append(acc)
    return pl.pallas_call(
        functools.partial(_mlstm_kernel, rev=rev, d=1 if rev else 0),
        out_shape=jax.ShapeDtypeStruct((bsz, s, vw), F32),
        grid=(bsz, n_all), in_specs=in_specs,
        out_specs=pl.BlockSpec((1, n, vw), tok),
        scratch_shapes=[pltpu.VMEM((ML_HEADS, ML_DK, ML_DV), F32),
                        pltpu.VMEM((ML_HEADS, ML_DK, LANES), F32),
                        pltpu.VMEM((ML_HEADS, 8, LANES), F32)],
        compiler_params=_params(2), name="mlstm_scan_bwd" if rev else "mlstm_scan_fwd",
    )(*args)


def _post_kernel(*refs, mode):
    if mode == "mix":
        (x_ref, y_ref, z_ref, hm_ref, og_ref, snw_ref, mnw_ref, wa_ref, wb_ref,
         mod_ref, n2_ref, wr_ref, br_ref, xo_ref, h2_ref, lg_ref) = refs
        y = _rms(y_ref[0].astype(F32) * _silu(z_ref[0].astype(F32)), snw_ref[...])
        og = og_ref[0].astype(F32)
        o = _dot(y.astype(BF16), wa_ref[...])
        parts = []
        for h in range(ML_HEADS):
            sl = slice(h * ML_DV, (h + 1) * ML_DV)
            parts.append(_rms(hm_ref[0, :, sl], mnw_ref[:, sl]) * jax.nn.sigmoid(og[:, sl]))
        o = o + _dot(jnp.concatenate(parts, axis=-1).astype(BF16), wb_ref[...])
    else:
        (x_ref, a_ref, wa_ref, mod_ref, n2_ref, wr_ref, br_ref, xo_ref, h2_ref, lg_ref) = refs
        o = _dot(a_ref[0], wa_ref[...])
    x_new = x_ref[0] + mod_ref[0, 2:3, :] * o
    xo_ref[0] = x_new
    h2 = _rms(x_new, n2_ref[...]) * (1.0 + mod_ref[0, 4:5, :]) + mod_ref[0, 3:4, :]
    h2_ref[0] = h2
    lg_ref[0] = _dot_split(h2, wr_ref) + br_ref[...]


def _post_call(mode, x, acts, rows, mats, mod_l, norm2_w, w_router, b_router, x_tile_off, n_tiles, ctx_tiles, ctx_row):
    bsz, _, d = x.shape
    tm = ROW_TILE
    s_out = n_tiles * tm
    tok = lambda b, i: (b, i, 0)
    const = lambda b, i: (0, 0)
    mod_idx = lambda b, i: (jnp.where(i + x_tile_off < ctx_tiles, ctx_row, b), 0, 0)
    in_specs = [pl.BlockSpec((1, tm, d), lambda b, i: (b, i + x_tile_off, 0))]
    in_specs += [pl.BlockSpec((1, tm, a.shape[2]), tok) for a in acts]
    in_specs += [pl.BlockSpec(r.shape, const) for r in rows]
    in_specs += [pl.BlockSpec(m.shape, const) for m in mats]
    in_specs += [pl.BlockSpec((1, 6, d), mod_idx), pl.BlockSpec((1, d), const),
                 pl.BlockSpec(w_router.shape, lambda b, i: (0, 0, 0)), pl.BlockSpec((1, LANES), const)]
    out_shape = [jax.ShapeDtypeStruct((bsz, s_out, d), F32),
                 jax.ShapeDtypeStruct((bsz, s_out, d), F32),
                 jax.ShapeDtypeStruct((bsz, s_out, LANES), F32)]
    out_specs = [pl.BlockSpec((1, tm, d), tok), pl.BlockSpec((1, tm, d), tok),
                 pl.BlockSpec((1, tm, LANES), tok)]
    return pl.pallas_call(
        functools.partial(_post_kernel, mode=mode),
        out_shape=out_shape, grid=(bsz, n_tiles), in_specs=in_specs, out_specs=out_specs,
        compiler_params=_params(2), name="post_" + mode,
    )(x, *acts, *rows, *mats, mod_l, norm2_w.reshape(1, d), w_router, b_router)


def _router_kernel(lg_ref, route_ref, cnt_ref, src_ref, all_ref, off_ref, inv_ref, *, moe_tile, n_tokens):
    phase = pl.program_id(0)
    i = pl.program_id(1)
    tm = lg_ref.shape[0]
    rows = pl.ds(pl.multiple_of(i * tm, tm), tm)

    @pl.when((phase == 0) & (i == 0))
    def _():
        cnt_ref[...] = jnp.zeros_like(cnt_ref)

    @pl.when(phase == 0)
    def _():
        all_ref[rows, :] = _route_fields(lg_ref[...], cnt_ref)

    @pl.when((phase == 1) & (i == 0))
    def _():
        tiles = jnp.ceil(cnt_ref[...] * (1.0 / moe_tile))
        r = lax.broadcasted_iota(jnp.int32, (LANES, LANES), 0)
        c = lax.broadcasted_iota(jnp.int32, (LANES, LANES), 1)
        earlier = jnp.where(r < c, 1.0, 0.0).astype(BF16)
        off_ref[...] = _dot(tiles.astype(BF16), earlier) * float(moe_tile)

    @pl.when(phase == 1)
    def _():
        f = all_ref[rows, :]
        lane = lax.broadcasted_iota(jnp.int32, f.shape, 1).astype(F32)
        off = off_ref[0:1, :]
        pos1 = jnp.sum(jnp.where(lane == f[:, 0:1], off, 0.0), axis=-1, keepdims=True) + f[:, 4:5]
        pos2 = jnp.sum(jnp.where(lane == f[:, 1:2], off, 0.0), axis=-1, keepdims=True) + f[:, 5:6]
        route_ref[...] = jnp.where(lane == 6.0, pos1, jnp.where(lane == 7.0, pos2, f))

        n_blk = inv_ref.shape[0]
        pos_t = jnp.where(lane == 0.0, pos1, jnp.where(lane == 1.0, pos2, 0.0)).T
        blk = lax.broadcasted_iota(jnp.int32, (n_blk, tm), 0).astype(F32)
        tok = (i * tm + lax.broadcasted_iota(jnp.int32, (tm, 1), 0)).astype(F32)
        tok_hi = jnp.floor(tok * (1.0 / LANES))
        tok_lo = tok - tok_hi * LANES
        lhs, rhs = [], []
        for k, pos in enumerate((pos1, pos2)):
            blk_of = jnp.floor(pos_t[k:k + 1, :] * (1.0 / LANES))
            lhs.append(jnp.where(blk == blk_of, 1.0, 0.0).astype(BF16))
            hit = lane == pos - jnp.floor(pos * (1.0 / LANES)) * LANES
            rhs.append(jnp.concatenate([jnp.where(hit, tok_hi, 0.0), jnp.where(hit, tok_lo, 0.0),
                                        jnp.where(hit, 1.0, 0.0)], axis=1).astype(BF16))
        upd = _dot(jnp.concatenate(lhs, axis=1), jnp.concatenate(rhs, axis=0))

        @pl.when(i == 0)
        def _():
            inv_ref[...] = upd

        @pl.when(i > 0)
        def _():
            inv_ref[...] = inv_ref[...] + upd

    @pl.when((phase == 1) & (i == pl.num_programs(1) - 1))
    def _():
        acc = inv_ref[...]
        n_blk = acc.shape[0]
        slot = (lax.broadcasted_iota(jnp.int32, (n_blk, LANES), 0) * LANES
                + lax.broadcasted_iota(jnp.int32, (n_blk, LANES), 1)).astype(F32)
        spare = slot - n_tokens * jnp.floor((slot + 0.5) * (1.0 / n_tokens))
        src = jnp.where(acc[:, 2 * LANES:] > 0.0, acc[:, :LANES] * LANES + acc[:, LANES:2 * LANES], spare)
        src_ref[...] = src.astype(jnp.int32)


def _route_fields(lg, cnt_ref):
    tm = lg.shape[0]
    lane = lax.broadcasted_iota(jnp.int32, lg.shape, 1).astype(F32)
    big = float(LANES)
    is_g = (lane >= ROUTE_G) & (lane < ROUTE_G + MOE_GROUPS)
    lgg = jnp.where(is_g, lg, -jnp.inf)
    g_max = jnp.max(lgg, axis=-1, keepdims=True)
    g_idx = jnp.min(jnp.where(lgg == g_max, lane - ROUTE_G, big), axis=-1, keepdims=True)
    g_prob = 1.0 / jnp.sum(jnp.exp(lgg - g_max), axis=-1, keepdims=True)
    lo = g_idx * MOE_EXPERTS
    le = jnp.where((lane >= lo) & (lane < lo + MOE_EXPERTS), lg, -jnp.inf)
    l1 = jnp.max(le, axis=-1, keepdims=True)
    i1 = jnp.min(jnp.where(le == l1, lane, big), axis=-1, keepdims=True)
    le2 = jnp.where(lane == i1, -jnp.inf, le)
    l2 = jnp.max(le2, axis=-1, keepdims=True)
    i2 = jnp.min(jnp.where(le2 == l2, lane, big), axis=-1, keepdims=True)
    r = jnp.exp(l2 - l1)
    w1 = g_prob / (1.0 + r)
    w2 = w1 * r
    oh1 = jnp.where(lane == i1, 1.0, 0.0)
    oh2 = jnp.where(lane == i2, 1.0, 0.0)
    oh = oh1 + oh2
    row = lax.broadcasted_iota(jnp.int32, (tm, tm), 0)
    col = lax.broadcasted_iota(jnp.int32, (tm, tm), 1)
    before = jnp.where(col < row, 1.0, 0.0).astype(BF16)
    prefix = _dot(before, oh.astype(BF16)) + cnt_ref[0:1, :]
    rank1 = jnp.sum(prefix * oh1, axis=-1, keepdims=True)
    rank2 = jnp.sum(prefix * oh2, axis=-1, keepdims=True)
    cnt_ref[...] = cnt_ref[...] + jnp.sum(oh, axis=0, keepdims=True)
    fields = (i1, i2, w1, w2, rank1, rank2)
    out = jnp.zeros_like(lg)
    for j, f in enumerate(fields):
        out = jnp.where(lane == float(j), f, out)
    return out


def _router(logits, n_sorted):
    t = logits.shape[0]
    tm = ROW_TILE
    n_blk = n_sorted // LANES
    fixed = lambda p, i: (0, 0)
    return pl.pallas_call(
        functools.partial(_router_kernel, moe_tile=MOE_TILE, n_tokens=t),
        out_shape=[jax.ShapeDtypeStruct((t, LANES), F32), jax.ShapeDtypeStruct((8, LANES), F32),
                   jax.ShapeDtypeStruct((n_blk, LANES), jnp.int32)],
        grid=(2, t // tm),
        in_specs=[pl.BlockSpec((tm, LANES), lambda p, i: (i * (1 - p), 0))],
        out_specs=[pl.BlockSpec((tm, LANES), lambda p, i: (i * p, 0)), pl.BlockSpec((8, LANES), fixed),
                   pl.BlockSpec((n_blk, LANES), fixed)],
        scratch_shapes=[pltpu.VMEM((t, LANES), F32), pltpu.VMEM((8, LANES), F32),
                        pltpu.VMEM((n_blk, 3 * LANES), F32)],
        compiler_params=_params(2), name="router",
    )(logits)


def _moe_kernel(te_ref, nt_ref, x_ref, wg_ref, wu_ref, wd_ref, o_ref, wgb, wub, wdb):
    i = pl.program_id(0)
    valid = i < nt_ref[0]
    fresh = (i == 0) | (te_ref[i] != te_ref[jnp.maximum(i - 1, 0)])

    @pl.when(valid & fresh)
    def _():
        wgb[...] = wg_ref[0].astype(BF16)
        wub[...] = wu_ref[0].astype(BF16)
        wdb[...] = wd_ref[0].astype(BF16)

    @pl.when(valid)
    def _():
        x = x_ref[...].astype(BF16)
        act = _silu(_dot(x, wgb[...])) * _dot(x, wub[...])
        o_ref[...] = _dot(act.astype(BF16), wdb[...]).astype(o_ref.dtype)

    @pl.when(jnp.logical_not(valid))
    def _():
        o_ref[...] = jnp.zeros_like(o_ref)


def _moe_experts(x_sorted, tile_expert, n_tiles_used, wg, wu, wd):
    tm = MOE_TILE
    rows, d = x_sorted.shape
    ff = wg.shape[2]
    grid_spec = pltpu.PrefetchScalarGridSpec(
        num_scalar_prefetch=2, grid=(rows // tm,),
        in_specs=[pl.BlockSpec((tm, d), lambda i, te, nt: (i, 0)),
                  pl.BlockSpec((1, d, ff), lambda i, te, nt: (te[i], 0, 0)),
                  pl.BlockSpec((1, d, ff), lambda i, te, nt: (te[i], 0, 0)),
                  pl.BlockSpec((1, ff, d), lambda i, te, nt: (te[i], 0, 0))],
        out_specs=pl.BlockSpec((tm, d), lambda i, te, nt: (i, 0)),
        scratch_shapes=[pltpu.VMEM((d, ff), BF16), pltpu.VMEM((d, ff), BF16), pltpu.VMEM((ff, d), BF16)])
    return pl.pallas_call(
        _moe_kernel, out_shape=jax.ShapeDtypeStruct((rows, d), F32), grid_spec=grid_spec,
        compiler_params=_params(1), name="moe_experts",
    )(tile_expert, n_tiles_used, x_sorted, wg, wu, wd)


def _combine_kernel(x_ref, y1_ref, y2_ref, rt_ref, mod_ref, o_ref):
    f = rt_ref[0, :, 2:3] * y1_ref[0] + rt_ref[0, :, 3:4] * y2_ref[0]
    o_ref[0] = x_ref[0] + mod_ref[0, 5:6, :] * f


def _combine(x, y1, y2, route, mod_l, ctx_tiles, ctx_row):
    bsz, s, d = x.shape
    tm = ROW_TILE
    tok = lambda b, i: (b, i, 0)
    mod_idx = lambda b, i: (jnp.where(i < ctx_tiles, ctx_row, b), 0, 0)
    return pl.pallas_call(
        _combine_kernel, out_shape=jax.ShapeDtypeStruct(x.shape, F32), grid=(bsz, s // tm),
        in_specs=[pl.BlockSpec((1, tm, d), tok), pl.BlockSpec((1, tm, d), tok), pl.BlockSpec((1, tm, d), tok),
                  pl.BlockSpec((1, tm, LANES), tok), pl.BlockSpec((1, 6, d), mod_idx)],
        out_specs=pl.BlockSpec((1, tm, d), tok),
        compiler_params=_params(2), name="moe_combine",
    )(x, y1.reshape(x.shape), y2.reshape(x.shape), route.reshape(bsz, s, LANES), mod_l)


def _hier_moe(h2, logits, layer, wg, wu, wd):
    bsz, s, d = h2.shape
    t = bsz * s
    tm = MOE_TILE
    n_tiles = 2 * t // tm + N_EXPERTS
    route, counts, src = _router(logits.reshape(t, LANES), n_tiles * tm)
    tiles_per = (counts[0, :N_EXPERTS].astype(jnp.int32) + tm - 1) // tm
    tile_end = jnp.cumsum(tiles_per)
    tile_ids = jnp.arange(n_tiles, dtype=jnp.int32)
    tile_expert = jnp.minimum(jnp.sum((tile_end[None, :] <= tile_ids[:, None]).astype(jnp.int32), axis=1),
                              N_EXPERTS - 1) + layer * N_EXPERTS
    pos = route[:, 6:8].astype(jnp.int32)
    x_sorted = jnp.take(h2.reshape(t, d), src.reshape(-1), axis=0, mode="clip")
    y_sorted = _moe_experts(x_sorted, tile_expert, tile_end[-1:], wg.reshape(-1, d, wg.shape[-1]),
                            wu.reshape(-1, d, wu.shape[-1]), wd.reshape(-1, wd.shape[-2], d))
    y1 = jnp.take(y_sorted, pos[:, 0], axis=0, mode="clip")
    y2 = jnp.take(y_sorted, pos[:, 1], axis=0, mode="clip")
    return y1, y2, route


def _attn_kernel(q_ref, k_ref, v_ref, bias_ref, qw_ref, kw_ref, o_ref, qn_ref, kn_ref, *, ctx_len, rows):
    lane = lax.broadcasted_iota(jnp.int32, (1, LANES), 1)
    first = lane < NA_HEAD_DIM

    r_head = lax.broadcasted_iota(jnp.int32, (LANES, LANES), 0) // NA_HEAD_DIM
    c_head = lax.broadcasted_iota(jnp.int32, (LANES, LANES), 1) // NA_HEAD_DIM
    same_head = jnp.where(r_head == c_head, 1.0, 0.0).astype(BF16)

    def head_norm(x, w):
        ms = _dot((x * x).astype(BF16), same_head) * (1.0 / NA_HEAD_DIM)
        return x * lax.rsqrt(ms + RMS_EPS) * w

    kn_ref[...] = head_norm(k_ref[0].astype(F32), kw_ref[...]).astype(BF16)
    qn_ref[...] = (head_norm(q_ref[0, ctx_len:, :].astype(F32), qw_ref[...]) * NA_HEAD_DIM ** -0.5).astype(BF16)
    n_groups = rows // ATTN_GROUP_ROWS
    n_q = ATTN_GROUP_ROWS * GRID_W
    n_loc = ATTN_KEY_ROWS * GRID_W
    k_ctx = kn_ref[0:ctx_len, :]
    v_ctx = v_ref[0, 0:ctx_len, :]

    def one_group(g):
        kind = jnp.where(g == 0, 0, jnp.where(g == n_groups - 1, 2, 1))
        kr0 = jnp.clip(g * ATTN_GROUP_ROWS - NA_KH // 2, 0, rows - ATTN_KEY_ROWS)
        q_rows = pl.ds(pl.multiple_of(g * n_q, n_q), n_q)
        q = qn_ref[q_rows, :]
        zero = jnp.zeros_like(q)
        q2 = jnp.concatenate([jnp.where(first, q, zero), jnp.where(first, zero, q)], axis=0)
        k_off = pl.multiple_of(ctx_len + kr0 * GRID_W, GRID_W)
        s_loc = _dot_nt(q2, kn_ref[pl.ds(k_off, n_loc), :])
        s_ctx = _dot_nt(q2, k_ctx)
        p_loc, p_ctx, inv = [], [], []
        for hh in range(2):
            sl = s_loc[hh * n_q:(hh + 1) * n_q] + bias_ref[hh, kind]
            sc = s_ctx[hh * n_q:(hh + 1) * n_q]
            m = jnp.maximum(jnp.max(sl, axis=-1, keepdims=True), jnp.max(sc, axis=-1, keepdims=True))
            el = jnp.exp(sl - m)
            ec = jnp.exp(sc - m)
            inv.append(1.0 / (jnp.sum(el, axis=-1, keepdims=True) + jnp.sum(ec, axis=-1, keepdims=True)))
            p_loc.append(el.astype(BF16))
            p_ctx.append(ec.astype(BF16))
        o = (_dot(jnp.concatenate(p_loc, axis=0), v_ref[0, pl.ds(k_off, n_loc), :])
             + _dot(jnp.concatenate(p_ctx, axis=0), v_ctx))
        o_ref[0, q_rows, :] = jnp.where(first, o[:n_q] * inv[0], o[n_q:] * inv[1]).astype(o_ref.dtype)

    def body(i, carry):
        for j in range(ATTN_GROUPS_PER_TRIP):
            one_group(i * ATTN_GROUPS_PER_TRIP + j)
        return carry

    lax.fori_loop(0, n_groups // ATTN_GROUPS_PER_TRIP, body, 0)


def _attn_group_layout(rows):
    n_groups = rows // ATTN_GROUP_ROWS
    assert rows % ATTN_GROUP_ROWS == 0 and rows >= ATTN_KEY_ROWS and n_groups >= 2
    u = np.arange(ATTN_GROUP_ROWS)[:, None]
    i = np.arange(ATTN_KEY_ROWS)[None, :]

    def layout(g):
        r = g * ATTN_GROUP_ROWS + u
        r0 = np.clip(r - NA_KH // 2, 0, rows - NA_KH)
        kr = np.clip(g * ATTN_GROUP_ROWS - NA_KH // 2, 0, rows - ATTN_KEY_ROWS) + i
        return (kr >= r0) & (kr < r0 + NA_KH), kr - r + NA_KH - 1

    kinds = [layout(0), layout(1), layout(n_groups - 1)]
    for g in range(1, n_groups - 1):
        valid, d = layout(g)
        assert (valid == kinds[1][0]).all() and (d[valid] == kinds[1][1][valid]).all()
    return np.stack([k[0] for k in kinds]), np.stack([k[1] for k in kinds])


def _bias_windows(rpb, rows):
    qc = np.arange(GRID_W)
    c0 = np.clip(qc - NA_KW // 2, 0, GRID_W - NA_KW)
    kc = np.arange(GRID_W)
    inwin = (kc[None, :] >= c0[:, None]) & (kc[None, :] < c0[:, None] + NA_KW)
    coff = kc[None, :] - qc[:, None] + NA_KW - 1
    pick = (coff[..., None] == np.arange(2 * NA_KW - 1)) & inwin[..., None]
    tab = jnp.einsum("hdo,qko->dhqk", rpb.astype(F32), jnp.asarray(pick, F32), precision=HIGHEST)
    tab = jnp.where(inwin[None, None], tab, NEG)
    valid, d = _attn_group_layout(rows)
    masked = jnp.full(tab.shape[1:], NEG, F32)
    blocks = []
    for kind in range(valid.shape[0]):
        for u in range(ATTN_GROUP_ROWS):
            blocks.append(jnp.concatenate(
                [tab[int(d[kind, u, i])] if valid[kind, u, i] else masked for i in range(ATTN_KEY_ROWS)], axis=-1))
    win = jnp.stack(blocks, axis=1)
    return win.reshape(rpb.shape[0], valid.shape[0], ATTN_GROUP_ROWS * GRID_W, ATTN_KEY_ROWS * GRID_W)


def _neighbourhood_attention(q, k, v, bias, qn_w, kn_w, ctx_len):
    bsz, s, w = q.shape
    seq = s - ctx_len
    rows = seq // GRID_W
    n_pairs = NA_HEADS // 2
    pair = lambda b, p: (b, 0, p)
    row2 = lambda b, p: (0, 0)
    return pl.pallas_call(
        functools.partial(_attn_kernel, ctx_len=ctx_len, rows=rows),
        out_shape=jax.ShapeDtypeStruct((bsz, seq, w), BF16),
        grid=(bsz, n_pairs),
        in_specs=[pl.BlockSpec((1, s, LANES), pair), pl.BlockSpec((1, s, LANES), pair),
                  pl.BlockSpec((1, s, LANES), pair),
                  pl.BlockSpec((2,) + bias.shape[1:], lambda b, p: (p, 0, 0, 0)),
                  pl.BlockSpec((1, LANES), row2), pl.BlockSpec((1, LANES), row2)],
        out_specs=pl.BlockSpec((1, seq, LANES), pair),
        scratch_shapes=[pltpu.VMEM((seq, LANES), BF16), pltpu.VMEM((s, LANES), BF16)],
        compiler_params=_params(2), name="neighbourhood_attention",
    )(q, k, v, bias, jnp.tile(qn_w, 2).reshape(1, LANES), jnp.tile(kn_w, 2).reshape(1, LANES))


def _pad_row(pieces, width=LANES):
    row = jnp.zeros((width,), F32)
    for off, vec in pieces:
        row = row.at[off:off + vec.shape[0]].set(vec.astype(F32))
    return row.reshape(1, width)


def _router_params(w_group, b_group, w_expert, b_expert):
    d = w_group.shape[0]
    w = jnp.zeros((d, LANES), F32).at[:, :N_EXPERTS].set(w_expert).at[:, ROUTE_G:ROUTE_G + MOE_GROUPS].set(w_group)
    return _split_weight(w), _pad_row([(0, b_expert), (ROUTE_G, b_group)])


def kernel(x, c, ctx, c_ctx, norm1_w, norm2_w, mod_w, mod_b, ab_w_in, ab_conv_w, ab_conv_b, ssd_a_log, ssd_dt_bias, ssd_d, ssd_norm_w, ml_i_bias, ml_f_bias, ml_norm_w, ab_w_out, na_w_qkv, na_q_norm, na_k_norm, na_rpb, na_w_out, moe_w_group, moe_b_group, moe_w_expert, moe_b_expert, moe_w_gate, moe_w_up, moe_w_down):
    bsz, seq, d = x.shape
    ctx_len = ctx.shape[1]
    depth = mod_w.shape[0]
    assert depth == 2 and ctx_len % ROW_TILE == 0 and seq % ROW_TILE == 0 and bsz < 8
    ctx_tiles = ctx_len // ROW_TILE
    lat_tiles = seq // ROW_TILE
    ctx_row = bsz

    cvec = jnp.zeros((8, d), F32).at[:bsz].set(c).at[bsz].set(c_ctx)
    mod = _mod_vectors(cvec, mod_w, mod_b)
    xs = jnp.concatenate([ctx, x], axis=1)

    ssd_w = SSD_HEADS * SSD_HEAD_DIM
    xbc_w = ssd_w + 2 * SSD_GROUPS * SSD_STATE
    qk_w, v_w = ML_HEADS * ML_DK, ML_HEADS * ML_DV
    sizes = (ssd_w, xbc_w, 2 * SSD_HEADS, qk_w, qk_w, v_w, v_w, 2 * ML_HEADS, 2 * ML_HEADS)
    w_z, w_xbc, w_dt, w_q, w_k, w_v, w_o, w_i, w_f = jnp.split(ab_w_in[0], np.cumsum(sizes)[:-1].tolist(), axis=1)
    w_gate = jnp.zeros((d, LANES), F32).at[:, :GATE_F + 2 * ML_HEADS].set(jnp.concatenate([w_dt, w_i, w_f], axis=1))
    weights = [w.astype(BF16) for w in (w_z, w_xbc, w_q, w_k, w_v, w_o)] + [_split_weight(w_gate)]
    z, xbc, q, k, v, og, gates = _norm_mod_matmul(xs, norm1_w[0], mod[0], weights, [BF16] * 6 + [F32],
                                                  ctx_tiles, ctx_row)
    xbc = _conv_silu(xbc, ab_conv_w[0], ab_conv_b[0], ctx_len)
    n_ctx = ctx_len // SCAN_CHUNK
    a_neg = -jnp.exp(ssd_a_log[0].astype(F32))
    dsk_row = jnp.repeat(ssd_d[0].astype(F32), SSD_HEAD_DIM).reshape(1, ssd_w)
    y = None
    hm = None
    for dr in range(2):
        rev = dr == 1
        dtb_row = _pad_row([(GATE_DT + dr * SSD_HEADS, ssd_dt_bias[0, dr])])
        a_row = _pad_row([(GATE_DT + dr * SSD_HEADS, a_neg[dr])])
        y = _ssd_scan(xbc, gates, dtb_row, a_row, y if rev else dsk_row, rev=rev, n_ctx=n_ctx)
        ib_row = _pad_row([(GATE_I + dr * ML_HEADS, ml_i_bias[0, dr])])
        fb_row = _pad_row([(GATE_F + dr * ML_HEADS, ml_f_bias[0, dr])])
        hm = _mlstm_scan(q, k, v, gates, ib_row, fb_row, hm, rev=rev, ctx_len=ctx_len)
    w_r, b_r = _router_params(moe_w_group[0], moe_b_group[0], moe_w_expert[0], moe_b_expert[0])
    w_out = ab_w_out[0].astype(BF16)
    x1, h2, logits = _post_call(
        "mix", xs, [y, z, hm, og],
        [ssd_norm_w[0].reshape(1, ssd_w), ml_norm_w[0].reshape(1, v_w)], [w_out[:ssd_w], w_out[ssd_w:]],
        mod[0], norm2_w[0], w_r, b_r, 0, ctx_tiles + lat_tiles, ctx_tiles, ctx_row)
    moe0 = _hier_moe(h2, logits, 0, moe_w_gate, moe_w_up, moe_w_down)

    w_qkv = na_w_qkv[0].astype(BF16)
    na_w = NA_HEADS * NA_HEAD_DIM
    q, k, v, xs = _norm_mod_matmul(x1, norm1_w[1], mod[1],
                                   [w_qkv[:, :na_w], w_qkv[:, na_w:2 * na_w], w_qkv[:, 2 * na_w:]],
                                   [BF16] * 3, ctx_tiles, ctx_row, pending_moe=(*moe0, mod[0]))
    bias = _bias_windows(na_rpb[0], seq // GRID_W)
    attn = _neighbourhood_attention(q, k, v, bias, na_q_norm[0], na_k_norm[0], ctx_len)
    w_r, b_r = _router_params(moe_w_group[1], moe_b_group[1], moe_w_expert[1], moe_b_expert[1])
    x1, h2, logits = _post_call("attn", xs, [attn], [], [na_w_out[0].astype(BF16)],
                                mod[1], norm2_w[1], w_r, b_r, ctx_tiles, lat_tiles, ctx_tiles, ctx_row)
    y1, y2, route = _hier_moe(h2, logits, 1, moe_w_gate, moe_w_up, moe_w_down)
    return _combine(x1, y1, y2, route, mod[1], 0, ctx_row)
```

```python
import functools

import numpy as np
import jax
import jax.numpy as jnp
from jax import lax
from jax.experimental import pallas as pl
from jax.experimental.pallas import tpu as pltpu

F32 = jnp.float32
BF16 = jnp.bfloat16
HIGHEST = lax.Precision.HIGHEST

RMS_EPS = 1e-6
GRID_W = 64
SSD_HEADS = 16
SSD_HEAD_DIM = 64
SSD_GROUPS = 2
SSD_STATE = 128
ML_HEADS = 4
ML_DK = 128
ML_DV = 256
NA_HEADS = 16
NA_HEAD_DIM = 64
NA_KH = 8
NA_KW = 16
MOE_GROUPS = 4
MOE_EXPERTS = 8
N_EXPERTS = MOE_GROUPS * MOE_EXPERTS

LANES = 128
ROW_TILE = 256
SCAN_CHUNK = 128
ML_CHUNK = 256
MOE_TILE = 256
MOE_PARTS = 4
ATTN_GROUP_ROWS = 4
ATTN_KEY_ROWS = ATTN_GROUP_ROWS + NA_KH - 1
ATTN_GROUPS_PER_TRIP = 4
VMEM_LIMIT = 56 * 1024 * 1024

GATE_DT = 0
GATE_I = 2 * SSD_HEADS
GATE_F = GATE_I + 2 * ML_HEADS
ROUTE_G = N_EXPERTS
NEG = -1e30


def _params(n_axes):
    return pltpu.CompilerParams(dimension_semantics=("arbitrary",) * n_axes,
                                vmem_limit_bytes=VMEM_LIMIT)


def _silu(x):
    return x * jax.nn.sigmoid(x)


def _softplus(x):
    return jnp.maximum(x, 0.0) + jnp.log1p(jnp.exp(-jnp.abs(x)))


def _rms(x, w):
    return x * lax.rsqrt(jnp.mean(x * x, axis=-1, keepdims=True) + RMS_EPS) * w


def _dot(a, b):
    return jnp.dot(a, b, preferred_element_type=F32)


def _dot_nt(a, b):
    return lax.dot_general(a, b, (((1,), (1,)), ((), ())), preferred_element_type=F32)


def _dot_hi(a, b):
    return jnp.dot(a, b, precision=HIGHEST, preferred_element_type=F32)


def _split_bf16(x, terms):
    parts = []
    for _ in range(terms - 1):
        p = x.astype(BF16)
        parts.append(p)
        x = x - p.astype(F32)
    parts.append(x.astype(BF16))
    return parts


def _split_weight(w):
    return jnp.stack(_split_bf16(w.astype(F32), 2))


def _dot_split(a, w2_ref):
    a_hi, a_lo = _split_bf16(a, 2)
    return _dot(a_hi, w2_ref[0]) + _dot(a_lo, w2_ref[0]) + _dot(a_hi, w2_ref[1])


def _cumsum_dot(tri, x):
    tri = jnp.where(tri, 1.0, 0.0).astype(BF16)
    hi, mid, lo = _split_bf16(x, 3)
    return _dot(tri, hi) + _dot(tri, mid) + _dot(tri, lo)


def _mod_kernel(c_ref, w_ref, b_ref, o_ref):
    o_ref[0] = _dot_hi(_silu(c_ref[...]), w_ref[0]) + b_ref[0]


def _mod_vectors(cvec, mod_w, mod_b, tn=512):
    depth, d, n = mod_w.shape
    rows = cvec.shape[0]
    out = pl.pallas_call(
        _mod_kernel,
        out_shape=jax.ShapeDtypeStruct((depth, rows, n), F32),
        grid=(depth, n // tn),
        in_specs=[pl.BlockSpec((rows, d), lambda l, j: (0, 0)),
                  pl.BlockSpec((1, d, tn), lambda l, j: (l, 0, j)),
                  pl.BlockSpec((1, 1, tn), lambda l, j: (l, 0, j))],
        out_specs=pl.BlockSpec((1, rows, tn), lambda l, j: (l, 0, j)),
        compiler_params=_params(2),
        name="mod_vectors",
    )(cvec, mod_w, mod_b.reshape(depth, 1, n))
    return out.reshape(depth, rows, 6, d)


def _stream_tile(refs, split_tiles, offset=0):
    if not split_tiles:
        return refs[0][0], refs[1:]
    return jnp.where(pl.program_id(1) + offset < split_tiles, refs[0][0], refs[1][0]), refs[2:]


def _stream_specs(xs, tm, offset=0):
    if not isinstance(xs, tuple):
        return [xs], [pl.BlockSpec((1, tm, xs.shape[2]), lambda b, i: (b, i + offset, 0))], 0
    ctx, lat = xs
    split = ctx.shape[1] // tm
    d = ctx.shape[2]
    return ([ctx, lat],
            [pl.BlockSpec((1, tm, d), lambda b, i: (b, jnp.minimum(i + offset, split - 1), 0)),
             pl.BlockSpec((1, tm, d), lambda b, i: (b, jnp.maximum(i + offset - split, 0), 0))], split)


def _nmm_kernel(*refs, n_out, pending_moe, split_tiles):
    x, refs = _stream_tile(refs, split_tiles)
    if pending_moe:
        y1_ref, y2_ref, rt_ref, pmod_ref = refs[:4]
        refs = refs[4:]
        x = x + pmod_ref[0, 5:6, :] * (rt_ref[0, :, 2:3] * y1_ref[0] + rt_ref[0, :, 3:4] * y2_ref[0])
        refs[-1][0] = x
        refs = refs[:-1]
    nw_ref, mod_ref = refs[:2]
    refs = refs[2:]
    w_refs, o_refs = refs[:n_out], refs[n_out:]
    h = _rms(x, nw_ref[...])
    h = h * (1.0 + mod_ref[0, 1:2, :]) + mod_ref[0, 0:1, :]
    hb = h.astype(BF16)
    for w_ref, o_ref in zip(w_refs, o_refs):
        if len(w_ref.shape) == 3:
            o_ref[0] = _dot_split(h, w_ref)
        else:
            o_ref[0] = _dot(hb, w_ref[...]).astype(o_ref.dtype)


def _norm_mod_matmul(xs, norm_w, mod_l, weights, out_dtypes, ctx_tiles, ctx_row, pending_moe=None):
    tm = ROW_TILE
    args, in_specs, split_tiles = _stream_specs(xs, tm)
    bsz, d = args[0].shape[0], args[0].shape[2]
    s = sum(a.shape[1] for a in args)
    tok = lambda b, i: (b, i, 0)
    mod_idx = lambda b, i: (jnp.where(i < ctx_tiles, ctx_row, b), 0, 0)
    if pending_moe is not None:
        y1, y2, route, mod_prev = pending_moe
        args += [y1.reshape(bsz, s, d), y2.reshape(bsz, s, d), route.reshape(bsz, s, LANES), mod_prev]
        in_specs += [pl.BlockSpec((1, tm, d), tok), pl.BlockSpec((1, tm, d), tok),
                     pl.BlockSpec((1, tm, LANES), tok), pl.BlockSpec((1, 6, d), mod_idx)]
    args += [norm_w.reshape(1, d), mod_l, *weights]
    in_specs += [pl.BlockSpec((1, d), lambda b, i: (0, 0)), pl.BlockSpec((1, 6, d), mod_idx)]
    in_specs += [pl.BlockSpec(w.shape, lambda b, i, nd=w.ndim: (0,) * nd) for w in weights]
    out_shape = [jax.ShapeDtypeStruct((bsz, s, w.shape[-1]), dt) for w, dt in zip(weights, out_dtypes)]
    out_specs = [pl.BlockSpec((1, tm, w.shape[-1]), tok) for w in weights]
    if pending_moe is not None:
        out_shape.append(jax.ShapeDtypeStruct((bsz, s, d), F32))
        out_specs.append(pl.BlockSpec((1, tm, d), tok))
    return pl.pallas_call(
        functools.partial(_nmm_kernel, n_out=len(weights), pending_moe=pending_moe is not None,
                          split_tiles=split_tiles),
        out_shape=out_shape, grid=(bsz, s // tm), in_specs=in_specs, out_specs=out_specs,
        compiler_params=_params(2), name="norm_mod_matmul",
    )(*args)


def _conv_kernel(u_ref, w_ref, b_ref, o_ref, *, ctx_len):
    u = u_ref[0].astype(F32)
    s = u.shape[0]
    t = lax.broadcasted_iota(jnp.int32, u.shape, 0)
    prev = jnp.where((t == 0) | (t == ctx_len), 0.0, pltpu.roll(u, 1, axis=0))
    nxt = jnp.where((t == ctx_len - 1) | (t == s - 1), 0.0, pltpu.roll(u, s - 1, axis=0))
    y = prev * w_ref[0:1, :] + u * w_ref[1:2, :] + nxt * w_ref[2:3, :] + b_ref[...]
    o_ref[0] = _silu(y).astype(o_ref.dtype)


def _conv_silu(u, conv_w, conv_b, ctx_len, tc=LANES):
    bsz, s, ch = u.shape
    return pl.pallas_call(
        functools.partial(_conv_kernel, ctx_len=ctx_len),
        out_shape=jax.ShapeDtypeStruct(u.shape, u.dtype),
        grid=(bsz, ch // tc),
        in_specs=[pl.BlockSpec((1, s, tc), lambda b, j: (b, 0, j)),
                  pl.BlockSpec((3, tc), lambda b, j: (0, j)),
                  pl.BlockSpec((1, tc), lambda b, j: (0, j))],
        out_specs=pl.BlockSpec((1, s, tc), lambda b, j: (b, 0, j)),
        compiler_params=_params(2), name="conv_silu",
    )(u, conv_w, conv_b.reshape(1, ch))


def _scan_chunk_index(c, rev, n_ctx, n_all):
    if not rev:
        return c
    return jnp.where(c < n_ctx, n_ctx - 1 - c, n_ctx + n_all - 1 - c)


def _tri(n, rev):
    row = lax.broadcasted_iota(jnp.int32, (n, n), 0)
    col = lax.broadcasted_iota(jnp.int32, (n, n), 1)
    return (col >= row) if rev else (col <= row)


def _ssd_kernel(*refs, rev, off):
    if rev:
        xs_ref, bc_ref, g_ref, dtb_ref, a_ref, ex_ref, acc_ref, o_ref, st_ref = refs
    else:
        xs_ref, bc_ref, g_ref, dtb_ref, a_ref, ex_ref, dsk_ref, o_ref, st_ref = refs

    @pl.when(pl.program_id(1) == 0)
    def _():
        st_ref[...] = jnp.zeros_like(st_ref)

    n = xs_ref.shape[1]
    last = 0 if rev else n - 1
    tri = _tri(n, rev)
    dt = _softplus(g_ref[0] + dtb_ref[...])
    log_a = dt * a_ref[...]
    cs = _cumsum_dot(tri, log_a)
    cs_t = cs.T
    dt_hi = dt.astype(BF16)
    dt_lo = (dt - dt_hi.astype(F32)).astype(BF16)
    dt_full = _dot(dt_hi, ex_ref[...]) + _dot(dt_lo, ex_ref[...])
    xs = xs_ref[0].astype(F32)
    xdt = (xs * dt_full).astype(BF16)
    lo_half = lax.broadcasted_iota(jnp.int32, (1, LANES), 1) < SSD_HEAD_DIM
    gw = SSD_GROUPS * SSD_STATE
    heads_per_group = SSD_HEADS // SSD_GROUPS
    for g in range(SSD_GROUPS):
        b_g = bc_ref[0, :, g * SSD_STATE:(g + 1) * SSD_STATE]
        c_g = bc_ref[0, :, gw + g * SSD_STATE:gw + (g + 1) * SSD_STATE]
        cb = _dot_nt(c_g, b_g)
        b_t = b_g.astype(F32).T
        for e in range(0, heads_per_group, 2):
            h0 = g * heads_per_group + e
            pair = h0 // 2
            sl = slice(pair * LANES, (pair + 1) * LANES)
            x_pair = xdt[:, sl]
            zero = jnp.zeros_like(x_pair)
            y, upd, a_bc, tots = None, None, [], []
            for j in range(2):
                col = off + h0 + j
                x_j = jnp.where(lo_half, x_pair, zero) if j == 0 else jnp.where(lo_half, zero, x_pair)
                a_b = jnp.broadcast_to(cs[:, col:col + 1], (n, LANES))
                a_row = cs_t[col:col + 1, :]
                tot = cs_t[col:col + 1, last:last + 1]
                decay = jnp.exp(jnp.where(tri, a_b - a_row, -jnp.inf))
                y_j = _dot((cb * decay).astype(BF16), x_j)
                upd_j = _dot((b_t * jnp.exp(tot - a_row)).astype(BF16), x_j)
                y = y_j if y is None else y + y_j
                upd = upd_j if upd is None else upd + upd_j
                a_bc.append(a_b)
                tots.append(tot)
            state = st_ref[pair]
            y = y + _dot(c_g, state.astype(BF16)) * jnp.exp(jnp.where(lo_half, a_bc[0], a_bc[1]))
            st_ref[pair] = state * jnp.exp(jnp.where(lo_half, tots[0], tots[1])) + upd
            if rev:
                y = y + acc_ref[0, :, sl].astype(F32)
            else:
                y = y + dsk_ref[:, sl] * xs[:, sl]
            o_ref[0, :, sl] = y.astype(o_ref.dtype)


def _ssd_scan(xbc, gates, dtb_row, a_row, extra, *, rev, n_ctx):
    bsz, s, _ = xbc.shape
    n = SCAN_CHUNK
    n_all = s // n
    w = SSD_HEADS * SSD_HEAD_DIM
    bcw = 2 * SSD_GROUPS * SSD_STATE
    assert n == LANES and 2 * SSD_HEAD_DIM == LANES and (SSD_HEADS // SSD_GROUPS) % 2 == 0
    off = GATE_DT + (SSD_HEADS if rev else 0)
    cidx = functools.partial(_scan_chunk_index, rev=rev, n_ctx=n_ctx, n_all=n_all)
    tok = lambda b, c: (b, cidx(c), 0)
    const = lambda b, c: (0, 0)
    expand = jnp.asarray(np.arange(LANES)[:, None] == off + np.arange(w)[None, :] // SSD_HEAD_DIM, BF16)
    in_specs = [pl.BlockSpec((1, n, w), tok),
                pl.BlockSpec((1, n, bcw), lambda b, c: (b, cidx(c), w // bcw)),
                pl.BlockSpec((1, n, LANES), tok),
                pl.BlockSpec((1, LANES), const),
                pl.BlockSpec((1, LANES), const),
                pl.BlockSpec((LANES, w), const)]
    if rev:
        in_specs.append(pl.BlockSpec((1, n, w), tok))
    else:
        in_specs.append(pl.BlockSpec((1, w), const))
    return pl.pallas_call(
        functools.partial(_ssd_kernel, rev=rev, off=off),
        out_shape=jax.ShapeDtypeStruct((bsz, s, w), BF16),
        grid=(bsz, n_all), in_specs=in_specs,
        out_specs=pl.BlockSpec((1, n, w), tok),
        scratch_shapes=[pltpu.VMEM((SSD_HEADS // 2, SSD_STATE, LANES), F32)],
        compiler_params=_params(2), name="ssd_scan_bwd" if rev else "ssd_scan_fwd",
    )(xbc, xbc, gates, dtb_row, a_row, expand, extra)


def _mlstm_kernel(*refs, rev, d):
    if rev:
        q_ref, k_ref, v_ref, g_ref, ib_ref, fb_ref, acc_ref, o_ref, c_st, n_st, m_st = refs
    else:
        q_ref, k_ref, v_ref, g_ref, ib_ref, fb_ref, o_ref, c_st, n_st, m_st = refs

    @pl.when(pl.program_id(1) == 0)
    def _():
        c_st[...] = jnp.zeros_like(c_st)
        n_st[...] = jnp.zeros_like(n_st)
        m_st[...] = jnp.zeros_like(m_st)

    n = q_ref.shape[1]
    last = 0 if rev else n - 1
    tri = _tri(n, rev)
    g = g_ref[0]
    log_i = g + ib_ref[...]
    log_f = -_softplus(-(g + fb_ref[...]))
    cs = _cumsum_dot(tri, log_f)
    cs_t = cs.T
    li_t = log_i.T
    for h in range(ML_HEADS):
        ci = GATE_I + ML_HEADS * d + h
        cf = GATE_F + ML_HEADS * d + h
        b_col = cs[:, cf:cf + 1]
        off_row = cs_t[cf:cf + 1, :] - li_t[ci:ci + 1, :]
        tot = cs_t[cf:cf + 1, last:last + 1]
        m_prev = m_st[h, 0:1, 0:1]
        pmax = jnp.max(jnp.where(tri, -off_row, -jnp.inf), axis=-1, keepdims=True)
        u_b = jnp.broadcast_to(-jnp.maximum(pmax, m_prev), (n, LANES))
        b_b = jnp.broadcast_to(b_col, (n, LANES))
        w_inter = jnp.exp(m_prev + u_b)
        a_end = tot - off_row
        m_loc = jnp.max(a_end, axis=-1, keepdims=True)
        w_end = jnp.exp(a_end - m_loc)
        qh = q_ref[0, :, h * ML_DK:(h + 1) * ML_DK]
        kf = k_ref[0, :, h * ML_DK:(h + 1) * ML_DK].astype(F32) * (ML_DK ** -0.5)
        kb = kf.astype(BF16)
        vh = v_ref[0, :, h * ML_DV:(h + 1) * ML_DV]
        s_mat = _dot_nt(qh, kb) * jnp.exp(jnp.where(tri, jnp.tile(u_b, (1, n // LANES)) - off_row, -jnp.inf))
        s_hi, s_lo = _split_bf16(s_mat, 2)
        c_prev = c_st[h]
        n_prev = n_st[h]
        ones = jnp.ones((n, LANES), BF16)
        den = _dot(s_hi, ones) + _dot(s_lo, ones) + _dot(qh, n_prev.astype(BF16)) * w_inter
        inv = 1.0 / jnp.maximum(jnp.abs(den), jnp.exp(u_b - b_b))
        reps = ML_DV // LANES
        out = (_dot(s_hi, vh) * jnp.tile(inv, (1, reps))
               + _dot(qh, c_prev.astype(BF16)) * jnp.tile(w_inter * inv, (1, reps)))
        kw_t = (kf.T * w_end).astype(BF16)
        c_chunk = _dot(kw_t, vh)
        n_chunk = _dot(kw_t, ones)
        m_new = jnp.maximum(tot + m_prev, m_loc)
        a_sc = jnp.exp(tot + m_prev - m_new)
        b_sc = jnp.exp(m_loc - m_new)
        c_st[h] = c_prev * a_sc + c_chunk * b_sc
        n_st[h] = n_prev * a_sc + n_chunk * b_sc
        m_st[h] = jnp.broadcast_to(m_new, m_st.shape[1:])
        if rev:
            out = out + acc_ref[0, :, h * ML_DV:(h + 1) * ML_DV].astype(F32)
        o_ref[0, :, h * ML_DV:(h + 1) * ML_DV] = out.astype(o_ref.dtype)


def _mlstm_scan(q, k, v, gates, ib_row, fb_row, acc, *, rev, ctx_len):
    bsz, s, _ = q.shape
    n = ML_CHUNK
    n_all = s // n
    n_ctx = ctx_len // n
    cidx = functools.partial(_scan_chunk_index, rev=rev, n_ctx=n_ctx, n_all=n_all)
    tok = lambda b, c: (b, cidx(c), 0)
    qw, vw = ML_HEADS * ML_DK, ML_HEADS * ML_DV
    in_specs = [pl.BlockSpec((1, n, qw), tok), pl.BlockSpec((1, n, qw), tok),
                pl.BlockSpec((1, n, vw), tok), pl.BlockSpec((1, n, LANES), tok),
                pl.BlockSpec((1, LANES), lambda b, c: (0, 0)),
                pl.BlockSpec((1, LANES), lambda b, c: (0, 0))]
    args = [q, k, v, gates, ib_row, fb_row]
    if rev:
        in_specs.append(pl.BlockSpec((1, n, vw), tok))
        args.append(acc)
    return pl.pallas_call(
        functools.partial(_mlstm_kernel, rev=rev, d=1 if rev else 0),
        out_shape=jax.ShapeDtypeStruct((bsz, s, vw), F32),
        grid=(bsz, n_all), in_specs=in_specs,
        out_specs=pl.BlockSpec((1, n, vw), tok),
        scratch_shapes=[pltpu.VMEM((ML_HEADS, ML_DK, ML_DV), F32),
                        pltpu.VMEM((ML_HEADS, ML_DK, LANES), F32),
                        pltpu.VMEM((ML_HEADS, 8, LANES), F32)],
        compiler_params=_params(2), name="mlstm_scan_bwd" if rev else "mlstm_scan_fwd",
    )(*args)


def _post_kernel(*refs, mode, split_tiles, x_tile_off):
    x, refs = _stream_tile(refs, split_tiles, x_tile_off)
    if mode == "mix":
        (y_ref, z_ref, hm_ref, og_ref, snw_ref, mnw_ref, wa_ref, wb_ref,
         mod_ref, n2_ref, wr_ref, br_ref, xo_ref, h2_ref, lg_ref) = refs
        y = _rms(y_ref[0].astype(F32) * _silu(z_ref[0].astype(F32)), snw_ref[...])
        og = og_ref[0].astype(F32)
        o = _dot(y.astype(BF16), wa_ref[...])
        parts = []
        for h in range(ML_HEADS):
            sl = slice(h * ML_DV, (h + 1) * ML_DV)
            parts.append(_rms(hm_ref[0, :, sl], mnw_ref[:, sl]) * jax.nn.sigmoid(og[:, sl]))
        o = o + _dot(jnp.concatenate(parts, axis=-1).astype(BF16), wb_ref[...])
    else:
        (a_ref, wa_ref, mod_ref, n2_ref, wr_ref, br_ref, xo_ref, h2_ref, lg_ref) = refs
        o = _dot(a_ref[0], wa_ref[...])
    x_new = x + mod_ref[0, 2:3, :] * o
    xo_ref[0] = x_new
    h2 = _rms(x_new, n2_ref[...]) * (1.0 + mod_ref[0, 4:5, :]) + mod_ref[0, 3:4, :]
    h2_ref[0] = h2
    lg_ref[0] = _dot_split(h2, wr_ref) + br_ref[...]


def _post_call(mode, x, acts, rows, mats, mod_l, norm2_w, w_router, b_router, x_tile_off, n_tiles, ctx_tiles, ctx_row):
    tm = ROW_TILE
    x_args, in_specs, split_tiles = _stream_specs(x, tm, x_tile_off)
    bsz, d = x_args[0].shape[0], x_args[0].shape[2]
    s_out = n_tiles * tm
    tok = lambda b, i: (b, i, 0)
    const = lambda b, i: (0, 0)
    mod_idx = lambda b, i: (jnp.where(i + x_tile_off < ctx_tiles, ctx_row, b), 0, 0)
    in_specs += [pl.BlockSpec((1, tm, a.shape[2]), tok) for a in acts]
    in_specs += [pl.BlockSpec(r.shape, const) for r in rows]
    in_specs += [pl.BlockSpec(m.shape, const) for m in mats]
    in_specs += [pl.BlockSpec((1, 6, d), mod_idx), pl.BlockSpec((1, d), const),
                 pl.BlockSpec(w_router.shape, lambda b, i: (0, 0, 0)), pl.BlockSpec((1, LANES), const)]
    out_shape = [jax.ShapeDtypeStruct((bsz, s_out, d), F32),
                 jax.ShapeDtypeStruct((bsz, s_out, d), F32),
                 jax.ShapeDtypeStruct((bsz, s_out, LANES), F32)]
    out_specs = [pl.BlockSpec((1, tm, d), tok), pl.BlockSpec((1, tm, d), tok),
                 pl.BlockSpec((1, tm, LANES), tok)]
    return pl.pallas_call(
        functools.partial(_post_kernel, mode=mode, split_tiles=split_tiles, x_tile_off=x_tile_off),
        out_shape=out_shape, grid=(bsz, n_tiles), in_specs=in_specs, out_specs=out_specs,
        compiler_params=_params(2), name="post_" + mode,
    )(*x_args, *acts, *rows, *mats, mod_l, norm2_w.reshape(1, d), w_router, b_router)


def _router_kernel(lg_ref, route_ref, cnt_ref, src_ref, all_ref, off_ref, inv_ref, *, moe_tile, n_tokens):
    phase = pl.program_id(0)
    i = pl.program_id(1)
    tm = lg_ref.shape[0]
    rows = pl.ds(pl.multiple_of(i * tm, tm), tm)

    @pl.when((phase == 0) & (i == 0))
    def _():
        cnt_ref[...] = jnp.zeros_like(cnt_ref)

    @pl.when(phase == 0)
    def _():
        all_ref[rows, :] = _route_fields(lg_ref[...], cnt_ref)

    @pl.when((phase == 1) & (i == 0))
    def _():
        tiles = jnp.ceil(cnt_ref[...] * (1.0 / moe_tile))
        r = lax.broadcasted_iota(jnp.int32, (LANES, LANES), 0)
        c = lax.broadcasted_iota(jnp.int32, (LANES, LANES), 1)
        earlier = jnp.where(r < c, 1.0, 0.0).astype(BF16)
        off_ref[...] = _dot(tiles.astype(BF16), earlier) * float(moe_tile)

    @pl.when(phase == 1)
    def _():
        f = all_ref[rows, :]
        lane = lax.broadcasted_iota(jnp.int32, f.shape, 1).astype(F32)
        off = off_ref[0:1, :]
        pos1 = jnp.sum(jnp.where(lane == f[:, 0:1], off, 0.0), axis=-1, keepdims=True) + f[:, 4:5]
        pos2 = jnp.sum(jnp.where(lane == f[:, 1:2], off, 0.0), axis=-1, keepdims=True) + f[:, 5:6]
        route_ref[...] = jnp.where(lane == 6.0, pos1, jnp.where(lane == 7.0, pos2, f))

        n_blk = inv_ref.shape[0]
        pos_t = jnp.where(lane == 0.0, pos1, jnp.where(lane == 1.0, pos2, 0.0)).T
        blk = lax.broadcasted_iota(jnp.int32, (n_blk, tm), 0).astype(F32)
        tok = (i * tm + lax.broadcasted_iota(jnp.int32, (tm, 1), 0)).astype(F32)
        tok_hi = jnp.floor(tok * (1.0 / LANES))
        tok_lo = tok - tok_hi * LANES
        lhs, rhs = [], []
        for k, pos in enumerate((pos1, pos2)):
            blk_of = jnp.floor(pos_t[k:k + 1, :] * (1.0 / LANES))
            lhs.append(jnp.where(blk == blk_of, 1.0, 0.0).astype(BF16))
            hit = lane == pos - jnp.floor(pos * (1.0 / LANES)) * LANES
            rhs.append(jnp.concatenate([jnp.where(hit, tok_hi, 0.0), jnp.where(hit, tok_lo, 0.0),
                                        jnp.where(hit, 1.0, 0.0)], axis=1).astype(BF16))
        upd = _dot(jnp.concatenate(lhs, axis=1), jnp.concatenate(rhs, axis=0))

        @pl.when(i == 0)
        def _():
            inv_ref[...] = upd

        @pl.when(i > 0)
        def _():
            inv_ref[...] = inv_ref[...] + upd

    @pl.when((phase == 1) & (i == pl.num_programs(1) - 1))
    def _():
        acc = inv_ref[...]
        n_blk = acc.shape[0]
        slot = (lax.broadcasted_iota(jnp.int32, (n_blk, LANES), 0) * LANES
                + lax.broadcasted_iota(jnp.int32, (n_blk, LANES), 1)).astype(F32)
        spare = slot - n_tokens * jnp.floor((slot + 0.5) * (1.0 / n_tokens))
        src = jnp.where(acc[:, 2 * LANES:] > 0.0, acc[:, :LANES] * LANES + acc[:, LANES:2 * LANES], spare)
        src_ref[...] = src.astype(jnp.int32)


def _route_fields(lg, cnt_ref):
    tm = lg.shape[0]
    lane = lax.broadcasted_iota(jnp.int32, lg.shape, 1).astype(F32)
    big = float(LANES)
    is_g = (lane >= ROUTE_G) & (lane < ROUTE_G + MOE_GROUPS)
    lgg = jnp.where(is_g, lg, -jnp.inf)
    g_max = jnp.max(lgg, axis=-1, keepdims=True)
    g_idx = jnp.min(jnp.where(lgg == g_max, lane - ROUTE_G, big), axis=-1, keepdims=True)
    g_prob = 1.0 / jnp.sum(jnp.exp(lgg - g_max), axis=-1, keepdims=True)
    lo = g_idx * MOE_EXPERTS
    le = jnp.where((lane >= lo) & (lane < lo + MOE_EXPERTS), lg, -jnp.inf)
    l1 = jnp.max(le, axis=-1, keepdims=True)
    i1 = jnp.min(jnp.where(le == l1, lane, big), axis=-1, keepdims=True)
    le2 = jnp.where(lane == i1, -jnp.inf, le)
    l2 = jnp.max(le2, axis=-1, keepdims=True)
    i2 = jnp.min(jnp.where(le2 == l2, lane, big), axis=-1, keepdims=True)
    r = jnp.exp(l2 - l1)
    w1 = g_prob / (1.0 + r)
    w2 = w1 * r
    oh1 = jnp.where(lane == i1, 1.0, 0.0)
    oh2 = jnp.where(lane == i2, 1.0, 0.0)
    oh = oh1 + oh2
    row = lax.broadcasted_iota(jnp.int32, (tm, tm), 0)
    col = lax.broadcasted_iota(jnp.int32, (tm, tm), 1)
    before = jnp.where(col < row, 1.0, 0.0).astype(BF16)
    prefix = _dot(before, oh.astype(BF16)) + cnt_ref[0:1, :]
    rank1 = jnp.sum(prefix * oh1, axis=-1, keepdims=True)
    rank2 = jnp.sum(prefix * oh2, axis=-1, keepdims=True)
    cnt_ref[...] = cnt_ref[...] + jnp.sum(oh, axis=0, keepdims=True)
    fields = (i1, i2, w1, w2, rank1, rank2)
    out = jnp.zeros_like(lg)
    for j, f in enumerate(fields):
        out = jnp.where(lane == float(j), f, out)
    return out


def _router(logits, n_sorted):
    t = logits.shape[0]
    tm = ROW_TILE
    n_blk = n_sorted // LANES
    fixed = lambda p, i: (0, 0)
    return pl.pallas_call(
        functools.partial(_router_kernel, moe_tile=MOE_TILE, n_tokens=t),
        out_shape=[jax.ShapeDtypeStruct((t, LANES), F32), jax.ShapeDtypeStruct((8, LANES), F32),
                   jax.ShapeDtypeStruct((n_blk, LANES), jnp.int32)],
        grid=(2, t // tm),
        in_specs=[pl.BlockSpec((tm, LANES), lambda p, i: (i * (1 - p), 0))],
        out_specs=[pl.BlockSpec((tm, LANES), lambda p, i: (i * p, 0)), pl.BlockSpec((8, LANES), fixed),
                   pl.BlockSpec((n_blk, LANES), fixed)],
        scratch_shapes=[pltpu.VMEM((t, LANES), F32), pltpu.VMEM((8, LANES), F32),
                        pltpu.VMEM((n_blk, 3 * LANES), F32)],
        compiler_params=_params(2), name="router",
    )(logits)


def _moe_kernel(te_ref, slot_a_ref, slot_b_ref, use_b_ref, nt_ref, x_ref, *refs, tile0):
    slot_a, slot_b = refs[0:3], refs[3:6]
    o_ref, gate_bf, up_bf, down_bf = refs[-4:]
    i = pl.program_id(0)
    tile = i + tile0
    valid = tile < nt_ref[0]
    fresh = (i == 0) | (te_ref[tile] != te_ref[jnp.maximum(tile - 1, 0)])

    def load(slot):
        gate_bf[...] = slot[0][0].astype(BF16)
        up_bf[...] = slot[1][0].astype(BF16)
        down_bf[...] = slot[2][0].astype(BF16)

    pl.when(valid & fresh & (use_b_ref[tile] == 0))(lambda: load(slot_a))
    pl.when(valid & fresh & (use_b_ref[tile] != 0))(lambda: load(slot_b))

    @pl.when(valid)
    def _():
        x = x_ref[...].astype(BF16)
        act = _silu(_dot(x, gate_bf[...])) * _dot(x, up_bf[...])
        o_ref[...] = _dot(act.astype(BF16), down_bf[...]).astype(o_ref.dtype)

    @pl.when(jnp.logical_not(valid))
    def _():
        o_ref[...] = jnp.zeros_like(o_ref)


def _moe_experts(x_part, y_buf, part, n_parts, plan, wg, wu, wd):
    tm = MOE_TILE
    part_rows, d = x_part.shape
    part_tiles = part_rows // tm
    tile0 = part * part_tiles
    ff = wg.shape[2]
    cur = lambda i, te, sa, sb, ub, nt: (i, 0)
    idx_a = lambda i, te, sa, sb, ub, nt: (sa[i + tile0], 0, 0)
    idx_b = lambda i, te, sa, sb, ub, nt: (sb[i + tile0], 0, 0)
    in_specs = [pl.BlockSpec((tm, d), cur),
                pl.BlockSpec((1, d, ff), idx_a), pl.BlockSpec((1, d, ff), idx_a), pl.BlockSpec((1, ff, d), idx_a),
                pl.BlockSpec((1, d, ff), idx_b), pl.BlockSpec((1, d, ff), idx_b), pl.BlockSpec((1, ff, d), idx_b)]
    args = [*plan, x_part, wg, wu, wd, wg, wu, wd]
    aliases = {}
    if y_buf is not None:
        in_specs.append(pl.BlockSpec(memory_space=pl.ANY))
        aliases = {len(args): 0}
        args.append(y_buf)
    grid_spec = pltpu.PrefetchScalarGridSpec(
        num_scalar_prefetch=5, grid=(part_tiles,), in_specs=in_specs,
        out_specs=pl.BlockSpec((tm, d), lambda i, te, sa, sb, ub, nt: (i + tile0, 0)),
        scratch_shapes=[pltpu.VMEM((d, ff), BF16), pltpu.VMEM((d, ff), BF16), pltpu.VMEM((ff, d), BF16)])
    return pl.pallas_call(
        functools.partial(_moe_kernel, tile0=tile0),
        out_shape=jax.ShapeDtypeStruct((part_rows * n_parts, d), F32), grid_spec=grid_spec,
        input_output_aliases=aliases, compiler_params=_params(1), name="moe_experts",
    )(*args)


def _expert_plan(tiles_per, n_tiles, layer):
    e = jnp.arange(N_EXPERTS, dtype=jnp.int32)
    tile_end = jnp.cumsum(tiles_per)
    tile_ids = jnp.arange(n_tiles, dtype=jnp.int32)
    cur = jnp.minimum(jnp.sum((tile_end[None, :] <= tile_ids[:, None]).astype(jnp.int32), axis=1), N_EXPERTS - 1)
    has = tiles_per > 0
    visit = jnp.cumsum(has.astype(jnp.int32)) - has.astype(jnp.int32)
    later = jnp.where((e[None, :] > e[:, None]) & has[None, :], e[None, :], N_EXPERTS)
    nxt = jnp.min(later, axis=1)
    nxt = jnp.where(nxt == N_EXPERTS, e, nxt)
    use_b = visit[cur] % 2
    nx = nxt[cur]
    base = layer * N_EXPERTS
    slot_a = jnp.where(use_b == 0, cur, nx) + base
    slot_b = jnp.where(use_b == 0, nx, cur) + base
    return (cur + base, slot_a, slot_b, use_b, tile_end[-1:])


def _combine_kernel(x_ref, y1_ref, y2_ref, rt_ref, mod_ref, o_ref):
    f = rt_ref[0, :, 2:3] * y1_ref[0] + rt_ref[0, :, 3:4] * y2_ref[0]
    o_ref[0] = x_ref[0] + mod_ref[0, 5:6, :] * f


def _combine(x, y1, y2, route, mod_l, ctx_tiles, ctx_row):
    bsz, s, d = x.shape
    tm = ROW_TILE
    tok = lambda b, i: (b, i, 0)
    mod_idx = lambda b, i: (jnp.where(i < ctx_tiles, ctx_row, b), 0, 0)
    return pl.pallas_call(
        _combine_kernel, out_shape=jax.ShapeDtypeStruct(x.shape, F32), grid=(bsz, s // tm),
        in_specs=[pl.BlockSpec((1, tm, d), tok), pl.BlockSpec((1, tm, d), tok), pl.BlockSpec((1, tm, d), tok),
                  pl.BlockSpec((1, tm, LANES), tok), pl.BlockSpec((1, 6, d), mod_idx)],
        out_specs=pl.BlockSpec((1, tm, d), tok),
        compiler_params=_params(2), name="moe_combine",
    )(x, y1.reshape(x.shape), y2.reshape(x.shape), route.reshape(bsz, s, LANES), mod_l)


def _hier_moe(h2, logits, layer, wg, wu, wd):
    bsz, s, d = h2.shape
    t = bsz * s
    tm = MOE_TILE
    n_tiles = 2 * t // tm + N_EXPERTS
    route, counts, src = _router(logits.reshape(t, LANES), n_tiles * tm)
    tiles_per = (counts[0, :N_EXPERTS].astype(jnp.int32) + tm - 1) // tm
    plan = _expert_plan(tiles_per, n_tiles, layer)
    pos = route[:, 6:8].astype(jnp.int32)
    weights = (wg.reshape(-1, d, wg.shape[-1]), wu.reshape(-1, d, wu.shape[-1]), wd.reshape(-1, wd.shape[-2], d))
    assert n_tiles % MOE_PARTS == 0
    src = src.reshape(MOE_PARTS, -1)
    h2_rows = h2.reshape(t, d)
    y_sorted = None
    for part in range(MOE_PARTS):
        x_part = jnp.take(h2_rows, src[part], axis=0, mode="clip")
        y_sorted = _moe_experts(x_part, y_sorted, part, MOE_PARTS, plan, *weights)
    y1 = jnp.take(y_sorted, pos[:, 0], axis=0, mode="clip")
    y2 = jnp.take(y_sorted, pos[:, 1], axis=0, mode="clip")
    return y1, y2, route


def _attn_kernel(q_ref, k_ref, v_ref, bias_ref, qw_ref, kw_ref, o_ref, qn_ref, kn_ref, *, ctx_len, rows):
    lane = lax.broadcasted_iota(jnp.int32, (1, LANES), 1)
    first = lane < NA_HEAD_DIM

    r_head = lax.broadcasted_iota(jnp.int32, (LANES, LANES), 0) // NA_HEAD_DIM
    c_head = lax.broadcasted_iota(jnp.int32, (LANES, LANES), 1) // NA_HEAD_DIM
    same_head = jnp.where(r_head == c_head, 1.0, 0.0).astype(BF16)

    def head_norm(x, w):
        ms = _dot((x * x).astype(BF16), same_head) * (1.0 / NA_HEAD_DIM)
        return x * lax.rsqrt(ms + RMS_EPS) * w

    kn_ref[...] = head_norm(k_ref[0].astype(F32), kw_ref[...]).astype(BF16)
    qn_ref[...] = (head_norm(q_ref[0, ctx_len:, :].astype(F32), qw_ref[...]) * NA_HEAD_DIM ** -0.5).astype(BF16)
    n_groups = rows // ATTN_GROUP_ROWS
    n_q = ATTN_GROUP_ROWS * GRID_W
    n_loc = ATTN_KEY_ROWS * GRID_W
    k_ctx = kn_ref[0:ctx_len, :]
    v_ctx = v_ref[0, 0:ctx_len, :]

    def one_group(g):
        kind = jnp.where(g == 0, 0, jnp.where(g == n_groups - 1, 2, 1))
        kr0 = jnp.clip(g * ATTN_GROUP_ROWS - NA_KH // 2, 0, rows - ATTN_KEY_ROWS)
        q_rows = pl.ds(pl.multiple_of(g * n_q, n_q), n_q)
        q = qn_ref[q_rows, :]
        zero = jnp.zeros_like(q)
        q2 = jnp.concatenate([jnp.where(first, q, zero), jnp.where(first, zero, q)], axis=0)
        k_off = pl.multiple_of(ctx_len + kr0 * GRID_W, GRID_W)
        s_loc = _dot_nt(q2, kn_ref[pl.ds(k_off, n_loc), :])
        s_ctx = _dot_nt(q2, k_ctx)
        p_loc, p_ctx, inv = [], [], []
        for hh in range(2):
            sl = s_loc[hh * n_q:(hh + 1) * n_q] + bias_ref[hh, kind]
            sc = s_ctx[hh * n_q:(hh + 1) * n_q]
            m = jnp.maximum(jnp.max(sl, axis=-1, keepdims=True), jnp.max(sc, axis=-1, keepdims=True))
            el = jnp.exp(sl - m)
            ec = jnp.exp(sc - m)
            inv.append(1.0 / (jnp.sum(el, axis=-1, keepdims=True) + jnp.sum(ec, axis=-1, keepdims=True)))
            p_loc.append(el.astype(BF16))
            p_ctx.append(ec.astype(BF16))
        o = (_dot(jnp.concatenate(p_loc, axis=0), v_ref[0, pl.ds(k_off, n_loc), :])
             + _dot(jnp.concatenate(p_ctx, axis=0), v_ctx))
        o_ref[0, q_rows, :] = jnp.where(first, o[:n_q] * inv[0], o[n_q:] * inv[1]).astype(o_ref.dtype)

    def body(i, carry):
        for j in range(ATTN_GROUPS_PER_TRIP):
            one_group(i * ATTN_GROUPS_PER_TRIP + j)
        return carry

    lax.fori_loop(0, n_groups // ATTN_GROUPS_PER_TRIP, body, 0)


def _attn_group_layout(rows):
    n_groups = rows // ATTN_GROUP_ROWS
    assert rows % ATTN_GROUP_ROWS == 0 and rows >= ATTN_KEY_ROWS and n_groups >= 2
    u = np.arange(ATTN_GROUP_ROWS)[:, None]
    i = np.arange(ATTN_KEY_ROWS)[None, :]

    def layout(g):
        r = g * ATTN_GROUP_ROWS + u
        r0 = np.clip(r - NA_KH // 2, 0, rows - NA_KH)
        kr = np.clip(g * ATTN_GROUP_ROWS - NA_KH // 2, 0, rows - ATTN_KEY_ROWS) + i
        return (kr >= r0) & (kr < r0 + NA_KH), kr - r + NA_KH - 1

    kinds = [layout(0), layout(1), layout(n_groups - 1)]
    for g in range(1, n_groups - 1):
        valid, d = layout(g)
        assert (valid == kinds[1][0]).all() and (d[valid] == kinds[1][1][valid]).all()
    return np.stack([k[0] for k in kinds]), np.stack([k[1] for k in kinds])


def _bias_windows(rpb, rows):
    qc = np.arange(GRID_W)
    c0 = np.clip(qc - NA_KW // 2, 0, GRID_W - NA_KW)
    kc = np.arange(GRID_W)
    inwin = (kc[None, :] >= c0[:, None]) & (kc[None, :] < c0[:, None] + NA_KW)
    coff = kc[None, :] - qc[:, None] + NA_KW - 1
    pick = (coff[..., None] == np.arange(2 * NA_KW - 1)) & inwin[..., None]
    tab = jnp.einsum("hdo,qko->dhqk", rpb.astype(F32), jnp.asarray(pick, F32), precision=HIGHEST)
    tab = jnp.where(inwin[None, None], tab, NEG)
    valid, d = _attn_group_layout(rows)
    masked = jnp.full(tab.shape[1:], NEG, F32)
    blocks = []
    for kind in range(valid.shape[0]):
        for u in range(ATTN_GROUP_ROWS):
            blocks.append(jnp.concatenate(
                [tab[int(d[kind, u, i])] if valid[kind, u, i] else masked for i in range(ATTN_KEY_ROWS)], axis=-1))
    win = jnp.stack(blocks, axis=1)
    return win.reshape(rpb.shape[0], valid.shape[0], ATTN_GROUP_ROWS * GRID_W, ATTN_KEY_ROWS * GRID_W)


def _neighbourhood_attention(q, k, v, bias, qn_w, kn_w, ctx_len):
    bsz, s, w = q.shape
    seq = s - ctx_len
    rows = seq // GRID_W
    n_pairs = NA_HEADS // 2
    pair = lambda b, p: (b, 0, p)
    row2 = lambda b, p: (0, 0)
    return pl.pallas_call(
        functools.partial(_attn_kernel, ctx_len=ctx_len, rows=rows),
        out_shape=jax.ShapeDtypeStruct((bsz, seq, w), BF16),
        grid=(bsz, n_pairs),
        in_specs=[pl.BlockSpec((1, s, LANES), pair), pl.BlockSpec((1, s, LANES), pair),
                  pl.BlockSpec((1, s, LANES), pair),
                  pl.BlockSpec((2,) + bias.shape[1:], lambda b, p: (p, 0, 0, 0)),
                  pl.BlockSpec((1, LANES), row2), pl.BlockSpec((1, LANES), row2)],
        out_specs=pl.BlockSpec((1, seq, LANES), pair),
        scratch_shapes=[pltpu.VMEM((seq, LANES), BF16), pltpu.VMEM((s, LANES), BF16)],
        compiler_params=_params(2), name="neighbourhood_attention",
    )(q, k, v, bias, jnp.tile(qn_w, 2).reshape(1, LANES), jnp.tile(kn_w, 2).reshape(1, LANES))


def _pad_row(pieces, width=LANES):
    row = jnp.zeros((width,), F32)
    for off, vec in pieces:
        row = row.at[off:off + vec.shape[0]].set(vec.astype(F32))
    return row.reshape(1, width)


def _router_params(w_group, b_group, w_expert, b_expert):
    d = w_group.shape[0]
    w = jnp.zeros((d, LANES), F32).at[:, :N_EXPERTS].set(w_expert).at[:, ROUTE_G:ROUTE_G + MOE_GROUPS].set(w_group)
    return _split_weight(w), _pad_row([(0, b_expert), (ROUTE_G, b_group)])


def kernel(x, c, ctx, c_ctx, norm1_w, norm2_w, mod_w, mod_b, ab_w_in, ab_conv_w, ab_conv_b, ssd_a_log, ssd_dt_bias, ssd_d, ssd_norm_w, ml_i_bias, ml_f_bias, ml_norm_w, ab_w_out, na_w_qkv, na_q_norm, na_k_norm, na_rpb, na_w_out, moe_w_group, moe_b_group, moe_w_expert, moe_b_expert, moe_w_gate, moe_w_up, moe_w_down):
    bsz, seq, d = x.shape
    ctx_len = ctx.shape[1]
    depth = mod_w.shape[0]
    assert depth == 2 and ctx_len % ROW_TILE == 0 and seq % ROW_TILE == 0 and bsz < 8
    ctx_tiles = ctx_len // ROW_TILE
    lat_tiles = seq // ROW_TILE
    ctx_row = bsz

    cvec = jnp.zeros((8, d), F32).at[:bsz].set(c).at[bsz].set(c_ctx)
    mod = _mod_vectors(cvec, mod_w, mod_b)
    xs = (ctx, x)

    ssd_w = SSD_HEADS * SSD_HEAD_DIM
    xbc_w = ssd_w + 2 * SSD_GROUPS * SSD_STATE
    qk_w, v_w = ML_HEADS * ML_DK, ML_HEADS * ML_DV
    sizes = (ssd_w, xbc_w, 2 * SSD_HEADS, qk_w, qk_w, v_w, v_w, 2 * ML_HEADS, 2 * ML_HEADS)
    w_z, w_xbc, w_dt, w_q, w_k, w_v, w_o, w_i, w_f = jnp.split(ab_w_in[0], np.cumsum(sizes)[:-1].tolist(), axis=1)
    w_gate = jnp.zeros((d, LANES), F32).at[:, :GATE_F + 2 * ML_HEADS].set(jnp.concatenate([w_dt, w_i, w_f], axis=1))
    weights = [w.astype(BF16) for w in (w_z, w_xbc, w_q, w_k, w_v, w_o)] + [_split_weight(w_gate)]
    z, xbc, q, k, v, og, gates = _norm_mod_matmul(xs, norm1_w[0], mod[0], weights, [BF16] * 6 + [F32],
                                                  ctx_tiles, ctx_row)
    xbc = _conv_silu(xbc, ab_conv_w[0], ab_conv_b[0], ctx_len)
    n_ctx = ctx_len // SCAN_CHUNK
    a_neg = -jnp.exp(ssd_a_log[0].astype(F32))
    dsk_row = jnp.repeat(ssd_d[0].astype(F32), SSD_HEAD_DIM).reshape(1, ssd_w)
    y = None
    hm = None
    for dr in range(2):
        rev = dr == 1
        dtb_row = _pad_row([(GATE_DT + dr * SSD_HEADS, ssd_dt_bias[0, dr])])
        a_row = _pad_row([(GATE_DT + dr * SSD_HEADS, a_neg[dr])])
        y = _ssd_scan(xbc, gates, dtb_row, a_row, y if rev else dsk_row, rev=rev, n_ctx=n_ctx)
        ib_row = _pad_row([(GATE_I + dr * ML_HEADS, ml_i_bias[0, dr])])
        fb_row = _pad_row([(GATE_F + dr * ML_HEADS, ml_f_bias[0, dr])])
        hm = _mlstm_scan(q, k, v, gates, ib_row, fb_row, hm, rev=rev, ctx_len=ctx_len)
    w_r, b_r = _router_params(moe_w_group[0], moe_b_group[0], moe_w_expert[0], moe_b_expert[0])
    w_out = ab_w_out[0].astype(BF16)
    x1, h2, logits = _post_call(
        "mix", xs, [y, z, hm, og],
        [ssd_norm_w[0].reshape(1, ssd_w), ml_norm_w[0].reshape(1, v_w)], [w_out[:ssd_w], w_out[ssd_w:]],
        mod[0], norm2_w[0], w_r, b_r, 0, ctx_tiles + lat_tiles, ctx_tiles, ctx_row)
    moe0 = _hier_moe(h2, logits, 0, moe_w_gate, moe_w_up, moe_w_down)

    w_qkv = na_w_qkv[0].astype(BF16)
    na_w = NA_HEADS * NA_HEAD_DIM
    q, k, v, xs = _norm_mod_matmul(x1, norm1_w[1], mod[1],
                                   [w_qkv[:, :na_w], w_qkv[:, na_w:2 * na_w], w_qkv[:, 2 * na_w:]],
                                   [BF16] * 3, ctx_tiles, ctx_row, pending_moe=(*moe0, mod[0]))
    bias = _bias_windows(na_rpb[0], seq // GRID_W)
    attn = _neighbourhood_attention(q, k, v, bias, na_q_norm[0], na_k_norm[0], ctx_len)
    w_r, b_r = _router_params(moe_w_group[1], moe_b_group[1], moe_w_expert[1], moe_b_expert[1])
    x1, h2, logits = _post_call("attn", xs, [attn], [], [na_w_out[0].astype(BF16)],
                                mod[1], norm2_w[1], w_r, b_r, ctx_tiles, lat_tiles, ctx_tiles, ctx_row)
    y1, y2, route = _hier_moe(h2, logits, 1, moe_w_gate, moe_w_up, moe_w_down)
    return _combine(x1, y1, y2, route, mod[1], 0, ctx_row)
```

```python
import functools

import numpy as np
import jax
import jax.numpy as jnp
from jax import lax
from jax.experimental import pallas as pl
from jax.experimental.pallas import tpu as pltpu

F32 = jnp.float32
BF16 = jnp.bfloat16
HIGHEST = lax.Precision.HIGHEST

RMS_EPS = 1e-6
GRID_W = 64
SSD_HEADS = 16
SSD_HEAD_DIM = 64
SSD_GROUPS = 2
SSD_STATE = 128
ML_HEADS = 4
ML_DK = 128
ML_DV = 256
NA_HEADS = 16
NA_HEAD_DIM = 64
NA_KH = 8
NA_KW = 16
MOE_GROUPS = 4
MOE_EXPERTS = 8
N_EXPERTS = MOE_GROUPS * MOE_EXPERTS

LANES = 128
ROW_TILE = 256
SCAN_CHUNK = 128
SSD_BLOCK_CHUNKS = 2
ML_CHUNK = 256
MOE_TILE = 256
ATTN_GROUP_ROWS = 4
ATTN_KEY_ROWS = ATTN_GROUP_ROWS + NA_KH - 1
ATTN_GROUPS_PER_TRIP = 4
VMEM_LIMIT = 56 * 1024 * 1024

GATE_DT = 0
GATE_I = 2 * SSD_HEADS
GATE_F = GATE_I + 2 * ML_HEADS
ROUTE_G = N_EXPERTS
NEG = -1e30


def _params(n_axes):
    return pltpu.CompilerParams(dimension_semantics=("arbitrary",) * n_axes,
                                vmem_limit_bytes=VMEM_LIMIT)


def _silu(x):
    return x * jax.nn.sigmoid(x)


def _softplus(x):
    return jnp.maximum(x, 0.0) + jnp.log1p(jnp.exp(-jnp.abs(x)))


def _rms(x, w):
    return x * lax.rsqrt(jnp.mean(x * x, axis=-1, keepdims=True) + RMS_EPS) * w


def _dot(a, b):
    return jnp.dot(a, b, preferred_element_type=F32)


def _dot_nt(a, b):
    return lax.dot_general(a, b, (((1,), (1,)), ((), ())), preferred_element_type=F32)


def _dot_hi(a, b):
    return jnp.dot(a, b, precision=HIGHEST, preferred_element_type=F32)


def _split_bf16(x, terms):
    parts = []
    for _ in range(terms - 1):
        p = x.astype(BF16)
        parts.append(p)
        x = x - p.astype(F32)
    parts.append(x.astype(BF16))
    return parts


def _split_weight(w):
    return jnp.stack(_split_bf16(w.astype(F32), 2))


def _dot_split(a, w2_ref):
    a_hi, a_lo = _split_bf16(a, 2)
    return _dot(a_hi, w2_ref[0]) + _dot(a_lo, w2_ref[0]) + _dot(a_hi, w2_ref[1])


def _cumsum_dot(tri, x):
    tri = jnp.where(tri, 1.0, 0.0).astype(BF16)
    hi, mid, lo = _split_bf16(x, 3)
    return _dot(tri, hi) + _dot(tri, mid) + _dot(tri, lo)


def _mod_kernel(c_ref, w_ref, b_ref, o_ref):
    o_ref[0] = _dot_hi(_silu(c_ref[...]), w_ref[0]) + b_ref[0]


def _mod_vectors(cvec, mod_w, mod_b, tn=512):
    depth, d, n = mod_w.shape
    rows = cvec.shape[0]
    out = pl.pallas_call(
        _mod_kernel,
        out_shape=jax.ShapeDtypeStruct((depth, rows, n), F32),
        grid=(depth, n // tn),
        in_specs=[pl.BlockSpec((rows, d), lambda l, j: (0, 0)),
                  pl.BlockSpec((1, d, tn), lambda l, j: (l, 0, j)),
                  pl.BlockSpec((1, 1, tn), lambda l, j: (l, 0, j))],
        out_specs=pl.BlockSpec((1, rows, tn), lambda l, j: (l, 0, j)),
        compiler_params=_params(2),
        name="mod_vectors",
    )(cvec, mod_w, mod_b.reshape(depth, 1, n))
    return out.reshape(depth, rows, 6, d)


def _stream_tile(refs, split_tiles, offset=0):
    if not split_tiles:
        return refs[0][0], refs[1:]
    return jnp.where(pl.program_id(1) + offset < split_tiles, refs[0][0], refs[1][0]), refs[2:]


def _stream_specs(xs, tm, offset=0):
    if not isinstance(xs, tuple):
        return [xs], [pl.BlockSpec((1, tm, xs.shape[2]), lambda b, i: (b, i + offset, 0))], 0
    ctx, lat = xs
    split = ctx.shape[1] // tm
    d = ctx.shape[2]
    return ([ctx, lat],
            [pl.BlockSpec((1, tm, d), lambda b, i: (b, jnp.minimum(i + offset, split - 1), 0)),
             pl.BlockSpec((1, tm, d), lambda b, i: (b, jnp.maximum(i + offset - split, 0), 0))], split)


def _nmm_kernel(*refs, n_out, pending_moe, split_tiles):
    x, refs = _stream_tile(refs, split_tiles)
    if pending_moe:
        y1_ref, y2_ref, rt_ref, pmod_ref = refs[:4]
        refs = refs[4:]
        x = x + pmod_ref[0, 5:6, :] * (rt_ref[0, :, 2:3] * y1_ref[0] + rt_ref[0, :, 3:4] * y2_ref[0])
        refs[-1][0] = x
        refs = refs[:-1]
    nw_ref, mod_ref = refs[:2]
    refs = refs[2:]
    w_refs, o_refs = refs[:n_out], refs[n_out:]
    h = _rms(x, nw_ref[...])
    h = h * (1.0 + mod_ref[0, 1:2, :]) + mod_ref[0, 0:1, :]
    hb = h.astype(BF16)
    for w_ref, o_ref in zip(w_refs, o_refs):
        if len(w_ref.shape) == 3:
            o_ref[0] = _dot_split(h, w_ref)
        else:
            o_ref[0] = _dot(hb, w_ref[...]).astype(o_ref.dtype)


def _norm_mod_matmul(xs, norm_w, mod_l, weights, out_dtypes, ctx_tiles, ctx_row, pending_moe=None):
    tm = ROW_TILE
    args, in_specs, split_tiles = _stream_specs(xs, tm)
    bsz, d = args[0].shape[0], args[0].shape[2]
    s = sum(a.shape[1] for a in args)
    tok = lambda b, i: (b, i, 0)
    mod_idx = lambda b, i: (jnp.where(i < ctx_tiles, ctx_row, b), 0, 0)
    if pending_moe is not None:
        y1, y2, route, mod_prev = pending_moe
        args += [y1.reshape(bsz, s, d), y2.reshape(bsz, s, d), route.reshape(bsz, s, LANES), mod_prev]
        in_specs += [pl.BlockSpec((1, tm, d), tok), pl.BlockSpec((1, tm, d), tok),
                     pl.BlockSpec((1, tm, LANES), tok), pl.BlockSpec((1, 6, d), mod_idx)]
    args += [norm_w.reshape(1, d), mod_l, *weights]
    in_specs += [pl.BlockSpec((1, d), lambda b, i: (0, 0)), pl.BlockSpec((1, 6, d), mod_idx)]
    in_specs += [pl.BlockSpec(w.shape, lambda b, i, nd=w.ndim: (0,) * nd) for w in weights]
    out_shape = [jax.ShapeDtypeStruct((bsz, s, w.shape[-1]), dt) for w, dt in zip(weights, out_dtypes)]
    out_specs = [pl.BlockSpec((1, tm, w.shape[-1]), tok) for w in weights]
    if pending_moe is not None:
        out_shape.append(jax.ShapeDtypeStruct((bsz, s, d), F32))
        out_specs.append(pl.BlockSpec((1, tm, d), tok))
    return pl.pallas_call(
        functools.partial(_nmm_kernel, n_out=len(weights), pending_moe=pending_moe is not None,
                          split_tiles=split_tiles),
        out_shape=out_shape, grid=(bsz, s // tm), in_specs=in_specs, out_specs=out_specs,
        compiler_params=_params(2), name="norm_mod_matmul",
    )(*args)


def _conv_kernel(u_ref, w_ref, b_ref, o_ref, *, ctx_len):
    u = u_ref[0].astype(F32)
    s = u.shape[0]
    t = lax.broadcasted_iota(jnp.int32, u.shape, 0)
    prev = jnp.where((t == 0) | (t == ctx_len), 0.0, pltpu.roll(u, 1, axis=0))
    nxt = jnp.where((t == ctx_len - 1) | (t == s - 1), 0.0, pltpu.roll(u, s - 1, axis=0))
    y = prev * w_ref[0:1, :] + u * w_ref[1:2, :] + nxt * w_ref[2:3, :] + b_ref[...]
    o_ref[0] = _silu(y).astype(o_ref.dtype)


def _conv_silu(u, conv_w, conv_b, ctx_len, tc=2 * LANES):
    bsz, s, ch = u.shape
    return pl.pallas_call(
        functools.partial(_conv_kernel, ctx_len=ctx_len),
        out_shape=jax.ShapeDtypeStruct(u.shape, u.dtype),
        grid=(bsz, ch // tc),
        in_specs=[pl.BlockSpec((1, s, tc), lambda b, j: (b, 0, j)),
                  pl.BlockSpec((3, tc), lambda b, j: (0, j)),
                  pl.BlockSpec((1, tc), lambda b, j: (0, j))],
        out_specs=pl.BlockSpec((1, s, tc), lambda b, j: (b, 0, j)),
        compiler_params=_params(2), name="conv_silu",
    )(u, conv_w, conv_b.reshape(1, ch))


def _scan_chunk_index(c, rev, n_ctx, n_all):
    if not rev:
        return c
    return jnp.where(c < n_ctx, n_ctx - 1 - c, n_ctx + n_all - 1 - c)


def _tri(n, rev):
    row = lax.broadcasted_iota(jnp.int32, (n, n), 0)
    col = lax.broadcasted_iota(jnp.int32, (n, n), 1)
    return (col >= row) if rev else (col <= row)


def _ssd_kernel(*refs, rev, off):
    if rev:
        xs_ref, bc_ref, g_ref, dtb_ref, a_ref, ex_ref, acc_ref, o_ref, st_ref = refs
    else:
        xs_ref, bc_ref, g_ref, dtb_ref, a_ref, ex_ref, dsk_ref, o_ref, st_ref = refs

    @pl.when(pl.program_id(1) == 0)
    def _():
        st_ref[...] = jnp.zeros_like(st_ref)

    n = SCAN_CHUNK
    n_sub = xs_ref.shape[1] // n
    for sub in (range(n_sub - 1, -1, -1) if rev else range(n_sub)):
        _ssd_chunk(refs, slice(sub * n, (sub + 1) * n), rev, off)


def _ssd_chunk(refs, rows, rev, off):
    if rev:
        xs_ref, bc_ref, g_ref, dtb_ref, a_ref, ex_ref, acc_ref, o_ref, st_ref = refs
    else:
        xs_ref, bc_ref, g_ref, dtb_ref, a_ref, ex_ref, dsk_ref, o_ref, st_ref = refs
    n = SCAN_CHUNK
    last = 0 if rev else n - 1
    tri = _tri(n, rev)
    dt = _softplus(g_ref[0, rows, :] + dtb_ref[...])
    log_a = dt * a_ref[...]
    cs = _cumsum_dot(tri, log_a)
    cs_t = cs.T
    dt_hi = dt.astype(BF16)
    dt_lo = (dt - dt_hi.astype(F32)).astype(BF16)
    dt_full = _dot(dt_hi, ex_ref[...]) + _dot(dt_lo, ex_ref[...])
    xs = xs_ref[0, rows, :].astype(F32)
    xdt = (xs * dt_full).astype(BF16)
    lo_half = lax.broadcasted_iota(jnp.int32, (1, LANES), 1) < SSD_HEAD_DIM
    gw = SSD_GROUPS * SSD_STATE
    heads_per_group = SSD_HEADS // SSD_GROUPS
    for g in range(SSD_GROUPS):
        b_g = bc_ref[0, rows, g * SSD_STATE:(g + 1) * SSD_STATE]
        c_g = bc_ref[0, rows, gw + g * SSD_STATE:gw + (g + 1) * SSD_STATE]
        cb = _dot_nt(c_g, b_g)
        b_t = b_g.astype(F32).T
        for e in range(0, heads_per_group, 2):
            h0 = g * heads_per_group + e
            pair = h0 // 2
            sl = slice(pair * LANES, (pair + 1) * LANES)
            x_pair = xdt[:, sl]
            zero = jnp.zeros_like(x_pair)
            y, upd, a_bc, tots = None, None, [], []
            for j in range(2):
                col = off + h0 + j
                x_j = jnp.where(lo_half, x_pair, zero) if j == 0 else jnp.where(lo_half, zero, x_pair)
                a_b = jnp.broadcast_to(cs[:, col:col + 1], (n, LANES))
                a_row = cs_t[col:col + 1, :]
                tot = cs_t[col:col + 1, last:last + 1]
                decay = jnp.exp(jnp.where(tri, a_b - a_row, -jnp.inf))
                y_j = _dot((cb * decay).astype(BF16), x_j)
                upd_j = _dot((b_t * jnp.exp(tot - a_row)).astype(BF16), x_j)
                y = y_j if y is None else y + y_j
                upd = upd_j if upd is None else upd + upd_j
                a_bc.append(a_b)
                tots.append(tot)
            state = st_ref[pair]
            y = y + _dot(c_g, state.astype(BF16)) * jnp.exp(jnp.where(lo_half, a_bc[0], a_bc[1]))
            st_ref[pair] = state * jnp.exp(jnp.where(lo_half, tots[0], tots[1])) + upd
            if rev:
                y = y + acc_ref[0, rows, sl].astype(F32)
            else:
                y = y + dsk_ref[:, sl] * xs[:, sl]
            o_ref[0, rows, sl] = y.astype(o_ref.dtype)


def _ssd_scan(xbc, gates, dtb_row, a_row, extra, *, rev, ctx_len):
    bsz, s, _ = xbc.shape
    n = SCAN_CHUNK * SSD_BLOCK_CHUNKS
    n_all = s // n
    n_ctx = ctx_len // n
    w = SSD_HEADS * SSD_HEAD_DIM
    bcw = 2 * SSD_GROUPS * SSD_STATE
    assert SCAN_CHUNK == LANES and 2 * SSD_HEAD_DIM == LANES and (SSD_HEADS // SSD_GROUPS) % 2 == 0
    assert ctx_len % n == 0 and s % n == 0
    off = GATE_DT + (SSD_HEADS if rev else 0)
    cidx = functools.partial(_scan_chunk_index, rev=rev, n_ctx=n_ctx, n_all=n_all)
    tok = lambda b, c: (b, cidx(c), 0)
    const = lambda b, c: (0, 0)
    expand = jnp.asarray(np.arange(LANES)[:, None] == off + np.arange(w)[None, :] // SSD_HEAD_DIM, BF16)
    in_specs = [pl.BlockSpec((1, n, w), tok),
                pl.BlockSpec((1, n, bcw), lambda b, c: (b, cidx(c), w // bcw)),
                pl.BlockSpec((1, n, LANES), tok),
                pl.BlockSpec((1, LANES), const),
                pl.BlockSpec((1, LANES), const),
                pl.BlockSpec((LANES, w), const)]
    if rev:
        in_specs.append(pl.BlockSpec((1, n, w), tok))
    else:
        in_specs.append(pl.BlockSpec((1, w), const))
    return pl.pallas_call(
        functools.partial(_ssd_kernel, rev=rev, off=off),
        out_shape=jax.ShapeDtypeStruct((bsz, s, w), BF16),
        grid=(bsz, n_all), in_specs=in_specs,
        out_specs=pl.BlockSpec((1, n, w), tok),
        scratch_shapes=[pltpu.VMEM((SSD_HEADS // 2, SSD_STATE, LANES), F32)],
        compiler_params=_params(2), name="ssd_scan_bwd" if rev else "ssd_scan_fwd",
    )(xbc, xbc, gates, dtb_row, a_row, expand, extra)


def _mlstm_kernel(*refs, rev, d):
    if rev:
        q_ref, k_ref, v_ref, g_ref, ib_ref, fb_ref, acc_ref, o_ref, c_st, n_st, m_st = refs
    else:
        q_ref, k_ref, v_ref, g_ref, ib_ref, fb_ref, o_ref, c_st, n_st, m_st = refs

    @pl.when(pl.program_id(1) == 0)
    def _():
        c_st[...] = jnp.zeros_like(c_st)
        n_st[...] = jnp.zeros_like(n_st)
        m_st[...] = jnp.zeros_like(m_st)

    n = q_ref.shape[1]
    last = 0 if rev else n - 1
    tri = _tri(n, rev)
    g = g_ref[0]
    log_i = g + ib_ref[...]
    log_f = -_softplus(-(g + fb_ref[...]))
    cs = _cumsum_dot(tri, log_f)
    cs_t = cs.T
    li_t = log_i.T
    for h in range(ML_HEADS):
        ci = GATE_I + ML_HEADS * d + h
        cf = GATE_F + ML_HEADS * d + h
        b_col = cs[:, cf:cf + 1]
        off_row = cs_t[cf:cf + 1, :] - li_t[ci:ci + 1, :]
        tot = cs_t[cf:cf + 1, last:last + 1]
        m_prev = m_st[h, 0:1, 0:1]
        pmax = jnp.max(jnp.where(tri, -off_row, -jnp.inf), axis=-1, keepdims=True)
        u_b = jnp.broadcast_to(-jnp.maximum(pmax, m_prev), (n, LANES))
        b_b = jnp.broadcast_to(b_col, (n, LANES))
        w_inter = jnp.exp(m_prev + u_b)
        a_end = tot - off_row
        m_loc = jnp.max(a_end, axis=-1, keepdims=True)
        w_end = jnp.exp(a_end - m_loc)
        qh = q_ref[0, :, h * ML_DK:(h + 1) * ML_DK]
        kf = k_ref[0, :, h * ML_DK:(h + 1) * ML_DK].astype(F32) * (ML_DK ** -0.5)
        kb = kf.astype(BF16)
        vh = v_ref[0, :, h * ML_DV:(h + 1) * ML_DV]
        s_mat = _dot_nt(qh, kb) * jnp.exp(jnp.where(tri, jnp.tile(u_b, (1, n // LANES)) - off_row, -jnp.inf))
        s_hi, s_lo = _split_bf16(s_mat, 2)
        c_prev = c_st[h]
        n_prev = n_st[h]
        ones = jnp.ones((n, LANES), BF16)
        den = _dot(s_hi, ones) + _dot(s_lo, ones) + _dot(qh, n_prev.astype(BF16)) * w_inter
        inv = 1.0 / jnp.maximum(jnp.abs(den), jnp.exp(u_b - b_b))
        reps = ML_DV // LANES
        out = (_dot(s_hi, vh) * jnp.tile(inv, (1, reps))
               + _dot(qh, c_prev.astype(BF16)) * jnp.tile(w_inter * inv, (1, reps)))
        kw_t = (kf.T * w_end).astype(BF16)
        c_chunk = _dot(kw_t, vh)
        n_chunk = _dot(kw_t, ones)
        m_new = jnp.maximum(tot + m_prev, m_loc)
        a_sc = jnp.exp(tot + m_prev - m_new)
        b_sc = jnp.exp(m_loc - m_new)
        c_st[h] = c_prev * a_sc + c_chunk * b_sc
        n_st[h] = n_prev * a_sc + n_chunk * b_sc
        m_st[h] = jnp.broadcast_to(m_new, m_st.shape[1:])
        if rev:
            out = out + acc_ref[0, :, h * ML_DV:(h + 1) * ML_DV].astype(F32)
        o_ref[0, :, h * ML_DV:(h + 1) * ML_DV] = out.astype(o_ref.dtype)


def _mlstm_scan(q, k, v, gates, ib_row, fb_row, acc, *, rev, ctx_len):
    bsz, s, _ = q.shape
    n = ML_CHUNK
    n_all = s // n
    n_ctx = ctx_len // n
    cidx = functools.partial(_scan_chunk_index, rev=rev, n_ctx=n_ctx, n_all=n_all)
    tok = lambda b, c: (b, cidx(c), 0)
    qw, vw = ML_HEADS * ML_DK, ML_HEADS * ML_DV
    in_specs = [pl.BlockSpec((1, n, qw), tok), pl.BlockSpec((1, n, qw), tok),
                pl.BlockSpec((1, n, vw), tok), pl.BlockSpec((1, n, LANES), tok),
                pl.BlockSpec((1, LANES), lambda b, c: (0, 0)),
                pl.BlockSpec((1, LANES), lambda b, c: (0, 0))]
    args = [q, k, v, gates, ib_row, fb_row]
    if rev:
        in_specs.append(pl.BlockSpec((1, n, vw), tok))
        args.append(acc)
    return pl.pallas_call(
        functools.partial(_mlstm_kernel, rev=rev, d=1 if rev else 0),
        out_shape=jax.ShapeDtypeStruct((bsz, s, vw), F32),
        grid=(bsz, n_all), in_specs=in_specs,
        out_specs=pl.BlockSpec((1, n, vw), tok),
        scratch_shapes=[pltpu.VMEM((ML_HEADS, ML_DK, ML_DV), F32),
                        pltpu.VMEM((ML_HEADS, ML_DK, LANES), F32),
                        pltpu.VMEM((ML_HEADS, 8, LANES), F32)],
        compiler_params=_params(2), name="mlstm_scan_bwd" if rev else "mlstm_scan_fwd",
    )(*args)


def _post_kernel(*refs, mode, split_tiles, x_tile_off):
    x, refs = _stream_tile(refs, split_tiles, x_tile_off)
    if mode == "mix":
        (y_ref, z_ref, hm_ref, og_ref, snw_ref, mnw_ref, wa_ref, wb_ref,
         mod_ref, n2_ref, wr_ref, br_ref, xo_ref, h2_ref, lg_ref) = refs
        y = _rms(y_ref[0].astype(F32) * _silu(z_ref[0].astype(F32)), snw_ref[...])
        og = og_ref[0].astype(F32)
        o = _dot(y.astype(BF16), wa_ref[...])
        parts = []
        for h in range(ML_HEADS):
            sl = slice(h * ML_DV, (h + 1) * ML_DV)
            parts.append(_rms(hm_ref[0, :, sl], mnw_ref[:, sl]) * jax.nn.sigmoid(og[:, sl]))
        o = o + _dot(jnp.concatenate(parts, axis=-1).astype(BF16), wb_ref[...])
    else:
        (a_ref, wa_ref, mod_ref, n2_ref, wr_ref, br_ref, xo_ref, h2_ref, lg_ref) = refs
        o = _dot(a_ref[0], wa_ref[...])
    x_new = x + mod_ref[0, 2:3, :] * o
    xo_ref[0] = x_new
    h2 = _rms(x_new, n2_ref[...]) * (1.0 + mod_ref[0, 4:5, :]) + mod_ref[0, 3:4, :]
    h2_ref[0] = h2
    lg_ref[0] = _dot_split(h2, wr_ref) + br_ref[...]


def _post_call(mode, x, acts, rows, mats, mod_l, norm2_w, w_router, b_router, x_tile_off, n_tiles, ctx_tiles, ctx_row):
    tm = ROW_TILE
    x_args, in_specs, split_tiles = _stream_specs(x, tm, x_tile_off)
    bsz, d = x_args[0].shape[0], x_args[0].shape[2]
    s_out = n_tiles * tm
    tok = lambda b, i: (b, i, 0)
    const = lambda b, i: (0, 0)
    mod_idx = lambda b, i: (jnp.where(i + x_tile_off < ctx_tiles, ctx_row, b), 0, 0)
    in_specs += [pl.BlockSpec((1, tm, a.shape[2]), tok) for a in acts]
    in_specs += [pl.BlockSpec(r.shape, const) for r in rows]
    in_specs += [pl.BlockSpec(m.shape, const) for m in mats]
    in_specs += [pl.BlockSpec((1, 6, d), mod_idx), pl.BlockSpec((1, d), const),
                 pl.BlockSpec(w_router.shape, lambda b, i: (0, 0, 0)), pl.BlockSpec((1, LANES), const)]
    out_shape = [jax.ShapeDtypeStruct((bsz, s_out, d), F32),
                 jax.ShapeDtypeStruct((bsz, s_out, d), F32),
                 jax.ShapeDtypeStruct((bsz, s_out, LANES), F32)]
    out_specs = [pl.BlockSpec((1, tm, d), tok), pl.BlockSpec((1, tm, d), tok),
                 pl.BlockSpec((1, tm, LANES), tok)]
    return pl.pallas_call(
        functools.partial(_post_kernel, mode=mode, split_tiles=split_tiles, x_tile_off=x_tile_off),
        out_shape=out_shape, grid=(bsz, n_tiles), in_specs=in_specs, out_specs=out_specs,
        compiler_params=_params(2), name="post_" + mode,
    )(*x_args, *acts, *rows, *mats, mod_l, norm2_w.reshape(1, d), w_router, b_router)


def _router_kernel(lg_ref, route_ref, cnt_ref, src_ref, all_ref, off_ref, inv_ref, *, moe_tile, n_tokens):
    phase = pl.program_id(0)
    i = pl.program_id(1)
    tm = lg_ref.shape[0]
    rows = pl.ds(pl.multiple_of(i * tm, tm), tm)

    @pl.when((phase == 0) & (i == 0))
    def _():
        cnt_ref[...] = jnp.zeros_like(cnt_ref)

    @pl.when(phase == 0)
    def _():
        all_ref[rows, :] = _route_fields(lg_ref[...], cnt_ref)

    @pl.when((phase == 1) & (i == 0))
    def _():
        tiles = jnp.ceil(cnt_ref[...] * (1.0 / moe_tile))
        r = lax.broadcasted_iota(jnp.int32, (LANES, LANES), 0)
        c = lax.broadcasted_iota(jnp.int32, (LANES, LANES), 1)
        earlier = jnp.where(r < c, 1.0, 0.0).astype(BF16)
        off_ref[...] = _dot(tiles.astype(BF16), earlier) * float(moe_tile)

    @pl.when(phase == 1)
    def _():
        f = all_ref[rows, :]
        lane = lax.broadcasted_iota(jnp.int32, f.shape, 1).astype(F32)
        off = off_ref[0:1, :]
        pos1 = jnp.sum(jnp.where(lane == f[:, 0:1], off, 0.0), axis=-1, keepdims=True) + f[:, 4:5]
        pos2 = jnp.sum(jnp.where(lane == f[:, 1:2], off, 0.0), axis=-1, keepdims=True) + f[:, 5:6]
        route_ref[...] = jnp.where(lane == 6.0, pos1, jnp.where(lane == 7.0, pos2, f))

        n_blk = inv_ref.shape[0]
        pos_t = jnp.where(lane == 0.0, pos1, jnp.where(lane == 1.0, pos2, 0.0)).T
        blk = lax.broadcasted_iota(jnp.int32, (n_blk, tm), 0).astype(F32)
        tok = (i * tm + lax.broadcasted_iota(jnp.int32, (tm, 1), 0)).astype(F32)
        tok_hi = jnp.floor(tok * (1.0 / LANES))
        tok_lo = tok - tok_hi * LANES
        lhs, rhs = [], []
        for k, pos in enumerate((pos1, pos2)):
            blk_of = jnp.floor(pos_t[k:k + 1, :] * (1.0 / LANES))
            lhs.append(jnp.where(blk == blk_of, 1.0, 0.0).astype(BF16))
            hit = lane == pos - jnp.floor(pos * (1.0 / LANES)) * LANES
            rhs.append(jnp.concatenate([jnp.where(hit, tok_hi, 0.0), jnp.where(hit, tok_lo, 0.0),
                                        jnp.where(hit, 1.0, 0.0)], axis=1).astype(BF16))
        upd = _dot(jnp.concatenate(lhs, axis=1), jnp.concatenate(rhs, axis=0))

        @pl.when(i == 0)
        def _():
            inv_ref[...] = upd

        @pl.when(i > 0)
        def _():
            inv_ref[...] = inv_ref[...] + upd

    @pl.when((phase == 1) & (i == pl.num_programs(1) - 1))
    def _():
        acc = inv_ref[...]
        n_blk = acc.shape[0]
        slot = (lax.broadcasted_iota(jnp.int32, (n_blk, LANES), 0) * LANES
                + lax.broadcasted_iota(jnp.int32, (n_blk, LANES), 1)).astype(F32)
        spare = slot - n_tokens * jnp.floor((slot + 0.5) * (1.0 / n_tokens))
        src = jnp.where(acc[:, 2 * LANES:] > 0.0, acc[:, :LANES] * LANES + acc[:, LANES:2 * LANES], spare)
        src_ref[...] = src.astype(jnp.int32)


def _route_fields(lg, cnt_ref):
    tm = lg.shape[0]
    lane = lax.broadcasted_iota(jnp.int32, lg.shape, 1).astype(F32)
    big = float(LANES)
    is_g = (lane >= ROUTE_G) & (lane < ROUTE_G + MOE_GROUPS)
    lgg = jnp.where(is_g, lg, -jnp.inf)
    g_max = jnp.max(lgg, axis=-1, keepdims=True)
    g_idx = jnp.min(jnp.where(lgg == g_max, lane - ROUTE_G, big), axis=-1, keepdims=True)
    g_prob = 1.0 / jnp.sum(jnp.exp(lgg - g_max), axis=-1, keepdims=True)
    lo = g_idx * MOE_EXPERTS
    le = jnp.where((lane >= lo) & (lane < lo + MOE_EXPERTS), lg, -jnp.inf)
    l1 = jnp.max(le, axis=-1, keepdims=True)
    i1 = jnp.min(jnp.where(le == l1, lane, big), axis=-1, keepdims=True)
    le2 = jnp.where(lane == i1, -jnp.inf, le)
    l2 = jnp.max(le2, axis=-1, keepdims=True)
    i2 = jnp.min(jnp.where(le2 == l2, lane, big), axis=-1, keepdims=True)
    r = jnp.exp(l2 - l1)
    w1 = g_prob / (1.0 + r)
    w2 = w1 * r
    oh1 = jnp.where(lane == i1, 1.0, 0.0)
    oh2 = jnp.where(lane == i2, 1.0, 0.0)
    oh = oh1 + oh2
    row = lax.broadcasted_iota(jnp.int32, (tm, tm), 0)
    col = lax.broadcasted_iota(jnp.int32, (tm, tm), 1)
    before = jnp.where(col < row, 1.0, 0.0).astype(BF16)
    prefix = _dot(before, oh.astype(BF16)) + cnt_ref[0:1, :]
    rank1 = jnp.sum(prefix * oh1, axis=-1, keepdims=True)
    rank2 = jnp.sum(prefix * oh2, axis=-1, keepdims=True)
    cnt_ref[...] = cnt_ref[...] + jnp.sum(oh, axis=0, keepdims=True)
    fields = (i1, i2, w1, w2, rank1, rank2)
    out = jnp.zeros_like(lg)
    for j, f in enumerate(fields):
        out = jnp.where(lane == float(j), f, out)
    return out


def _router(logits, n_sorted):
    t = logits.shape[0]
    tm = ROW_TILE
    n_blk = n_sorted // LANES
    fixed = lambda p, i: (0, 0)
    return pl.pallas_call(
        functools.partial(_router_kernel, moe_tile=MOE_TILE, n_tokens=t),
        out_shape=[jax.ShapeDtypeStruct((t, LANES), F32), jax.ShapeDtypeStruct((8, LANES), F32),
                   jax.ShapeDtypeStruct((n_blk, LANES), jnp.int32)],
        grid=(2, t // tm),
        in_specs=[pl.BlockSpec((tm, LANES), lambda p, i: (i * (1 - p), 0))],
        out_specs=[pl.BlockSpec((tm, LANES), lambda p, i: (i * p, 0)), pl.BlockSpec((8, LANES), fixed),
                   pl.BlockSpec((n_blk, LANES), fixed)],
        scratch_shapes=[pltpu.VMEM((t, LANES), F32), pltpu.VMEM((8, LANES), F32),
                        pltpu.VMEM((n_blk, 3 * LANES), F32)],
        compiler_params=_params(2), name="router",
    )(logits)


def _moe_kernel(te_ref, nt_ref, x_ref, wg_ref, wu_ref, wd_ref, o_ref, gate_bf, up_bf, down_bf):
    i = pl.program_id(0)
    valid = i < nt_ref[0]
    fresh = (i == 0) | (te_ref[i] != te_ref[jnp.maximum(i - 1, 0)])

    @pl.when(valid & fresh)
    def _():
        gate_bf[...] = wg_ref[0].astype(BF16)
        up_bf[...] = wu_ref[0].astype(BF16)
        down_bf[...] = wd_ref[0].astype(BF16)

    @pl.when(valid)
    def _():
        x = x_ref[...].astype(BF16)
        act = _silu(_dot(x, gate_bf[...])) * _dot(x, up_bf[...])
        o_ref[...] = _dot(act.astype(BF16), down_bf[...]).astype(o_ref.dtype)

    @pl.when(jnp.logical_not(valid))
    def _():
        o_ref[...] = jnp.zeros_like(o_ref)


def _moe_experts(x_sorted, tile_expert, n_tiles_used, wg, wu, wd):
    tm = MOE_TILE
    rows, d = x_sorted.shape
    ff = wg.shape[2]
    grid_spec = pltpu.PrefetchScalarGridSpec(
        num_scalar_prefetch=2, grid=(rows // tm,),
        in_specs=[pl.BlockSpec((tm, d), lambda i, te, nt: (i, 0)),
                  pl.BlockSpec((1, d, ff), lambda i, te, nt: (te[i], 0, 0)),
                  pl.BlockSpec((1, d, ff), lambda i, te, nt: (te[i], 0, 0)),
                  pl.BlockSpec((1, ff, d), lambda i, te, nt: (te[i], 0, 0))],
        out_specs=pl.BlockSpec((tm, d), lambda i, te, nt: (i, 0)),
        scratch_shapes=[pltpu.VMEM((d, ff), BF16), pltpu.VMEM((d, ff), BF16), pltpu.VMEM((ff, d), BF16)])
    return pl.pallas_call(
        _moe_kernel, out_shape=jax.ShapeDtypeStruct((rows, d), F32), grid_spec=grid_spec,
        compiler_params=_params(1), name="moe_experts",
    )(tile_expert, n_tiles_used, x_sorted, wg, wu, wd)


def _combine_kernel(x_ref, y1_ref, y2_ref, rt_ref, mod_ref, o_ref):
    f = rt_ref[0, :, 2:3] * y1_ref[0] + rt_ref[0, :, 3:4] * y2_ref[0]
    o_ref[0] = x_ref[0] + mod_ref[0, 5:6, :] * f


def _combine(x, y1, y2, route, mod_l, ctx_tiles, ctx_row):
    bsz, s, d = x.shape
    tm = ROW_TILE
    tok = lambda b, i: (b, i, 0)
    mod_idx = lambda b, i: (jnp.where(i < ctx_tiles, ctx_row, b), 0, 0)
    return pl.pallas_call(
        _combine_kernel, out_shape=jax.ShapeDtypeStruct(x.shape, F32), grid=(bsz, s // tm),
        in_specs=[pl.BlockSpec((1, tm, d), tok), pl.BlockSpec((1, tm, d), tok), pl.BlockSpec((1, tm, d), tok),
                  pl.BlockSpec((1, tm, LANES), tok), pl.BlockSpec((1, 6, d), mod_idx)],
        out_specs=pl.BlockSpec((1, tm, d), tok),
        compiler_params=_params(2), name="moe_combine",
    )(x, y1.reshape(x.shape), y2.reshape(x.shape), route.reshape(bsz, s, LANES), mod_l)


def _hier_moe(h2, logits, layer, wg, wu, wd):
    bsz, s, d = h2.shape
    t = bsz * s
    tm = MOE_TILE
    n_tiles = 2 * t // tm + N_EXPERTS
    route, counts, src = _router(logits.reshape(t, LANES), n_tiles * tm)
    tiles_per = (counts[0, :N_EXPERTS].astype(jnp.int32) + tm - 1) // tm
    tile_end = jnp.cumsum(tiles_per)
    tile_ids = jnp.arange(n_tiles, dtype=jnp.int32)
    tile_expert = jnp.minimum(jnp.sum((tile_end[None, :] <= tile_ids[:, None]).astype(jnp.int32), axis=1),
                              N_EXPERTS - 1) + layer * N_EXPERTS
    pos = route[:, 6:8].astype(jnp.int32)
    x_sorted = jnp.take(h2.reshape(t, d), src.reshape(-1), axis=0, mode="clip")
    y_sorted = _moe_experts(x_sorted, tile_expert, tile_end[-1:], wg.reshape(-1, d, wg.shape[-1]),
                            wu.reshape(-1, d, wu.shape[-1]), wd.reshape(-1, wd.shape[-2], d))
    y1 = jnp.take(y_sorted, pos[:, 0], axis=0, mode="clip")
    y2 = jnp.take(y_sorted, pos[:, 1], axis=0, mode="clip")
    return y1, y2, route


def _attn_kernel(q_ref, k_ref, v_ref, bias_ref, qw_ref, kw_ref, o_ref, qn_ref, kn_ref, *, ctx_len, rows):
    lane = lax.broadcasted_iota(jnp.int32, (1, LANES), 1)
    first = lane < NA_HEAD_DIM

    r_head = lax.broadcasted_iota(jnp.int32, (LANES, LANES), 0) // NA_HEAD_DIM
    c_head = lax.broadcasted_iota(jnp.int32, (LANES, LANES), 1) // NA_HEAD_DIM
    same_head = jnp.where(r_head == c_head, 1.0, 0.0).astype(BF16)

    def head_norm(x, w):
        ms = _dot((x * x).astype(BF16), same_head) * (1.0 / NA_HEAD_DIM)
        return x * lax.rsqrt(ms + RMS_EPS) * w

    kn_ref[...] = head_norm(k_ref[0].astype(F32), kw_ref[...]).astype(BF16)
    qn_ref[...] = (head_norm(q_ref[0, ctx_len:, :].astype(F32), qw_ref[...]) * NA_HEAD_DIM ** -0.5).astype(BF16)
    n_groups = rows // ATTN_GROUP_ROWS
    n_q = ATTN_GROUP_ROWS * GRID_W
    n_loc = ATTN_KEY_ROWS * GRID_W
    k_ctx = kn_ref[0:ctx_len, :]
    v_ctx = v_ref[0, 0:ctx_len, :]

    def one_group(g):
        kind = jnp.where(g == 0, 0, jnp.where(g == n_groups - 1, 2, 1))
        kr0 = jnp.clip(g * ATTN_GROUP_ROWS - NA_KH // 2, 0, rows - ATTN_KEY_ROWS)
        q_rows = pl.ds(pl.multiple_of(g * n_q, n_q), n_q)
        q = qn_ref[q_rows, :]
        zero = jnp.zeros_like(q)
        q2 = jnp.concatenate([jnp.where(first, q, zero), jnp.where(first, zero, q)], axis=0)
        k_off = pl.multiple_of(ctx_len + kr0 * GRID_W, GRID_W)
        s_loc = _dot_nt(q2, kn_ref[pl.ds(k_off, n_loc), :])
        s_ctx = _dot_nt(q2, k_ctx)
        p_loc, p_ctx, inv = [], [], []
        for hh in range(2):
            sl = s_loc[hh * n_q:(hh + 1) * n_q] + bias_ref[hh, kind]
            sc = s_ctx[hh * n_q:(hh + 1) * n_q]
            m = jnp.maximum(jnp.max(sl, axis=-1, keepdims=True), jnp.max(sc, axis=-1, keepdims=True))
            el = jnp.exp(sl - m)
            ec = jnp.exp(sc - m)
            inv.append(1.0 / (jnp.sum(el, axis=-1, keepdims=True) + jnp.sum(ec, axis=-1, keepdims=True)))
            p_loc.append(el.astype(BF16))
            p_ctx.append(ec.astype(BF16))
        o = (_dot(jnp.concatenate(p_loc, axis=0), v_ref[0, pl.ds(k_off, n_loc), :])
             + _dot(jnp.concatenate(p_ctx, axis=0), v_ctx))
        o_ref[0, q_rows, :] = jnp.where(first, o[:n_q] * inv[0], o[n_q:] * inv[1]).astype(o_ref.dtype)

    def body(i, carry):
        for j in range(ATTN_GROUPS_PER_TRIP):
            one_group(i * ATTN_GROUPS_PER_TRIP + j)
        return carry

    lax.fori_loop(0, n_groups // ATTN_GROUPS_PER_TRIP, body, 0)


def _attn_group_layout(rows):
    n_groups = rows // ATTN_GROUP_ROWS
    assert rows % ATTN_GROUP_ROWS == 0 and rows >= ATTN_KEY_ROWS and n_groups >= 2
    u = np.arange(ATTN_GROUP_ROWS)[:, None]
    i = np.arange(ATTN_KEY_ROWS)[None, :]

    def layout(g):
        r = g * ATTN_GROUP_ROWS + u
        r0 = np.clip(r - NA_KH // 2, 0, rows - NA_KH)
        kr = np.clip(g * ATTN_GROUP_ROWS - NA_KH // 2, 0, rows - ATTN_KEY_ROWS) + i
        return (kr >= r0) & (kr < r0 + NA_KH), kr - r + NA_KH - 1

    kinds = [layout(0), layout(1), layout(n_groups - 1)]
    for g in range(1, n_groups - 1):
        valid, d = layout(g)
        assert (valid == kinds[1][0]).all() and (d[valid] == kinds[1][1][valid]).all()
    return np.stack([k[0] for k in kinds]), np.stack([k[1] for k in kinds])


def _bias_windows(rpb, rows):
    qc = np.arange(GRID_W)
    c0 = np.clip(qc - NA_KW // 2, 0, GRID_W - NA_KW)
    kc = np.arange(GRID_W)
    inwin = (kc[None, :] >= c0[:, None]) & (kc[None, :] < c0[:, None] + NA_KW)
    coff = kc[None, :] - qc[:, None] + NA_KW - 1
    pick = (coff[..., None] == np.arange(2 * NA_KW - 1)) & inwin[..., None]
    tab = jnp.einsum("hdo,qko->dhqk", rpb.astype(F32), jnp.asarray(pick, F32), precision=HIGHEST)
    tab = jnp.where(inwin[None, None], tab, NEG)
    valid, d = _attn_group_layout(rows)
    masked = jnp.full(tab.shape[1:], NEG, F32)
    blocks = []
    for kind in range(valid.shape[0]):
        for u in range(ATTN_GROUP_ROWS):
            blocks.append(jnp.concatenate(
                [tab[int(d[kind, u, i])] if valid[kind, u, i] else masked for i in range(ATTN_KEY_ROWS)], axis=-1))
    win = jnp.stack(blocks, axis=1)
    return win.reshape(rpb.shape[0], valid.shape[0], ATTN_GROUP_ROWS * GRID_W, ATTN_KEY_ROWS * GRID_W)


def _neighbourhood_attention(q, k, v, bias, qn_w, kn_w, ctx_len):
    bsz, s, w = q.shape
    seq = s - ctx_len
    rows = seq // GRID_W
    n_pairs = NA_HEADS // 2
    pair = lambda b, p: (b, 0, p)
    row2 = lambda b, p: (0, 0)
    return pl.pallas_call(
        functools.partial(_attn_kernel, ctx_len=ctx_len, rows=rows),
        out_shape=jax.ShapeDtypeStruct((bsz, seq, w), BF16),
        grid=(bsz, n_pairs),
        in_specs=[pl.BlockSpec((1, s, LANES), pair), pl.BlockSpec((1, s, LANES), pair),
                  pl.BlockSpec((1, s, LANES), pair),
                  pl.BlockSpec((2,) + bias.shape[1:], lambda b, p: (p, 0, 0, 0)),
                  pl.BlockSpec((1, LANES), row2), pl.BlockSpec((1, LANES), row2)],
        out_specs=pl.BlockSpec((1, seq, LANES), pair),
        scratch_shapes=[pltpu.VMEM((seq, LANES), BF16), pltpu.VMEM((s, LANES), BF16)],
        compiler_params=_params(2), name="neighbourhood_attention",
    )(q, k, v, bias, jnp.tile(qn_w, 2).reshape(1, LANES), jnp.tile(kn_w, 2).reshape(1, LANES))


def _pad_row(pieces, width=LANES):
    row = jnp.zeros((width,), F32)
    for off, vec in pieces:
        row = row.at[off:off + vec.shape[0]].set(vec.astype(F32))
    return row.reshape(1, width)


def _router_params(w_group, b_group, w_expert, b_expert):
    d = w_group.shape[0]
    w = jnp.zeros((d, LANES), F32).at[:, :N_EXPERTS].set(w_expert).at[:, ROUTE_G:ROUTE_G + MOE_GROUPS].set(w_group)
    return _split_weight(w), _pad_row([(0, b_expert), (ROUTE_G, b_group)])


def kernel(x, c, ctx, c_ctx, norm1_w, norm2_w, mod_w, mod_b, ab_w_in, ab_conv_w, ab_conv_b, ssd_a_log, ssd_dt_bias, ssd_d, ssd_norm_w, ml_i_bias, ml_f_bias, ml_norm_w, ab_w_out, na_w_qkv, na_q_norm, na_k_norm, na_rpb, na_w_out, moe_w_group, moe_b_group, moe_w_expert, moe_b_expert, moe_w_gate, moe_w_up, moe_w_down):
    bsz, seq, d = x.shape
    ctx_len = ctx.shape[1]
    depth = mod_w.shape[0]
    assert depth == 2 and ctx_len % ROW_TILE == 0 and seq % ROW_TILE == 0 and bsz < 8
    ctx_tiles = ctx_len // ROW_TILE
    lat_tiles = seq // ROW_TILE
    ctx_row = bsz

    cvec = jnp.zeros((8, d), F32).at[:bsz].set(c).at[bsz].set(c_ctx)
    mod = _mod_vectors(cvec, mod_w, mod_b)
    xs = (ctx, x)

    ssd_w = SSD_HEADS * SSD_HEAD_DIM
    xbc_w = ssd_w + 2 * SSD_GROUPS * SSD_STATE
    qk_w, v_w = ML_HEADS * ML_DK, ML_HEADS * ML_DV
    sizes = (ssd_w, xbc_w, 2 * SSD_HEADS, qk_w, qk_w, v_w, v_w, 2 * ML_HEADS, 2 * ML_HEADS)
    w_z, w_xbc, w_dt, w_q, w_k, w_v, w_o, w_i, w_f = jnp.split(ab_w_in[0], np.cumsum(sizes)[:-1].tolist(), axis=1)
    w_gate = jnp.zeros((d, LANES), F32).at[:, :GATE_F + 2 * ML_HEADS].set(jnp.concatenate([w_dt, w_i, w_f], axis=1))
    weights = [w.astype(BF16) for w in (w_z, w_xbc, w_q, w_k, w_v, w_o)] + [_split_weight(w_gate)]
    z, xbc, q, k, v, og, gates = _norm_mod_matmul(xs, norm1_w[0], mod[0], weights, [BF16] * 6 + [F32],
                                                  ctx_tiles, ctx_row)
    xbc = _conv_silu(xbc, ab_conv_w[0], ab_conv_b[0], ctx_len)
    a_neg = -jnp.exp(ssd_a_log[0].astype(F32))
    dsk_row = jnp.repeat(ssd_d[0].astype(F32), SSD_HEAD_DIM).reshape(1, ssd_w)
    y = None
    hm = None
    for dr in range(2):
        rev = dr == 1
        dtb_row = _pad_row([(GATE_DT + dr * SSD_HEADS, ssd_dt_bias[0, dr])])
        a_row = _pad_row([(GATE_DT + dr * SSD_HEADS, a_neg[dr])])
        y = _ssd_scan(xbc, gates, dtb_row, a_row, y if rev else dsk_row, rev=rev, ctx_len=ctx_len)
        ib_row = _pad_row([(GATE_I + dr * ML_HEADS, ml_i_bias[0, dr])])
        fb_row = _pad_row([(GATE_F + dr * ML_HEADS, ml_f_bias[0, dr])])
        hm = _mlstm_scan(q, k, v, gates, ib_row, fb_row, hm, rev=rev, ctx_len=ctx_len)
    w_r, b_r = _router_params(moe_w_group[0], moe_b_group[0], moe_w_expert[0], moe_b_expert[0])
    w_out = ab_w_out[0].astype(BF16)
    x1, h2, logits = _post_call(
        "mix", xs, [y, z, hm, og],
        [ssd_norm_w[0].reshape(1, ssd_w), ml_norm_w[0].reshape(1, v_w)], [w_out[:ssd_w], w_out[ssd_w:]],
        mod[0], norm2_w[0], w_r, b_r, 0, ctx_tiles + lat_tiles, ctx_tiles, ctx_row)
    moe0 = _hier_moe(h2, logits, 0, moe_w_gate, moe_w_up, moe_w_down)

    w_qkv = na_w_qkv[0].astype(BF16)
    na_w = NA_HEADS * NA_HEAD_DIM
    q, k, v, xs = _norm_mod_matmul(x1, norm1_w[1], mod[1],
                                   [w_qkv[:, :na_w], w_qkv[:, na_w:2 * na_w], w_qkv[:, 2 * na_w:]],
                                   [BF16] * 3, ctx_tiles, ctx_row, pending_moe=(*moe0, mod[0]))
    bias = _bias_windows(na_rpb[0], seq // GRID_W)
    attn = _neighbourhood_attention(q, k, v, bias, na_q_norm[0], na_k_norm[0], ctx_len)
    w_r, b_r = _router_params(moe_w_group[1], moe_b_group[1], moe_w_expert[1], moe_b_expert[1])
    x1, h2, logits = _post_call("attn", xs, [attn], [], [na_w_out[0].astype(BF16)],
                                mod[1], norm2_w[1], w_r, b_r, ctx_tiles, lat_tiles, ctx_tiles, ctx_row)
    y1, y2, route = _hier_moe(h2, logits, 1, moe_w_gate, moe_w_up, moe_w_down)
    return _combine(x1, y1, y2, route, mod[1], 0, ctx_row)
```

```python
import functools

import numpy as np
import jax
import jax.numpy as jnp
from jax import lax
from jax.experimental import pallas as pl
from jax.experimental.pallas import tpu as pltpu

F32 = jnp.float32
BF16 = jnp.bfloat16
HIGHEST = lax.Precision.HIGHEST

RMS_EPS = 1e-6
GRID_W = 64
SSD_HEADS = 16
SSD_HEAD_DIM = 64
SSD_GROUPS = 2
SSD_STATE = 128
ML_HEADS = 4
ML_DK = 128
ML_DV = 256
NA_HEADS = 16
NA_HEAD_DIM = 64
NA_KH = 8
NA_KW = 16
MOE_GROUPS = 4
MOE_EXPERTS = 8
N_EXPERTS = MOE_GROUPS * MOE_EXPERTS

LANES = 128
ROW_TILE = 256
SCAN_CHUNK = 128
SSD_BLOCK_CHUNKS = 2
ML_CHUNK = 256
MOE_TILE = 256
ATTN_GROUP_ROWS = 4
ATTN_KEY_ROWS = ATTN_GROUP_ROWS + NA_KH - 1
ATTN_GROUPS_PER_TRIP = 4
VMEM_LIMIT = 56 * 1024 * 1024

GATE_DT = 0
GATE_I = 2 * SSD_HEADS
GATE_F = GATE_I + 2 * ML_HEADS
ROUTE_G = N_EXPERTS
NEG = -1e30


def _params(n_axes):
    return pltpu.CompilerParams(dimension_semantics=("arbitrary",) * n_axes,
                                vmem_limit_bytes=VMEM_LIMIT)


def _silu(x):
    return x * jax.nn.sigmoid(x)


def _softplus(x):
    return jnp.maximum(x, 0.0) + jnp.log1p(jnp.exp(-jnp.abs(x)))


def _rms(x, w):
    return x * lax.rsqrt(jnp.mean(x * x, axis=-1, keepdims=True) + RMS_EPS) * w


def _dot(a, b):
    return jnp.dot(a, b, preferred_element_type=F32)


def _dot_nt(a, b):
    return lax.dot_general(a, b, (((1,), (1,)), ((), ())), preferred_element_type=F32)


def _dot_hi(a, b):
    return jnp.dot(a, b, precision=HIGHEST, preferred_element_type=F32)


def _split_bf16(x, terms):
    parts = []
    for _ in range(terms - 1):
        p = x.astype(BF16)
        parts.append(p)
        x = x - p.astype(F32)
    parts.append(x.astype(BF16))
    return parts


def _split_weight(w):
    return jnp.stack(_split_bf16(w.astype(F32), 2))


def _dot_split(a, w2_ref):
    a_hi, a_lo = _split_bf16(a, 2)
    return _dot(a_hi, w2_ref[0]) + _dot(a_lo, w2_ref[0]) + _dot(a_hi, w2_ref[1])


def _cumsum_dot(tri, x):
    tri = jnp.where(tri, 1.0, 0.0).astype(BF16)
    hi, mid, lo = _split_bf16(x, 3)
    return _dot(tri, hi) + _dot(tri, mid) + _dot(tri, lo)


def _mod_kernel(c_ref, w_ref, b_ref, o_ref):
    o_ref[0] = _dot_hi(_silu(c_ref[...]), w_ref[0]) + b_ref[0]


def _mod_vectors(cvec, mod_w, mod_b, tn=512):
    depth, d, n = mod_w.shape
    rows = cvec.shape[0]
    out = pl.pallas_call(
        _mod_kernel,
        out_shape=jax.ShapeDtypeStruct((depth, rows, n), F32),
        grid=(depth, n // tn),
        in_specs=[pl.BlockSpec((rows, d), lambda l, j: (0, 0)),
                  pl.BlockSpec((1, d, tn), lambda l, j: (l, 0, j)),
                  pl.BlockSpec((1, 1, tn), lambda l, j: (l, 0, j))],
        out_specs=pl.BlockSpec((1, rows, tn), lambda l, j: (l, 0, j)),
        compiler_params=_params(2),
        name="mod_vectors",
    )(cvec, mod_w, mod_b.reshape(depth, 1, n))
    return out.reshape(depth, rows, 6, d)


def _stream_tile(refs, split_tiles, tile):
    if not split_tiles:
        return refs[0][0], refs[1:]
    return jnp.where(tile < split_tiles, refs[0][0], refs[1][0]), refs[2:]


def _stream_specs(xs, tm, offset=0):
    if not isinstance(xs, tuple):
        return [xs], [pl.BlockSpec((1, tm, xs.shape[2]), lambda b, i: (b, i + offset, 0))], 0
    ctx, lat = xs
    split = ctx.shape[1] // tm
    d = ctx.shape[2]
    return ([ctx, lat],
            [pl.BlockSpec((1, tm, d), lambda b, i: (b, jnp.minimum(i + offset, split - 1), 0)),
             pl.BlockSpec((1, tm, d), lambda b, i: (b, jnp.maximum(i + offset - split, 0), 0))], split)


def _nmm_kernel(*refs, n_out, pending_moe, split_tiles):
    x, refs = _stream_tile(refs, split_tiles, pl.program_id(1))
    if pending_moe:
        y1_ref, y2_ref, rt_ref, pmod_ref = refs[:4]
        refs = refs[4:]
        x = x + pmod_ref[0, 5:6, :] * (rt_ref[0, :, 2:3] * y1_ref[0] + rt_ref[0, :, 3:4] * y2_ref[0])
        refs[-1][0] = x
        refs = refs[:-1]
    nw_ref, mod_ref = refs[:2]
    refs = refs[2:]
    w_refs, o_refs = refs[:n_out], refs[n_out:]
    h = _rms(x, nw_ref[...])
    h = h * (1.0 + mod_ref[0, 1:2, :]) + mod_ref[0, 0:1, :]
    hb = h.astype(BF16)
    for w_ref, o_ref in zip(w_refs, o_refs):
        if len(w_ref.shape) == 3:
            o_ref[0] = _dot_split(h, w_ref)
        else:
            o_ref[0] = _dot(hb, w_ref[...]).astype(o_ref.dtype)


def _norm_mod_matmul(xs, norm_w, mod_l, weights, out_dtypes, ctx_tiles, ctx_row, pending_moe=None):
    tm = ROW_TILE
    args, in_specs, split_tiles = _stream_specs(xs, tm)
    bsz, d = args[0].shape[0], args[0].shape[2]
    s = sum(a.shape[1] for a in args)
    tok = lambda b, i: (b, i, 0)
    mod_idx = lambda b, i: (jnp.where(i < ctx_tiles, ctx_row, b), 0, 0)
    if pending_moe is not None:
        y1, y2, route, mod_prev = pending_moe
        args += [y1.reshape(bsz, s, d), y2.reshape(bsz, s, d), route.reshape(bsz, s, LANES), mod_prev]
        in_specs += [pl.BlockSpec((1, tm, d), tok), pl.BlockSpec((1, tm, d), tok),
                     pl.BlockSpec((1, tm, LANES), tok), pl.BlockSpec((1, 6, d), mod_idx)]
    args += [norm_w.reshape(1, d), mod_l, *weights]
    in_specs += [pl.BlockSpec((1, d), lambda b, i: (0, 0)), pl.BlockSpec((1, 6, d), mod_idx)]
    in_specs += [pl.BlockSpec(w.shape, lambda b, i, nd=w.ndim: (0,) * nd) for w in weights]
    out_shape = [jax.ShapeDtypeStruct((bsz, s, w.shape[-1]), dt) for w, dt in zip(weights, out_dtypes)]
    out_specs = [pl.BlockSpec((1, tm, w.shape[-1]), tok) for w in weights]
    if pending_moe is not None:
        out_shape.append(jax.ShapeDtypeStruct((bsz, s, d), F32))
        out_specs.append(pl.BlockSpec((1, tm, d), tok))
    return pl.pallas_call(
        functools.partial(_nmm_kernel, n_out=len(weights), pending_moe=pending_moe is not None,
                          split_tiles=split_tiles),
        out_shape=out_shape, grid=(bsz, s // tm), in_specs=in_specs, out_specs=out_specs,
        compiler_params=_params(2), name="norm_mod_matmul",
    )(*args)


def _conv_kernel(u_ref, w_ref, b_ref, o_ref, *, ctx_len):
    u = u_ref[0].astype(F32)
    s = u.shape[0]
    t = lax.broadcasted_iota(jnp.int32, u.shape, 0)
    prev = jnp.where((t == 0) | (t == ctx_len), 0.0, pltpu.roll(u, 1, axis=0))
    nxt = jnp.where((t == ctx_len - 1) | (t == s - 1), 0.0, pltpu.roll(u, s - 1, axis=0))
    y = prev * w_ref[0:1, :] + u * w_ref[1:2, :] + nxt * w_ref[2:3, :] + b_ref[...]
    o_ref[0] = _silu(y).astype(o_ref.dtype)


def _conv_silu(u, conv_w, conv_b, ctx_len, tc=2 * LANES):
    bsz, s, ch = u.shape
    return pl.pallas_call(
        functools.partial(_conv_kernel, ctx_len=ctx_len),
        out_shape=jax.ShapeDtypeStruct(u.shape, u.dtype),
        grid=(bsz, ch // tc),
        in_specs=[pl.BlockSpec((1, s, tc), lambda b, j: (b, 0, j)),
                  pl.BlockSpec((3, tc), lambda b, j: (0, j)),
                  pl.BlockSpec((1, tc), lambda b, j: (0, j))],
        out_specs=pl.BlockSpec((1, s, tc), lambda b, j: (b, 0, j)),
        compiler_params=_params(2), name="conv_silu",
    )(u, conv_w, conv_b.reshape(1, ch))


def _scan_chunk_index(c, rev, n_ctx, n_all):
    if not rev:
        return c
    return jnp.where(c < n_ctx, n_ctx - 1 - c, n_ctx + n_all - 1 - c)


def _tri(n, rev):
    row = lax.broadcasted_iota(jnp.int32, (n, n), 0)
    col = lax.broadcasted_iota(jnp.int32, (n, n), 1)
    return (col >= row) if rev else (col <= row)


def _ssd_kernel(*refs, rev, off):
    if rev:
        xs_ref, bc_ref, g_ref, dtb_ref, a_ref, ex_ref, acc_ref, o_ref, st_ref = refs
    else:
        xs_ref, bc_ref, g_ref, dtb_ref, a_ref, ex_ref, dsk_ref, o_ref, st_ref = refs

    @pl.when(pl.program_id(1) == 0)
    def _():
        st_ref[...] = jnp.zeros_like(st_ref)

    n = SCAN_CHUNK
    n_sub = xs_ref.shape[1] // n
    for sub in (range(n_sub - 1, -1, -1) if rev else range(n_sub)):
        _ssd_chunk(refs, slice(sub * n, (sub + 1) * n), rev, off)


def _ssd_chunk(refs, rows, rev, off):
    if rev:
        xs_ref, bc_ref, g_ref, dtb_ref, a_ref, ex_ref, acc_ref, o_ref, st_ref = refs
    else:
        xs_ref, bc_ref, g_ref, dtb_ref, a_ref, ex_ref, dsk_ref, o_ref, st_ref = refs
    n = SCAN_CHUNK
    last = 0 if rev else n - 1
    tri = _tri(n, rev)
    dt = _softplus(g_ref[0, rows, :] + dtb_ref[...])
    log_a = dt * a_ref[...]
    cs = _cumsum_dot(tri, log_a)
    cs_t = cs.T
    dt_hi = dt.astype(BF16)
    dt_lo = (dt - dt_hi.astype(F32)).astype(BF16)
    dt_full = _dot(dt_hi, ex_ref[...]) + _dot(dt_lo, ex_ref[...])
    xs = xs_ref[0, rows, :].astype(F32)
    xdt = (xs * dt_full).astype(BF16)
    lo_half = lax.broadcasted_iota(jnp.int32, (1, LANES), 1) < SSD_HEAD_DIM
    gw = SSD_GROUPS * SSD_STATE
    heads_per_group = SSD_HEADS // SSD_GROUPS
    for g in range(SSD_GROUPS):
        b_g = bc_ref[0, rows, g * SSD_STATE:(g + 1) * SSD_STATE]
        c_g = bc_ref[0, rows, gw + g * SSD_STATE:gw + (g + 1) * SSD_STATE]
        cb = _dot_nt(c_g, b_g)
        b_t = b_g.astype(F32).T
        for e in range(0, heads_per_group, 2):
            h0 = g * heads_per_group + e
            pair = h0 // 2
            sl = slice(pair * LANES, (pair + 1) * LANES)
            x_pair = xdt[:, sl]
            zero = jnp.zeros_like(x_pair)
            y, upd, a_bc, tots = None, None, [], []
            for j in range(2):
                col = off + h0 + j
                x_j = jnp.where(lo_half, x_pair, zero) if j == 0 else jnp.where(lo_half, zero, x_pair)
                a_b = jnp.broadcast_to(cs[:, col:col + 1], (n, LANES))
                a_row = cs_t[col:col + 1, :]
                tot = cs_t[col:col + 1, last:last + 1]
                decay = jnp.exp(jnp.where(tri, a_b - a_row, -jnp.inf))
                y_j = _dot((cb * decay).astype(BF16), x_j)
                upd_j = _dot((b_t * jnp.exp(tot - a_row)).astype(BF16), x_j)
                y = y_j if y is None else y + y_j
                upd = upd_j if upd is None else upd + upd_j
                a_bc.append(a_b)
                tots.append(tot)
            state = st_ref[pair]
            y = y + _dot(c_g, state.astype(BF16)) * jnp.exp(jnp.where(lo_half, a_bc[0], a_bc[1]))
            st_ref[pair] = state * jnp.exp(jnp.where(lo_half, tots[0], tots[1])) + upd
            if rev:
                y = y + acc_ref[0, rows, sl].astype(F32)
            else:
                y = y + dsk_ref[:, sl] * xs[:, sl]
            o_ref[0, rows, sl] = y.astype(o_ref.dtype)


def _ssd_scan(xbc, gates, dtb_row, a_row, extra, *, rev, ctx_len):
    bsz, s, _ = xbc.shape
    n = SCAN_CHUNK * SSD_BLOCK_CHUNKS
    n_all = s // n
    n_ctx = ctx_len // n
    w = SSD_HEADS * SSD_HEAD_DIM
    bcw = 2 * SSD_GROUPS * SSD_STATE
    assert SCAN_CHUNK == LANES and 2 * SSD_HEAD_DIM == LANES and (SSD_HEADS // SSD_GROUPS) % 2 == 0
    assert ctx_len % n == 0 and s % n == 0
    off = GATE_DT + (SSD_HEADS if rev else 0)
    cidx = functools.partial(_scan_chunk_index, rev=rev, n_ctx=n_ctx, n_all=n_all)
    tok = lambda b, c: (b, cidx(c), 0)
    const = lambda b, c: (0, 0)
    expand = jnp.asarray(np.arange(LANES)[:, None] == off + np.arange(w)[None, :] // SSD_HEAD_DIM, BF16)
    in_specs = [pl.BlockSpec((1, n, w), tok),
                pl.BlockSpec((1, n, bcw), lambda b, c: (b, cidx(c), w // bcw)),
                pl.BlockSpec((1, n, LANES), tok),
                pl.BlockSpec((1, LANES), const),
                pl.BlockSpec((1, LANES), const),
                pl.BlockSpec((LANES, w), const)]
    if rev:
        in_specs.append(pl.BlockSpec((1, n, w), tok))
    else:
        in_specs.append(pl.BlockSpec((1, w), const))
    return pl.pallas_call(
        functools.partial(_ssd_kernel, rev=rev, off=off),
        out_shape=jax.ShapeDtypeStruct((bsz, s, w), BF16),
        grid=(bsz, n_all), in_specs=in_specs,
        out_specs=pl.BlockSpec((1, n, w), tok),
        scratch_shapes=[pltpu.VMEM((SSD_HEADS // 2, SSD_STATE, LANES), F32)],
        compiler_params=_params(2), name="ssd_scan_bwd" if rev else "ssd_scan_fwd",
    )(xbc, xbc, gates, dtb_row, a_row, expand, extra)


def _mlstm_kernel(*refs, rev, d):
    if rev:
        q_ref, k_ref, v_ref, g_ref, ib_ref, fb_ref, acc_ref, o_ref, c_st, n_st, m_st = refs
    else:
        q_ref, k_ref, v_ref, g_ref, ib_ref, fb_ref, o_ref, c_st, n_st, m_st = refs

    @pl.when(pl.program_id(1) == 0)
    def _():
        c_st[...] = jnp.zeros_like(c_st)
        n_st[...] = jnp.zeros_like(n_st)
        m_st[...] = jnp.zeros_like(m_st)

    n = q_ref.shape[1]
    last = 0 if rev else n - 1
    tri = _tri(n, rev)
    g = g_ref[0]
    log_i = g + ib_ref[...]
    log_f = -_softplus(-(g + fb_ref[...]))
    cs = _cumsum_dot(tri, log_f)
    cs_t = cs.T
    li_t = log_i.T
    for h in range(ML_HEADS):
        ci = GATE_I + ML_HEADS * d + h
        cf = GATE_F + ML_HEADS * d + h
        b_col = cs[:, cf:cf + 1]
        off_row = cs_t[cf:cf + 1, :] - li_t[ci:ci + 1, :]
        tot = cs_t[cf:cf + 1, last:last + 1]
        m_prev = m_st[h, 0:1, 0:1]
        pmax = jnp.max(jnp.where(tri, -off_row, -jnp.inf), axis=-1, keepdims=True)
        u_b = jnp.broadcast_to(-jnp.maximum(pmax, m_prev), (n, LANES))
        b_b = jnp.broadcast_to(b_col, (n, LANES))
        w_inter = jnp.exp(m_prev + u_b)
        a_end = tot - off_row
        m_loc = jnp.max(a_end, axis=-1, keepdims=True)
        w_end = jnp.exp(a_end - m_loc)
        qh = q_ref[0, :, h * ML_DK:(h + 1) * ML_DK]
        kf = k_ref[0, :, h * ML_DK:(h + 1) * ML_DK].astype(F32) * (ML_DK ** -0.5)
        kb = kf.astype(BF16)
        vh = v_ref[0, :, h * ML_DV:(h + 1) * ML_DV]
        s_mat = _dot_nt(qh, kb) * jnp.exp(jnp.where(tri, jnp.tile(u_b, (1, n // LANES)) - off_row, -jnp.inf))
        s_hi, s_lo = _split_bf16(s_mat, 2)
        c_prev = c_st[h]
        n_prev = n_st[h]
        ones = jnp.ones((n, LANES), BF16)
        den = _dot(s_hi, ones) + _dot(s_lo, ones) + _dot(qh, n_prev.astype(BF16)) * w_inter
        inv = 1.0 / jnp.maximum(jnp.abs(den), jnp.exp(u_b - b_b))
        reps = ML_DV // LANES
        out = (_dot(s_hi, vh) * jnp.tile(inv, (1, reps))
               + _dot(qh, c_prev.astype(BF16)) * jnp.tile(w_inter * inv, (1, reps)))
        kw_t = (kf.T * w_end).astype(BF16)
        c_chunk = _dot(kw_t, vh)
        n_chunk = _dot(kw_t, ones)
        m_new = jnp.maximum(tot + m_prev, m_loc)
        a_sc = jnp.exp(tot + m_prev - m_new)
        b_sc = jnp.exp(m_loc - m_new)
        c_st[h] = c_prev * a_sc + c_chunk * b_sc
        n_st[h] = n_prev * a_sc + n_chunk * b_sc
        m_st[h] = jnp.broadcast_to(m_new, m_st.shape[1:])
        if rev:
            out = out + acc_ref[0, :, h * ML_DV:(h + 1) * ML_DV].astype(F32)
        o_ref[0, :, h * ML_DV:(h + 1) * ML_DV] = out.astype(o_ref.dtype)


def _mlstm_scan(q, k, v, gates, ib_row, fb_row, acc, *, rev, ctx_len):
    bsz, s, _ = q.shape
    n = ML_CHUNK
    n_all = s // n
    n_ctx = ctx_len // n
    cidx = functools.partial(_scan_chunk_index, rev=rev, n_ctx=n_ctx, n_all=n_all)
    tok = lambda b, c: (b, cidx(c), 0)
    qw, vw = ML_HEADS * ML_DK, ML_HEADS * ML_DV
    in_specs = [pl.BlockSpec((1, n, qw), tok), pl.BlockSpec((1, n, qw), tok),
                pl.BlockSpec((1, n, vw), tok), pl.BlockSpec((1, n, LANES), tok),
                pl.BlockSpec((1, LANES), lambda b, c: (0, 0)),
                pl.BlockSpec((1, LANES), lambda b, c: (0, 0))]
    args = [q, k, v, gates, ib_row, fb_row]
    if rev:
        in_specs.append(pl.BlockSpec((1, n, vw), tok))
        args.append(acc)
    return pl.pallas_call(
        functools.partial(_mlstm_kernel, rev=rev, d=1 if rev else 0),
        out_shape=jax.ShapeDtypeStruct((bsz, s, vw), F32),
        grid=(bsz, n_all), in_specs=in_specs,
        out_specs=pl.BlockSpec((1, n, vw), tok),
        scratch_shapes=[pltpu.VMEM((ML_HEADS, ML_DK, ML_DV), F32),
                        pltpu.VMEM((ML_HEADS, ML_DK, LANES), F32),
                        pltpu.VMEM((ML_HEADS, 8, LANES), F32)],
        compiler_params=_params(2), name="mlstm_scan_bwd" if rev else "mlstm_scan_fwd",
    )(*args)


def _post_kernel(*refs, mode, split_tiles, x_tile_off, n_tiles, n_steps):
    k = pl.program_id(0)
    tile = jnp.minimum(k, n_steps - 1) % n_tiles
    x, refs = _stream_tile(refs, split_tiles, tile + x_tile_off)
    lg_sc = refs[-1]
    refs = refs[:-1]
    cnt_ref = refs[-1]

    @pl.when(k == 0)
    def _():
        lg_sc[...] = jnp.zeros_like(lg_sc)
        cnt_ref[...] = jnp.zeros_like(cnt_ref)

    lg_prev = lg_sc[...]
    if mode == "mix":
        (y_ref, z_ref, hm_ref, og_ref, snw_ref, mnw_ref, wa_ref, wb_ref,
         mod_ref, n2_ref, wr_ref, br_ref, xo_ref, h2_ref, rf_ref, cnt_ref) = refs
        y = _rms(y_ref[0].astype(F32) * _silu(z_ref[0].astype(F32)), snw_ref[...])
        og = og_ref[0].astype(F32)
        o = _dot(y.astype(BF16), wa_ref[...])
        parts = []
        for h in range(ML_HEADS):
            sl = slice(h * ML_DV, (h + 1) * ML_DV)
            parts.append(_rms(hm_ref[0, :, sl], mnw_ref[:, sl]) * jax.nn.sigmoid(og[:, sl]))
        o = o + _dot(jnp.concatenate(parts, axis=-1).astype(BF16), wb_ref[...])
    else:
        (a_ref, wa_ref, mod_ref, n2_ref, wr_ref, br_ref, xo_ref, h2_ref, rf_ref, cnt_ref) = refs
        o = _dot(a_ref[0], wa_ref[...])
    x_new = x + mod_ref[0, 2:3, :] * o
    xo_ref[0] = x_new
    h2 = _rms(x_new, n2_ref[...]) * (1.0 + mod_ref[0, 4:5, :]) + mod_ref[0, 3:4, :]
    h2_ref[0] = h2
    lg_sc[...] = _dot_split(h2, wr_ref) + br_ref[...]
    rf_ref[0] = _route_fields(lg_prev, cnt_ref, jnp.where(k > 0, 1.0, 0.0))


def _post_call(mode, x, acts, rows, mats, mod_l, norm2_w, w_router, b_router, x_tile_off, n_tiles, ctx_tiles, ctx_row):
    tm = ROW_TILE
    x_args, in_specs, split_tiles = _stream_specs(x, tm, x_tile_off)
    bsz, d = x_args[0].shape[0], x_args[0].shape[2]
    s_out = n_tiles * tm
    tok = lambda b, i: (b, i, 0)
    const = lambda b, i: (0, 0)
    mod_idx = lambda b, i: (jnp.where(i + x_tile_off < ctx_tiles, ctx_row, b), 0, 0)
    in_specs += [pl.BlockSpec((1, tm, a.shape[2]), tok) for a in acts]
    in_specs += [pl.BlockSpec(r.shape, const) for r in rows]
    in_specs += [pl.BlockSpec(m.shape, const) for m in mats]
    in_specs += [pl.BlockSpec((1, 6, d), mod_idx), pl.BlockSpec((1, d), const),
                 pl.BlockSpec(w_router.shape, lambda b, i: (0, 0, 0)), pl.BlockSpec((1, LANES), const)]
    out_shape = [jax.ShapeDtypeStruct((bsz, s_out, d), F32),
                 jax.ShapeDtypeStruct((bsz, s_out, d), F32),
                 jax.ShapeDtypeStruct((bsz, s_out, LANES), F32),
                 jax.ShapeDtypeStruct((8, LANES), F32)]
    out_specs = [pl.BlockSpec((1, tm, d), tok), pl.BlockSpec((1, tm, d), tok),
                 pl.BlockSpec((1, tm, LANES), tok), pl.BlockSpec((8, LANES), const)]
    n_steps = bsz * n_tiles

    def flat(index_map, lag=0):
        def at(k):
            kk = jnp.clip(k - lag, 0, n_steps - 1)
            return index_map(kk // n_tiles, kk % n_tiles)
        return at

    in_specs = [pl.BlockSpec(sp.block_shape, flat(sp.index_map)) for sp in in_specs]
    out_specs = [pl.BlockSpec(sp.block_shape, flat(sp.index_map, lag=int(j == 2))) for j, sp in enumerate(out_specs)]
    return pl.pallas_call(
        functools.partial(_post_kernel, mode=mode, split_tiles=split_tiles, x_tile_off=x_tile_off,
                          n_tiles=n_tiles, n_steps=n_steps),
        out_shape=out_shape, grid=(n_steps + 1,), in_specs=in_specs, out_specs=out_specs,
        scratch_shapes=[pltpu.VMEM((tm, LANES), F32)],
        compiler_params=_params(1), name="post_" + mode,
    )(*x_args, *acts, *rows, *mats, mod_l, norm2_w.reshape(1, d), w_router, b_router)


def _router_kernel(f_ref, cnt_ref, route_ref, src_ref, off_ref, inv_ref, *, moe_tile, n_tokens):
    i = pl.program_id(0)
    tm = f_ref.shape[0]

    @pl.when(i == 0)
    def _():
        tiles = jnp.ceil(cnt_ref[...] * (1.0 / moe_tile))
        r = lax.broadcasted_iota(jnp.int32, (LANES, LANES), 0)
        c = lax.broadcasted_iota(jnp.int32, (LANES, LANES), 1)
        earlier = jnp.where(r < c, 1.0, 0.0).astype(BF16)
        off_ref[...] = _dot(tiles.astype(BF16), earlier) * float(moe_tile)

    f = f_ref[...]
    lane = lax.broadcasted_iota(jnp.int32, f.shape, 1).astype(F32)
    off = off_ref[0:1, :]
    pos1 = jnp.sum(jnp.where(lane == f[:, 0:1], off, 0.0), axis=-1, keepdims=True) + f[:, 4:5]
    pos2 = jnp.sum(jnp.where(lane == f[:, 1:2], off, 0.0), axis=-1, keepdims=True) + f[:, 5:6]
    route_ref[...] = jnp.where(lane == 6.0, pos1, jnp.where(lane == 7.0, pos2, f))

    n_blk = inv_ref.shape[0]
    pos_t = jnp.where(lane == 0.0, pos1, jnp.where(lane == 1.0, pos2, 0.0)).T
    blk = lax.broadcasted_iota(jnp.int32, (n_blk, tm), 0).astype(F32)
    tok = (i * tm + lax.broadcasted_iota(jnp.int32, (tm, 1), 0)).astype(F32)
    tok_hi = jnp.floor(tok * (1.0 / LANES))
    tok_lo = tok - tok_hi * LANES
    lhs, rhs = [], []
    for k, pos in enumerate((pos1, pos2)):
        blk_of = jnp.floor(pos_t[k:k + 1, :] * (1.0 / LANES))
        lhs.append(jnp.where(blk == blk_of, 1.0, 0.0).astype(BF16))
        hit = lane == pos - jnp.floor(pos * (1.0 / LANES)) * LANES
        rhs.append(jnp.concatenate([jnp.where(hit, tok_hi, 0.0), jnp.where(hit, tok_lo, 0.0),
                                    jnp.where(hit, 1.0, 0.0)], axis=1).astype(BF16))
    upd = _dot(jnp.concatenate(lhs, axis=1), jnp.concatenate(rhs, axis=0))

    @pl.when(i == 0)
    def _():
        inv_ref[...] = upd

    @pl.when(i > 0)
    def _():
        inv_ref[...] = inv_ref[...] + upd

    @pl.when(i == pl.num_programs(0) - 1)
    def _():
        acc = inv_ref[...]
        n_blk = acc.shape[0]
        slot = (lax.broadcasted_iota(jnp.int32, (n_blk, LANES), 0) * LANES
                + lax.broadcasted_iota(jnp.int32, (n_blk, LANES), 1)).astype(F32)
        spare = slot - n_tokens * jnp.floor((slot + 0.5) * (1.0 / n_tokens))
        src = jnp.where(acc[:, 2 * LANES:] > 0.0, acc[:, :LANES] * LANES + acc[:, LANES:2 * LANES], spare)
        src_ref[...] = src.astype(jnp.int32)


def _route_fields(lg, cnt_ref, live):
    tm = lg.shape[0]
    lane = lax.broadcasted_iota(jnp.int32, lg.shape, 1).astype(F32)
    big = float(LANES)
    is_g = (lane >= ROUTE_G) & (lane < ROUTE_G + MOE_GROUPS)
    lgg = jnp.where(is_g, lg, -jnp.inf)
    g_max = jnp.max(lgg, axis=-1, keepdims=True)
    g_idx = jnp.min(jnp.where(lgg == g_max, lane - ROUTE_G, big), axis=-1, keepdims=True)
    g_prob = 1.0 / jnp.sum(jnp.exp(lgg - g_max), axis=-1, keepdims=True)
    lo = g_idx * MOE_EXPERTS
    le = jnp.where((lane >= lo) & (lane < lo + MOE_EXPERTS), lg, -jnp.inf)
    l1 = jnp.max(le, axis=-1, keepdims=True)
    i1 = jnp.min(jnp.where(le == l1, lane, big), axis=-1, keepdims=True)
    le2 = jnp.where(lane == i1, -jnp.inf, le)
    l2 = jnp.max(le2, axis=-1, keepdims=True)
    i2 = jnp.min(jnp.where(le2 == l2, lane, big), axis=-1, keepdims=True)
    r = jnp.exp(l2 - l1)
    w1 = g_prob / (1.0 + r)
    w2 = w1 * r
    oh1 = jnp.where(lane == i1, 1.0, 0.0)
    oh2 = jnp.where(lane == i2, 1.0, 0.0)
    oh = oh1 + oh2
    row = lax.broadcasted_iota(jnp.int32, (tm, tm), 0)
    col = lax.broadcasted_iota(jnp.int32, (tm, tm), 1)
    before = jnp.where(col < row, 1.0, 0.0).astype(BF16)
    prefix = _dot(before, oh.astype(BF16)) + cnt_ref[0:1, :]
    rank1 = jnp.sum(prefix * oh1, axis=-1, keepdims=True)
    rank2 = jnp.sum(prefix * oh2, axis=-1, keepdims=True)
    cnt_ref[...] = cnt_ref[...] + jnp.sum(oh, axis=0, keepdims=True) * live
    fields = (i1, i2, w1, w2, rank1, rank2)
    out = jnp.zeros_like(lg)
    for j, f in enumerate(fields):
        out = jnp.where(lane == float(j), f, out)
    return out


def _router(fields, counts, n_sorted):
    t = fields.shape[0]
    tm = ROW_TILE
    n_blk = n_sorted // LANES
    fixed = lambda i: (0, 0)
    return pl.pallas_call(
        functools.partial(_router_kernel, moe_tile=MOE_TILE, n_tokens=t),
        out_shape=[jax.ShapeDtypeStruct((t, LANES), F32), jax.ShapeDtypeStruct((n_blk, LANES), jnp.int32)],
        grid=(t // tm,),
        in_specs=[pl.BlockSpec((tm, LANES), lambda i: (i, 0)), pl.BlockSpec((8, LANES), fixed)],
        out_specs=[pl.BlockSpec((tm, LANES), lambda i: (i, 0)), pl.BlockSpec((n_blk, LANES), fixed)],
        scratch_shapes=[pltpu.VMEM((8, LANES), F32), pltpu.VMEM((n_blk, 3 * LANES), F32)],
        compiler_params=_params(1), name="router",
    )(fields, counts)


def _moe_kernel(te_ref, nt_ref, x_ref, wg_ref, wu_ref, wd_ref, o_ref, gate_bf, up_bf, down_bf):
    i = pl.program_id(0)
    valid = i < nt_ref[0]
    fresh = (i == 0) | (te_ref[i] != te_ref[jnp.maximum(i - 1, 0)])

    @pl.when(valid & fresh)
    def _():
        gate_bf[...] = wg_ref[0].astype(BF16)
        up_bf[...] = wu_ref[0].astype(BF16)
        down_bf[...] = wd_ref[0].astype(BF16)

    @pl.when(valid)
    def _():
        x = x_ref[...].astype(BF16)
        act = _silu(_dot(x, gate_bf[...])) * _dot(x, up_bf[...])
        o_ref[...] = _dot(act.astype(BF16), down_bf[...]).astype(o_ref.dtype)

    @pl.when(jnp.logical_not(valid))
    def _():
        o_ref[...] = jnp.zeros_like(o_ref)


def _moe_experts(x_sorted, tile_expert, n_tiles_used, wg, wu, wd):
    tm = MOE_TILE
    rows, d = x_sorted.shape
    ff = wg.shape[2]
    grid_spec = pltpu.PrefetchScalarGridSpec(
        num_scalar_prefetch=2, grid=(rows // tm,),
        in_specs=[pl.BlockSpec((tm, d), lambda i, te, nt: (i, 0)),
                  pl.BlockSpec((1, d, ff), lambda i, te, nt: (te[i], 0, 0)),
                  pl.BlockSpec((1, d, ff), lambda i, te, nt: (te[i], 0, 0)),
                  pl.BlockSpec((1, ff, d), lambda i, te, nt: (te[i], 0, 0))],
        out_specs=pl.BlockSpec((tm, d), lambda i, te, nt: (i, 0)),
        scratch_shapes=[pltpu.VMEM((d, ff), BF16), pltpu.VMEM((d, ff), BF16), pltpu.VMEM((ff, d), BF16)])
    return pl.pallas_call(
        _moe_kernel, out_shape=jax.ShapeDtypeStruct((rows, d), F32), grid_spec=grid_spec,
        compiler_params=_params(1), name="moe_experts",
    )(tile_expert, n_tiles_used, x_sorted, wg, wu, wd)


def _combine_kernel(x_ref, y1_ref, y2_ref, rt_ref, mod_ref, o_ref):
    f = rt_ref[0, :, 2:3] * y1_ref[0] + rt_ref[0, :, 3:4] * y2_ref[0]
    o_ref[0] = x_ref[0] + mod_ref[0, 5:6, :] * f


def _combine(x, y1, y2, route, mod_l, ctx_tiles, ctx_row):
    bsz, s, d = x.shape
    tm = ROW_TILE
    tok = lambda b, i: (b, i, 0)
    mod_idx = lambda b, i: (jnp.where(i < ctx_tiles, ctx_row, b), 0, 0)
    return pl.pallas_call(
        _combine_kernel, out_shape=jax.ShapeDtypeStruct(x.shape, F32), grid=(bsz, s // tm),
        in_specs=[pl.BlockSpec((1, tm, d), tok), pl.BlockSpec((1, tm, d), tok), pl.BlockSpec((1, tm, d), tok),
                  pl.BlockSpec((1, tm, LANES), tok), pl.BlockSpec((1, 6, d), mod_idx)],
        out_specs=pl.BlockSpec((1, tm, d), tok),
        compiler_params=_params(2), name="moe_combine",
    )(x, y1.reshape(x.shape), y2.reshape(x.shape), route.reshape(bsz, s, LANES), mod_l)


def _hier_moe(h2, fields, counts, layer, wg, wu, wd):
    bsz, s, d = h2.shape
    t = bsz * s
    tm = MOE_TILE
    n_tiles = 2 * t // tm + N_EXPERTS
    route, src = _router(fields.reshape(t, LANES), counts, n_tiles * tm)
    tiles_per = (counts[0, :N_EXPERTS].astype(jnp.int32) + tm - 1) // tm
    tile_end = jnp.cumsum(tiles_per)
    tile_ids = jnp.arange(n_tiles, dtype=jnp.int32)
    tile_expert = jnp.minimum(jnp.sum((tile_end[None, :] <= tile_ids[:, None]).astype(jnp.int32), axis=1),
                              N_EXPERTS - 1) + layer * N_EXPERTS
    pos = route[:, 6:8].astype(jnp.int32)
    x_sorted = jnp.take(h2.reshape(t, d), src.reshape(-1), axis=0, mode="clip")
    y_sorted = _moe_experts(x_sorted, tile_expert, tile_end[-1:], wg.reshape(-1, d, wg.shape[-1]),
                            wu.reshape(-1, d, wu.shape[-1]), wd.reshape(-1, wd.shape[-2], d))
    y1 = jnp.take(y_sorted, pos[:, 0], axis=0, mode="clip")
    y2 = jnp.take(y_sorted, pos[:, 1], axis=0, mode="clip")
    return y1, y2, route


def _attn_kernel(q_ref, k_ref, v_ref, bias_ref, qw_ref, kw_ref, o_ref, qn_ref, kn_ref, *, ctx_len, rows):
    lane = lax.broadcasted_iota(jnp.int32, (1, LANES), 1)
    first = lane < NA_HEAD_DIM

    r_head = lax.broadcasted_iota(jnp.int32, (LANES, LANES), 0) // NA_HEAD_DIM
    c_head = lax.broadcasted_iota(jnp.int32, (LANES, LANES), 1) // NA_HEAD_DIM
    same_head = jnp.where(r_head == c_head, 1.0, 0.0).astype(BF16)

    def head_norm(x, w):
        ms = _dot((x * x).astype(BF16), same_head) * (1.0 / NA_HEAD_DIM)
        return x * lax.rsqrt(ms + RMS_EPS) * w

    kn_ref[...] = head_norm(k_ref[0].astype(F32), kw_ref[...]).astype(BF16)
    qn_ref[...] = (head_norm(q_ref[0, ctx_len:, :].astype(F32), qw_ref[...]) * NA_HEAD_DIM ** -0.5).astype(BF16)
    n_groups = rows // ATTN_GROUP_ROWS
    n_q = ATTN_GROUP_ROWS * GRID_W
    n_loc = ATTN_KEY_ROWS * GRID_W
    k_ctx = kn_ref[0:ctx_len, :]
    v_ctx = v_ref[0, 0:ctx_len, :]

    def one_group(g):
        kind = jnp.where(g == 0, 0, jnp.where(g == n_groups - 1, 2, 1))
        kr0 = jnp.clip(g * ATTN_GROUP_ROWS - NA_KH // 2, 0, rows - ATTN_KEY_ROWS)
        q_rows = pl.ds(pl.multiple_of(g * n_q, n_q), n_q)
        q = qn_ref[q_rows, :]
        zero = jnp.zeros_like(q)
        q2 = jnp.concatenate([jnp.where(first, q, zero), jnp.where(first, zero, q)], axis=0)
        k_off = pl.multiple_of(ctx_len + kr0 * GRID_W, GRID_W)
        s_loc = _dot_nt(q2, kn_ref[pl.ds(k_off, n_loc), :])
        s_ctx = _dot_nt(q2, k_ctx)
        p_loc, p_ctx, inv = [], [], []
        for hh in range(2):
            sl = s_loc[hh * n_q:(hh + 1) * n_q] + bias_ref[hh, kind]
            sc = s_ctx[hh * n_q:(hh + 1) * n_q]
            m = jnp.maximum(jnp.max(sl, axis=-1, keepdims=True), jnp.max(sc, axis=-1, keepdims=True))
            el = jnp.exp(sl - m)
            ec = jnp.exp(sc - m)
            inv.append(1.0 / (jnp.sum(el, axis=-1, keepdims=True) + jnp.sum(ec, axis=-1, keepdims=True)))
            p_loc.append(el.astype(BF16))
            p_ctx.append(ec.astype(BF16))
        o = (_dot(jnp.concatenate(p_loc, axis=0), v_ref[0, pl.ds(k_off, n_loc), :])
             + _dot(jnp.concatenate(p_ctx, axis=0), v_ctx))
        o_ref[0, q_rows, :] = jnp.where(first, o[:n_q] * inv[0], o[n_q:] * inv[1]).astype(o_ref.dtype)

    def body(i, carry):
        for j in range(ATTN_GROUPS_PER_TRIP):
            one_group(i * ATTN_GROUPS_PER_TRIP + j)
        return carry

    lax.fori_loop(0, n_groups // ATTN_GROUPS_PER_TRIP, body, 0)


def _attn_group_layout(rows):
    n_groups = rows // ATTN_GROUP_ROWS
    assert rows % ATTN_GROUP_ROWS == 0 and rows >= ATTN_KEY_ROWS and n_groups >= 2
    u = np.arange(ATTN_GROUP_ROWS)[:, None]
    i = np.arange(ATTN_KEY_ROWS)[None, :]

    def layout(g):
        r = g * ATTN_GROUP_ROWS + u
        r0 = np.clip(r - NA_KH // 2, 0, rows - NA_KH)
        kr = np.clip(g * ATTN_GROUP_ROWS - NA_KH // 2, 0, rows - ATTN_KEY_ROWS) + i
        return (kr >= r0) & (kr < r0 + NA_KH), kr - r + NA_KH - 1

    kinds = [layout(0), layout(1), layout(n_groups - 1)]
    for g in range(1, n_groups - 1):
        valid, d = layout(g)
        assert (valid == kinds[1][0]).all() and (d[valid] == kinds[1][1][valid]).all()
    return np.stack([k[0] for k in kinds]), np.stack([k[1] for k in kinds])


def _bias_windows(rpb, rows):
    qc = np.arange(GRID_W)
    c0 = np.clip(qc - NA_KW // 2, 0, GRID_W - NA_KW)
    kc = np.arange(GRID_W)
    inwin = (kc[None, :] >= c0[:, None]) & (kc[None, :] < c0[:, None] + NA_KW)
    coff = kc[None, :] - qc[:, None] + NA_KW - 1
    pick = (coff[..., None] == np.arange(2 * NA_KW - 1)) & inwin[..., None]
    tab = jnp.einsum("hdo,qko->dhqk", rpb.astype(F32), jnp.asarray(pick, F32), precision=HIGHEST)
    tab = jnp.where(inwin[None, None], tab, NEG)
    valid, d = _attn_group_layout(rows)
    masked = jnp.full(tab.shape[1:], NEG, F32)
    blocks = []
    for kind in range(valid.shape[0]):
        for u in range(ATTN_GROUP_ROWS):
            blocks.append(jnp.concatenate(
                [tab[int(d[kind, u, i])] if valid[kind, u, i] else masked for i in range(ATTN_KEY_ROWS)], axis=-1))
    win = jnp.stack(blocks, axis=1)
    return win.reshape(rpb.shape[0], valid.shape[0], ATTN_GROUP_ROWS * GRID_W, ATTN_KEY_ROWS * GRID_W)


def _neighbourhood_attention(q, k, v, bias, qn_w, kn_w, ctx_len):
    bsz, s, w = q.shape
    seq = s - ctx_len
    rows = seq // GRID_W
    n_pairs = NA_HEADS // 2
    pair = lambda b, p: (b, 0, p)
    row2 = lambda b, p: (0, 0)
    return pl.pallas_call(
        functools.partial(_attn_kernel, ctx_len=ctx_len, rows=rows),
        out_shape=jax.ShapeDtypeStruct((bsz, seq, w), BF16),
        grid=(bsz, n_pairs),
        in_specs=[pl.BlockSpec((1, s, LANES), pair), pl.BlockSpec((1, s, LANES), pair),
                  pl.BlockSpec((1, s, LANES), pair),
                  pl.BlockSpec((2,) + bias.shape[1:], lambda b, p: (p, 0, 0, 0)),
                  pl.BlockSpec((1, LANES), row2), pl.BlockSpec((1, LANES), row2)],
        out_specs=pl.BlockSpec((1, seq, LANES), pair),
        scratch_shapes=[pltpu.VMEM((seq, LANES), BF16), pltpu.VMEM((s, LANES), BF16)],
        compiler_params=_params(2), name="neighbourhood_attention",
    )(q, k, v, bias, jnp.tile(qn_w, 2).reshape(1, LANES), jnp.tile(kn_w, 2).reshape(1, LANES))


def _pad_row(pieces, width=LANES):
    row = jnp.zeros((width,), F32)
    for off, vec in pieces:
        row = row.at[off:off + vec.shape[0]].set(vec.astype(F32))
    return row.reshape(1, width)


def _router_params(w_group, b_group, w_expert, b_expert):
    d = w_group.shape[0]
    w = jnp.zeros((d, LANES), F32).at[:, :N_EXPERTS].set(w_expert).at[:, ROUTE_G:ROUTE_G + MOE_GROUPS].set(w_group)
    return _split_weight(w), _pad_row([(0, b_expert), (ROUTE_G, b_group)])


def kernel(x, c, ctx, c_ctx, norm1_w, norm2_w, mod_w, mod_b, ab_w_in, ab_conv_w, ab_conv_b, ssd_a_log, ssd_dt_bias, ssd_d, ssd_norm_w, ml_i_bias, ml_f_bias, ml_norm_w, ab_w_out, na_w_qkv, na_q_norm, na_k_norm, na_rpb, na_w_out, moe_w_group, moe_b_group, moe_w_expert, moe_b_expert, moe_w_gate, moe_w_up, moe_w_down):
    bsz, seq, d = x.shape
    ctx_len = ctx.shape[1]
    depth = mod_w.shape[0]
    assert depth == 2 and ctx_len % ROW_TILE == 0 and seq % ROW_TILE == 0 and bsz < 8
    ctx_tiles = ctx_len // ROW_TILE
    lat_tiles = seq // ROW_TILE
    ctx_row = bsz

    cvec = jnp.zeros((8, d), F32).at[:bsz].set(c).at[bsz].set(c_ctx)
    mod = _mod_vectors(cvec, mod_w, mod_b)
    xs = (ctx, x)

    ssd_w = SSD_HEADS * SSD_HEAD_DIM
    xbc_w = ssd_w + 2 * SSD_GROUPS * SSD_STATE
    qk_w, v_w = ML_HEADS * ML_DK, ML_HEADS * ML_DV
    sizes = (ssd_w, xbc_w, 2 * SSD_HEADS, qk_w, qk_w, v_w, v_w, 2 * ML_HEADS, 2 * ML_HEADS)
    w_z, w_xbc, w_dt, w_q, w_k, w_v, w_o, w_i, w_f = jnp.split(ab_w_in[0], np.cumsum(sizes)[:-1].tolist(), axis=1)
    w_gate = jnp.zeros((d, LANES), F32).at[:, :GATE_F + 2 * ML_HEADS].set(jnp.concatenate([w_dt, w_i, w_f], axis=1))
    weights = [w.astype(BF16) for w in (w_z, w_xbc, w_q, w_k, w_v, w_o)] + [_split_weight(w_gate)]
    z, xbc, q, k, v, og, gates = _norm_mod_matmul(xs, norm1_w[0], mod[0], weights, [BF16] * 6 + [F32],
                                                  ctx_tiles, ctx_row)
    xbc = _conv_silu(xbc, ab_conv_w[0], ab_conv_b[0], ctx_len)
    a_neg = -jnp.exp(ssd_a_log[0].astype(F32))
    dsk_row = jnp.repeat(ssd_d[0].astype(F32), SSD_HEAD_DIM).reshape(1, ssd_w)
    y = None
    hm = None
    for dr in range(2):
        rev = dr == 1
        dtb_row = _pad_row([(GATE_DT + dr * SSD_HEADS, ssd_dt_bias[0, dr])])
        a_row = _pad_row([(GATE_DT + dr * SSD_HEADS, a_neg[dr])])
        y = _ssd_scan(xbc, gates, dtb_row, a_row, y if rev else dsk_row, rev=rev, ctx_len=ctx_len)
        ib_row = _pad_row([(GATE_I + dr * ML_HEADS, ml_i_bias[0, dr])])
        fb_row = _pad_row([(GATE_F + dr * ML_HEADS, ml_f_bias[0, dr])])
        hm = _mlstm_scan(q, k, v, gates, ib_row, fb_row, hm, rev=rev, ctx_len=ctx_len)
    w_r, b_r = _router_params(moe_w_group[0], moe_b_group[0], moe_w_expert[0], moe_b_expert[0])
    w_out = ab_w_out[0].astype(BF16)
    x1, h2, fields, counts = _post_call(
        "mix", xs, [y, z, hm, og],
        [ssd_norm_w[0].reshape(1, ssd_w), ml_norm_w[0].reshape(1, v_w)], [w_out[:ssd_w], w_out[ssd_w:]],
        mod[0], norm2_w[0], w_r, b_r, 0, ctx_tiles + lat_tiles, ctx_tiles, ctx_row)
    moe0 = _hier_moe(h2, fields, counts, 0, moe_w_gate, moe_w_up, moe_w_down)

    w_qkv = na_w_qkv[0].astype(BF16)
    na_w = NA_HEADS * NA_HEAD_DIM
    q, k, v, xs = _norm_mod_matmul(x1, norm1_w[1], mod[1],
                                   [w_qkv[:, :na_w], w_qkv[:, na_w:2 * na_w], w_qkv[:, 2 * na_w:]],
                                   [BF16] * 3, ctx_tiles, ctx_row, pending_moe=(*moe0, mod[0]))
    bias = _bias_windows(na_rpb[0], seq // GRID_W)
    attn = _neighbourhood_attention(q, k, v, bias, na_q_norm[0], na_k_norm[0], ctx_len)
    w_r, b_r = _router_params(moe_w_group[1], moe_b_group[1], moe_w_expert[1], moe_b_expert[1])
    x1, h2, fields, counts = _post_call("attn", xs, [attn], [], [na_w_out[0].astype(BF16)],
                                        mod[1], norm2_w[1], w_r, b_r, ctx_tiles, lat_tiles, ctx_tiles, ctx_row)
    y1, y2, route = _hier_moe(h2, fields, counts, 1, moe_w_gate, moe_w_up, moe_w_down)
    return _combine(x1, y1, y2, route, mod[1], 0, ctx_row)
```

```python
import functools

import numpy as np
import jax
import jax.numpy as jnp
from jax import lax
from jax.experimental import pallas as pl
from jax.experimental.pallas import tpu as pltpu

F32 = jnp.float32
BF16 = jnp.bfloat16
HIGHEST = lax.Precision.HIGHEST

RMS_EPS = 1e-6
GRID_W = 64
SSD_HEADS = 16
SSD_HEAD_DIM = 64
SSD_GROUPS = 2
SSD_STATE = 128
ML_HEADS = 4
ML_DK = 128
ML_DV = 256
NA_HEADS = 16
NA_HEAD_DIM = 64
NA_KH = 8
NA_KW = 16
MOE_GROUPS = 4
MOE_EXPERTS = 8
N_EXPERTS = MOE_GROUPS * MOE_EXPERTS

LANES = 128
ROW_TILE = 256
SCAN_CHUNK = 128
SSD_BLOCK_CHUNKS = 2
ML_CHUNK = 256
MOE_TILE = 512
ATTN_GROUP_ROWS = 4
ATTN_KEY_ROWS = ATTN_GROUP_ROWS + NA_KH - 1
ATTN_GROUPS_PER_TRIP = 4
VMEM_LIMIT = 56 * 1024 * 1024

GATE_DT = 0
GATE_I = 2 * SSD_HEADS
GATE_F = GATE_I + 2 * ML_HEADS
ROUTE_G = N_EXPERTS
NEG = -1e30


def _params(n_axes):
    return pltpu.CompilerParams(dimension_semantics=("arbitrary",) * n_axes,
                                vmem_limit_bytes=VMEM_LIMIT)


def _silu(x):
    return x * jax.nn.sigmoid(x)


def _softplus(x):
    return jnp.maximum(x, 0.0) + jnp.log1p(jnp.exp(-jnp.abs(x)))


def _rms(x, w):
    return x * lax.rsqrt(jnp.mean(x * x, axis=-1, keepdims=True) + RMS_EPS) * w


def _dot(a, b):
    return jnp.dot(a, b, preferred_element_type=F32)


def _dot_nt(a, b):
    return lax.dot_general(a, b, (((1,), (1,)), ((), ())), preferred_element_type=F32)


def _dot_hi(a, b):
    return jnp.dot(a, b, precision=HIGHEST, preferred_element_type=F32)


def _split_bf16(x, terms):
    parts = []
    for _ in range(terms - 1):
        p = x.astype(BF16)
        parts.append(p)
        x = x - p.astype(F32)
    parts.append(x.astype(BF16))
    return parts


def _split_weight(w):
    return jnp.stack(_split_bf16(w.astype(F32), 2))


def _dot_split(a, w2_ref):
    a_hi, a_lo = _split_bf16(a, 2)
    return _dot(a_hi, w2_ref[0]) + _dot(a_lo, w2_ref[0]) + _dot(a_hi, w2_ref[1])


def _cumsum_dot(tri, x):
    tri = jnp.where(tri, 1.0, 0.0).astype(BF16)
    hi, mid, lo = _split_bf16(x, 3)
    return _dot(tri, hi) + _dot(tri, mid) + _dot(tri, lo)


def _mod_kernel(c_ref, w_ref, b_ref, o_ref):
    o_ref[0] = _dot_hi(_silu(c_ref[...]), w_ref[0]) + b_ref[0]


def _mod_vectors(cvec, mod_w, mod_b, tn=512):
    depth, d, n = mod_w.shape
    rows = cvec.shape[0]
    out = pl.pallas_call(
        _mod_kernel,
        out_shape=jax.ShapeDtypeStruct((depth, rows, n), F32),
        grid=(depth, n // tn),
        in_specs=[pl.BlockSpec((rows, d), lambda l, j: (0, 0)),
                  pl.BlockSpec((1, d, tn), lambda l, j: (l, 0, j)),
                  pl.BlockSpec((1, 1, tn), lambda l, j: (l, 0, j))],
        out_specs=pl.BlockSpec((1, rows, tn), lambda l, j: (l, 0, j)),
        compiler_params=_params(2),
        name="mod_vectors",
    )(cvec, mod_w, mod_b.reshape(depth, 1, n))
    return out.reshape(depth, rows, 6, d)


def _stream_tile(refs, split_tiles, tile):
    if not split_tiles:
        return refs[0][0], refs[1:]
    return jnp.where(tile < split_tiles, refs[0][0], refs[1][0]), refs[2:]


def _stream_specs(xs, tm, offset=0):
    if not isinstance(xs, tuple):
        return [xs], [pl.BlockSpec((1, tm, xs.shape[2]), lambda b, i: (b, i + offset, 0))], 0
    ctx, lat = xs
    split = ctx.shape[1] // tm
    d = ctx.shape[2]
    return ([ctx, lat],
            [pl.BlockSpec((1, tm, d), lambda b, i: (b, jnp.minimum(i + offset, split - 1), 0)),
             pl.BlockSpec((1, tm, d), lambda b, i: (b, jnp.maximum(i + offset - split, 0), 0))], split)


def _nmm_kernel(*refs, n_out, pending_moe, split_tiles):
    x, refs = _stream_tile(refs, split_tiles, pl.program_id(1))
    if pending_moe:
        y1_ref, y2_ref, rt_ref, pmod_ref = refs[:4]
        refs = refs[4:]
        x = x + pmod_ref[0, 5:6, :] * (rt_ref[0, :, 2:3] * y1_ref[0] + rt_ref[0, :, 3:4] * y2_ref[0])
        refs[-1][0] = x
        refs = refs[:-1]
    nw_ref, mod_ref = refs[:2]
    refs = refs[2:]
    w_refs, o_refs = refs[:n_out], refs[n_out:]
    h = _rms(x, nw_ref[...])
    h = h * (1.0 + mod_ref[0, 1:2, :]) + mod_ref[0, 0:1, :]
    hb = h.astype(BF16)
    for w_ref, o_ref in zip(w_refs, o_refs):
        if len(w_ref.shape) == 3:
            o_ref[0] = _dot_split(h, w_ref)
        else:
            o_ref[0] = _dot(hb, w_ref[...]).astype(o_ref.dtype)


def _norm_mod_matmul(xs, norm_w, mod_l, weights, out_dtypes, ctx_tiles, ctx_row, pending_moe=None):
    tm = ROW_TILE
    args, in_specs, split_tiles = _stream_specs(xs, tm)
    bsz, d = args[0].shape[0], args[0].shape[2]
    s = sum(a.shape[1] for a in args)
    tok = lambda b, i: (b, i, 0)
    mod_idx = lambda b, i: (jnp.where(i < ctx_tiles, ctx_row, b), 0, 0)
    if pending_moe is not None:
        y1, y2, route, mod_prev = pending_moe
        args += [y1.reshape(bsz, s, d), y2.reshape(bsz, s, d), route.reshape(bsz, s, LANES), mod_prev]
        in_specs += [pl.BlockSpec((1, tm, d), tok), pl.BlockSpec((1, tm, d), tok),
                     pl.BlockSpec((1, tm, LANES), tok), pl.BlockSpec((1, 6, d), mod_idx)]
    args += [norm_w.reshape(1, d), mod_l, *weights]
    in_specs += [pl.BlockSpec((1, d), lambda b, i: (0, 0)), pl.BlockSpec((1, 6, d), mod_idx)]
    in_specs += [pl.BlockSpec(w.shape, lambda b, i, nd=w.ndim: (0,) * nd) for w in weights]
    out_shape = [jax.ShapeDtypeStruct((bsz, s, w.shape[-1]), dt) for w, dt in zip(weights, out_dtypes)]
    out_specs = [pl.BlockSpec((1, tm, w.shape[-1]), tok) for w in weights]
    if pending_moe is not None:
        out_shape.append(jax.ShapeDtypeStruct((bsz, s, d), F32))
        out_specs.append(pl.BlockSpec((1, tm, d), tok))
    return pl.pallas_call(
        functools.partial(_nmm_kernel, n_out=len(weights), pending_moe=pending_moe is not None,
                          split_tiles=split_tiles),
        out_shape=out_shape, grid=(bsz, s // tm), in_specs=in_specs, out_specs=out_specs,
        compiler_params=_params(2), name="norm_mod_matmul",
    )(*args)


def _conv_kernel(u_ref, w_ref, b_ref, o_ref, *, ctx_len):
    u = u_ref[0].astype(F32)
    s = u.shape[0]
    t = lax.broadcasted_iota(jnp.int32, u.shape, 0)
    prev = jnp.where((t == 0) | (t == ctx_len), 0.0, pltpu.roll(u, 1, axis=0))
    nxt = jnp.where((t == ctx_len - 1) | (t == s - 1), 0.0, pltpu.roll(u, s - 1, axis=0))
    y = prev * w_ref[0:1, :] + u * w_ref[1:2, :] + nxt * w_ref[2:3, :] + b_ref[...]
    o_ref[0] = _silu(y).astype(o_ref.dtype)


def _conv_silu(u, conv_w, conv_b, ctx_len, tc=2 * LANES):
    bsz, s, ch = u.shape
    return pl.pallas_call(
        functools.partial(_conv_kernel, ctx_len=ctx_len),
        out_shape=jax.ShapeDtypeStruct(u.shape, u.dtype),
        grid=(bsz, ch // tc),
        in_specs=[pl.BlockSpec((1, s, tc), lambda b, j: (b, 0, j)),
                  pl.BlockSpec((3, tc), lambda b, j: (0, j)),
                  pl.BlockSpec((1, tc), lambda b, j: (0, j))],
        out_specs=pl.BlockSpec((1, s, tc), lambda b, j: (b, 0, j)),
        compiler_params=_params(2), name="conv_silu",
    )(u, conv_w, conv_b.reshape(1, ch))


def _scan_chunk_index(c, rev, n_ctx, n_all):
    if not rev:
        return c
    return jnp.where(c < n_ctx, n_ctx - 1 - c, n_ctx + n_all - 1 - c)


def _tri(n, rev):
    row = lax.broadcasted_iota(jnp.int32, (n, n), 0)
    col = lax.broadcasted_iota(jnp.int32, (n, n), 1)
    return (col >= row) if rev else (col <= row)


def _ssd_kernel(*refs, rev, off):
    if rev:
        xs_ref, bc_ref, g_ref, dtb_ref, a_ref, ex_ref, acc_ref, o_ref, st_ref = refs
    else:
        xs_ref, bc_ref, g_ref, dtb_ref, a_ref, ex_ref, dsk_ref, o_ref, st_ref = refs

    @pl.when(pl.program_id(1) == 0)
    def _():
        st_ref[...] = jnp.zeros_like(st_ref)

    n = SCAN_CHUNK
    n_sub = xs_ref.shape[1] // n
    for sub in (range(n_sub - 1, -1, -1) if rev else range(n_sub)):
        _ssd_chunk(refs, slice(sub * n, (sub + 1) * n), rev, off)


def _ssd_chunk(refs, rows, rev, off):
    if rev:
        xs_ref, bc_ref, g_ref, dtb_ref, a_ref, ex_ref, acc_ref, o_ref, st_ref = refs
    else:
        xs_ref, bc_ref, g_ref, dtb_ref, a_ref, ex_ref, dsk_ref, o_ref, st_ref = refs
    n = SCAN_CHUNK
    last = 0 if rev else n - 1
    tri = _tri(n, rev)
    dt = _softplus(g_ref[0, rows, :] + dtb_ref[...])
    log_a = dt * a_ref[...]
    cs = _cumsum_dot(tri, log_a)
    cs_t = cs.T
    dt_hi = dt.astype(BF16)
    dt_lo = (dt - dt_hi.astype(F32)).astype(BF16)
    dt_full = _dot(dt_hi, ex_ref[...]) + _dot(dt_lo, ex_ref[...])
    xs = xs_ref[0, rows, :].astype(F32)
    xdt = (xs * dt_full).astype(BF16)
    lo_half = lax.broadcasted_iota(jnp.int32, (1, LANES), 1) < SSD_HEAD_DIM
    gw = SSD_GROUPS * SSD_STATE
    heads_per_group = SSD_HEADS // SSD_GROUPS
    for g in range(SSD_GROUPS):
        b_g = bc_ref[0, rows, g * SSD_STATE:(g + 1) * SSD_STATE]
        c_g = bc_ref[0, rows, gw + g * SSD_STATE:gw + (g + 1) * SSD_STATE]
        cb = _dot_nt(c_g, b_g)
        b_t = b_g.astype(F32).T
        for e in range(0, heads_per_group, 2):
            h0 = g * heads_per_group + e
            pair = h0 // 2
            sl = slice(pair * LANES, (pair + 1) * LANES)
            x_pair = xdt[:, sl]
            zero = jnp.zeros_like(x_pair)
            y, upd, a_bc, tots = None, None, [], []
            for j in range(2):
                col = off + h0 + j
                x_j = jnp.where(lo_half, x_pair, zero) if j == 0 else jnp.where(lo_half, zero, x_pair)
                a_b = jnp.broadcast_to(cs[:, col:col + 1], (n, LANES))
                a_row = cs_t[col:col + 1, :]
                tot = cs_t[col:col + 1, last:last + 1]
                decay = jnp.exp(jnp.where(tri, a_b - a_row, -jnp.inf))
                y_j = _dot((cb * decay).astype(BF16), x_j)
                upd_j = _dot((b_t * jnp.exp(tot - a_row)).astype(BF16), x_j)
                y = y_j if y is None else y + y_j
                upd = upd_j if upd is None else upd + upd_j
                a_bc.append(a_b)
                tots.append(tot)
            state = st_ref[pair]
            y = y + _dot(c_g, state.astype(BF16)) * jnp.exp(jnp.where(lo_half, a_bc[0], a_bc[1]))
            st_ref[pair] = state * jnp.exp(jnp.where(lo_half, tots[0], tots[1])) + upd
            if rev:
                y = y + acc_ref[0, rows, sl].astype(F32)
            else:
                y = y + dsk_ref[:, sl] * xs[:, sl]
            o_ref[0, rows, sl] = y.astype(o_ref.dtype)


def _ssd_scan(xbc, gates, dtb_row, a_row, extra, *, rev, ctx_len):
    bsz, s, _ = xbc.shape
    n = SCAN_CHUNK * SSD_BLOCK_CHUNKS
    n_all = s // n
    n_ctx = ctx_len // n
    w = SSD_HEADS * SSD_HEAD_DIM
    bcw = 2 * SSD_GROUPS * SSD_STATE
    assert SCAN_CHUNK == LANES and 2 * SSD_HEAD_DIM == LANES and (SSD_HEADS // SSD_GROUPS) % 2 == 0
    assert ctx_len % n == 0 and s % n == 0
    off = GATE_DT + (SSD_HEADS if rev else 0)
    cidx = functools.partial(_scan_chunk_index, rev=rev, n_ctx=n_ctx, n_all=n_all)
    tok = lambda b, c: (b, cidx(c), 0)
    const = lambda b, c: (0, 0)
    expand = jnp.asarray(np.arange(LANES)[:, None] == off + np.arange(w)[None, :] // SSD_HEAD_DIM, BF16)
    in_specs = [pl.BlockSpec((1, n, w), tok),
                pl.BlockSpec((1, n, bcw), lambda b, c: (b, cidx(c), w // bcw)),
                pl.BlockSpec((1, n, LANES), tok),
                pl.BlockSpec((1, LANES), const),
                pl.BlockSpec((1, LANES), const),
                pl.BlockSpec((LANES, w), const)]
    if rev:
        in_specs.append(pl.BlockSpec((1, n, w), tok))
    else:
        in_specs.append(pl.BlockSpec((1, w), const))
    return pl.pallas_call(
        functools.partial(_ssd_kernel, rev=rev, off=off),
        out_shape=jax.ShapeDtypeStruct((bsz, s, w), BF16),
        grid=(bsz, n_all), in_specs=in_specs,
        out_specs=pl.BlockSpec((1, n, w), tok),
        scratch_shapes=[pltpu.VMEM((SSD_HEADS // 2, SSD_STATE, LANES), F32)],
        compiler_params=_params(2), name="ssd_scan_bwd" if rev else "ssd_scan_fwd",
    )(xbc, xbc, gates, dtb_row, a_row, expand, extra)


def _mlstm_kernel(*refs, rev, d):
    if rev:
        q_ref, k_ref, v_ref, g_ref, ib_ref, fb_ref, acc_ref, o_ref, c_st, n_st, m_st = refs
    else:
        q_ref, k_ref, v_ref, g_ref, ib_ref, fb_ref, o_ref, c_st, n_st, m_st = refs

    @pl.when(pl.program_id(1) == 0)
    def _():
        c_st[...] = jnp.zeros_like(c_st)
        n_st[...] = jnp.zeros_like(n_st)
        m_st[...] = jnp.zeros_like(m_st)

    n = q_ref.shape[1]
    last = 0 if rev else n - 1
    tri = _tri(n, rev)
    g = g_ref[0]
    log_i = g + ib_ref[...]
    log_f = -_softplus(-(g + fb_ref[...]))
    cs = _cumsum_dot(tri, log_f)
    cs_t = cs.T
    li_t = log_i.T
    for h in range(ML_HEADS):
        ci = GATE_I + ML_HEADS * d + h
        cf = GATE_F + ML_HEADS * d + h
        b_col = cs[:, cf:cf + 1]
        off_row = cs_t[cf:cf + 1, :] - li_t[ci:ci + 1, :]
        tot = cs_t[cf:cf + 1, last:last + 1]
        m_prev = m_st[h, 0:1, 0:1]
        pmax = jnp.max(jnp.where(tri, -off_row, -jnp.inf), axis=-1, keepdims=True)
        u_b = jnp.broadcast_to(-jnp.maximum(pmax, m_prev), (n, LANES))
        b_b = jnp.broadcast_to(b_col, (n, LANES))
        w_inter = jnp.exp(m_prev + u_b)
        a_end = tot - off_row
        m_loc = jnp.max(a_end, axis=-1, keepdims=True)
        w_end = jnp.exp(a_end - m_loc)
        qh = q_ref[0, :, h * ML_DK:(h + 1) * ML_DK]
        kf = k_ref[0, :, h * ML_DK:(h + 1) * ML_DK].astype(F32) * (ML_DK ** -0.5)
        kb = kf.astype(BF16)
        vh = v_ref[0, :, h * ML_DV:(h + 1) * ML_DV]
        s_mat = _dot_nt(qh, kb) * jnp.exp(jnp.where(tri, jnp.tile(u_b, (1, n // LANES)) - off_row, -jnp.inf))
        s_hi, s_lo = _split_bf16(s_mat, 2)
        c_prev = c_st[h]
        n_prev = n_st[h]
        ones = jnp.ones((n, LANES), BF16)
        den = _dot(s_hi, ones) + _dot(s_lo, ones) + _dot(qh, n_prev.astype(BF16)) * w_inter
        inv = 1.0 / jnp.maximum(jnp.abs(den), jnp.exp(u_b - b_b))
        reps = ML_DV // LANES
        out = (_dot(s_hi, vh) * jnp.tile(inv, (1, reps))
               + _dot(qh, c_prev.astype(BF16)) * jnp.tile(w_inter * inv, (1, reps)))
        kw_t = (kf.T * w_end).astype(BF16)
        c_chunk = _dot(kw_t, vh)
        n_chunk = _dot(kw_t, ones)
        m_new = jnp.maximum(tot + m_prev, m_loc)
        a_sc = jnp.exp(tot + m_prev - m_new)
        b_sc = jnp.exp(m_loc - m_new)
        c_st[h] = c_prev * a_sc + c_chunk * b_sc
        n_st[h] = n_prev * a_sc + n_chunk * b_sc
        m_st[h] = jnp.broadcast_to(m_new, m_st.shape[1:])
        if rev:
            out = out + acc_ref[0, :, h * ML_DV:(h + 1) * ML_DV].astype(F32)
        o_ref[0, :, h * ML_DV:(h + 1) * ML_DV] = out.astype(o_ref.dtype)


def _mlstm_scan(q, k, v, gates, ib_row, fb_row, acc, *, rev, ctx_len):
    bsz, s, _ = q.shape
    n = ML_CHUNK
    n_all = s // n
    n_ctx = ctx_len // n
    cidx = functools.partial(_scan_chunk_index, rev=rev, n_ctx=n_ctx, n_all=n_all)
    tok = lambda b, c: (b, cidx(c), 0)
    qw, vw = ML_HEADS * ML_DK, ML_HEADS * ML_DV
    in_specs = [pl.BlockSpec((1, n, qw), tok), pl.BlockSpec((1, n, qw), tok),
                pl.BlockSpec((1, n, vw), tok), pl.BlockSpec((1, n, LANES), tok),
                pl.BlockSpec((1, LANES), lambda b, c: (0, 0)),
                pl.BlockSpec((1, LANES), lambda b, c: (0, 0))]
    args = [q, k, v, gates, ib_row, fb_row]
    if rev:
        in_specs.append(pl.BlockSpec((1, n, vw), tok))
        args.append(acc)
    return pl.pallas_call(
        functools.partial(_mlstm_kernel, rev=rev, d=1 if rev else 0),
        out_shape=jax.ShapeDtypeStruct((bsz, s, vw), F32),
        grid=(bsz, n_all), in_specs=in_specs,
        out_specs=pl.BlockSpec((1, n, vw), tok),
        scratch_shapes=[pltpu.VMEM((ML_HEADS, ML_DK, ML_DV), F32),
                        pltpu.VMEM((ML_HEADS, ML_DK, LANES), F32),
                        pltpu.VMEM((ML_HEADS, 8, LANES), F32)],
        compiler_params=_params(2), name="mlstm_scan_bwd" if rev else "mlstm_scan_fwd",
    )(*args)


def _post_kernel(*refs, mode, split_tiles, x_tile_off, n_tiles, n_steps):
    k = pl.program_id(0)
    tile = jnp.minimum(k, n_steps - 1) % n_tiles
    x, refs = _stream_tile(refs, split_tiles, tile + x_tile_off)
    lg_sc = refs[-1]
    refs = refs[:-1]
    cnt_ref = refs[-1]

    @pl.when(k == 0)
    def _():
        lg_sc[...] = jnp.zeros_like(lg_sc)
        cnt_ref[...] = jnp.zeros_like(cnt_ref)

    lg_prev = lg_sc[...]
    if mode == "mix":
        (y_ref, z_ref, hm_ref, og_ref, snw_ref, mnw_ref, wa_ref, wb_ref,
         mod_ref, n2_ref, wr_ref, br_ref, xo_ref, h2_ref, rf_ref, cnt_ref) = refs
        y = _rms(y_ref[0].astype(F32) * _silu(z_ref[0].astype(F32)), snw_ref[...])
        og = og_ref[0].astype(F32)
        o = _dot(y.astype(BF16), wa_ref[...])
        parts = []
        for h in range(ML_HEADS):
            sl = slice(h * ML_DV, (h + 1) * ML_DV)
            parts.append(_rms(hm_ref[0, :, sl], mnw_ref[:, sl]) * jax.nn.sigmoid(og[:, sl]))
        o = o + _dot(jnp.concatenate(parts, axis=-1).astype(BF16), wb_ref[...])
    else:
        (a_ref, wa_ref, mod_ref, n2_ref, wr_ref, br_ref, xo_ref, h2_ref, rf_ref, cnt_ref) = refs
        o = _dot(a_ref[0], wa_ref[...])
    x_new = x + mod_ref[0, 2:3, :] * o
    xo_ref[0] = x_new
    h2 = _rms(x_new, n2_ref[...]) * (1.0 + mod_ref[0, 4:5, :]) + mod_ref[0, 3:4, :]
    h2_ref[0] = h2
    lg_sc[...] = _dot_split(h2, wr_ref) + br_ref[...]
    rf_ref[0] = _route_fields(lg_prev, cnt_ref, jnp.where(k > 0, 1.0, 0.0))


def _post_call(mode, x, acts, rows, mats, mod_l, norm2_w, w_router, b_router, x_tile_off, n_tiles, ctx_tiles, ctx_row):
    tm = ROW_TILE
    x_args, in_specs, split_tiles = _stream_specs(x, tm, x_tile_off)
    bsz, d = x_args[0].shape[0], x_args[0].shape[2]
    s_out = n_tiles * tm
    tok = lambda b, i: (b, i, 0)
    const = lambda b, i: (0, 0)
    mod_idx = lambda b, i: (jnp.where(i + x_tile_off < ctx_tiles, ctx_row, b), 0, 0)
    in_specs += [pl.BlockSpec((1, tm, a.shape[2]), tok) for a in acts]
    in_specs += [pl.BlockSpec(r.shape, const) for r in rows]
    in_specs += [pl.BlockSpec(m.shape, const) for m in mats]
    in_specs += [pl.BlockSpec((1, 6, d), mod_idx), pl.BlockSpec((1, d), const),
                 pl.BlockSpec(w_router.shape, lambda b, i: (0, 0, 0)), pl.BlockSpec((1, LANES), const)]
    out_shape = [jax.ShapeDtypeStruct((bsz, s_out, d), F32),
                 jax.ShapeDtypeStruct((bsz, s_out, d), F32),
                 jax.ShapeDtypeStruct((bsz, s_out, LANES), F32),
                 jax.ShapeDtypeStruct((8, LANES), F32)]
    out_specs = [pl.BlockSpec((1, tm, d), tok), pl.BlockSpec((1, tm, d), tok),
                 pl.BlockSpec((1, tm, LANES), tok), pl.BlockSpec((8, LANES), const)]
    n_steps = bsz * n_tiles

    def flat(index_map, lag=0):
        def at(k):
            kk = jnp.clip(k - lag, 0, n_steps - 1)
            return index_map(kk // n_tiles, kk % n_tiles)
        return at

    in_specs = [pl.BlockSpec(sp.block_shape, flat(sp.index_map)) for sp in in_specs]
    out_specs = [pl.BlockSpec(sp.block_shape, flat(sp.index_map, lag=int(j == 2))) for j, sp in enumerate(out_specs)]
    return pl.pallas_call(
        functools.partial(_post_kernel, mode=mode, split_tiles=split_tiles, x_tile_off=x_tile_off,
                          n_tiles=n_tiles, n_steps=n_steps),
        out_shape=out_shape, grid=(n_steps + 1,), in_specs=in_specs, out_specs=out_specs,
        scratch_shapes=[pltpu.VMEM((tm, LANES), F32)],
        compiler_params=_params(1), name="post_" + mode,
    )(*x_args, *acts, *rows, *mats, mod_l, norm2_w.reshape(1, d), w_router, b_router)


def _router_kernel(f_ref, cnt_ref, route_ref, src_ref, off_ref, inv_ref, *, moe_tile, n_tokens):
    i = pl.program_id(0)
    tm = f_ref.shape[0]

    @pl.when(i == 0)
    def _():
        tiles = jnp.ceil(cnt_ref[...] * (1.0 / moe_tile))
        r = lax.broadcasted_iota(jnp.int32, (LANES, LANES), 0)
        c = lax.broadcasted_iota(jnp.int32, (LANES, LANES), 1)
        earlier = jnp.where(r < c, 1.0, 0.0).astype(BF16)
        off_ref[...] = _dot(tiles.astype(BF16), earlier) * float(moe_tile)

    f = f_ref[...]
    lane = lax.broadcasted_iota(jnp.int32, f.shape, 1).astype(F32)
    off = off_ref[0:1, :]
    pos1 = jnp.sum(jnp.where(lane == f[:, 0:1], off, 0.0), axis=-1, keepdims=True) + f[:, 4:5]
    pos2 = jnp.sum(jnp.where(lane == f[:, 1:2], off, 0.0), axis=-1, keepdims=True) + f[:, 5:6]
    route_ref[...] = jnp.where(lane == 6.0, pos1, jnp.where(lane == 7.0, pos2, f))

    n_blk = inv_ref.shape[0]
    pos_t = jnp.where(lane == 0.0, pos1, jnp.where(lane == 1.0, pos2, 0.0)).T
    blk = lax.broadcasted_iota(jnp.int32, (n_blk, tm), 0).astype(F32)
    tok = (i * tm + lax.broadcasted_iota(jnp.int32, (tm, 1), 0)).astype(F32)
    tok_hi = jnp.floor(tok * (1.0 / LANES))
    tok_lo = tok - tok_hi * LANES
    lhs, rhs = [], []
    for k, pos in enumerate((pos1, pos2)):
        blk_of = jnp.floor(pos_t[k:k + 1, :] * (1.0 / LANES))
        lhs.append(jnp.where(blk == blk_of, 1.0, 0.0).astype(BF16))
        hit = lane == pos - jnp.floor(pos * (1.0 / LANES)) * LANES
        rhs.append(jnp.concatenate([jnp.where(hit, tok_hi, 0.0), jnp.where(hit, tok_lo, 0.0),
                                    jnp.where(hit, 1.0, 0.0)], axis=1).astype(BF16))
    upd = _dot(jnp.concatenate(lhs, axis=1), jnp.concatenate(rhs, axis=0))

    @pl.when(i == 0)
    def _():
        inv_ref[...] = upd

    @pl.when(i > 0)
    def _():
        inv_ref[...] = inv_ref[...] + upd

    @pl.when(i == pl.num_programs(0) - 1)
    def _():
        acc = inv_ref[...]
        n_blk = acc.shape[0]
        slot = (lax.broadcasted_iota(jnp.int32, (n_blk, LANES), 0) * LANES
                + lax.broadcasted_iota(jnp.int32, (n_blk, LANES), 1)).astype(F32)
        spare = slot - n_tokens * jnp.floor((slot + 0.5) * (1.0 / n_tokens))
        src = jnp.where(acc[:, 2 * LANES:] > 0.0, acc[:, :LANES] * LANES + acc[:, LANES:2 * LANES], spare)
        src_ref[...] = src.astype(jnp.int32)


def _route_fields(lg, cnt_ref, live):
    tm = lg.shape[0]
    lane = lax.broadcasted_iota(jnp.int32, lg.shape, 1).astype(F32)
    big = float(LANES)
    is_g = (lane >= ROUTE_G) & (lane < ROUTE_G + MOE_GROUPS)
    lgg = jnp.where(is_g, lg, -jnp.inf)
    g_max = jnp.max(lgg, axis=-1, keepdims=True)
    g_idx = jnp.min(jnp.where(lgg == g_max, lane - ROUTE_G, big), axis=-1, keepdims=True)
    g_prob = 1.0 / jnp.sum(jnp.exp(lgg - g_max), axis=-1, keepdims=True)
    lo = g_idx * MOE_EXPERTS
    le = jnp.where((lane >= lo) & (lane < lo + MOE_EXPERTS), lg, -jnp.inf)
    l1 = jnp.max(le, axis=-1, keepdims=True)
    i1 = jnp.min(jnp.where(le == l1, lane, big), axis=-1, keepdims=True)
    le2 = jnp.where(lane == i1, -jnp.inf, le)
    l2 = jnp.max(le2, axis=-1, keepdims=True)
    i2 = jnp.min(jnp.where(le2 == l2, lane, big), axis=-1, keepdims=True)
    r = jnp.exp(l2 - l1)
    w1 = g_prob / (1.0 + r)
    w2 = w1 * r
    oh1 = jnp.where(lane == i1, 1.0, 0.0)
    oh2 = jnp.where(lane == i2, 1.0, 0.0)
    oh = oh1 + oh2
    row = lax.broadcasted_iota(jnp.int32, (tm, tm), 0)
    col = lax.broadcasted_iota(jnp.int32, (tm, tm), 1)
    before = jnp.where(col < row, 1.0, 0.0).astype(BF16)
    prefix = _dot(before, oh.astype(BF16)) + cnt_ref[0:1, :]
    rank1 = jnp.sum(prefix * oh1, axis=-1, keepdims=True)
    rank2 = jnp.sum(prefix * oh2, axis=-1, keepdims=True)
    cnt_ref[...] = cnt_ref[...] + jnp.sum(oh, axis=0, keepdims=True) * live
    fields = (i1, i2, w1, w2, rank1, rank2)
    out = jnp.zeros_like(lg)
    for j, f in enumerate(fields):
        out = jnp.where(lane == float(j), f, out)
    return out


def _router(fields, counts, n_sorted):
    t = fields.shape[0]
    tm = ROW_TILE
    n_blk = n_sorted // LANES
    fixed = lambda i: (0, 0)
    return pl.pallas_call(
        functools.partial(_router_kernel, moe_tile=MOE_TILE, n_tokens=t),
        out_shape=[jax.ShapeDtypeStruct((t, LANES), F32), jax.ShapeDtypeStruct((n_blk, LANES), jnp.int32)],
        grid=(t // tm,),
        in_specs=[pl.BlockSpec((tm, LANES), lambda i: (i, 0)), pl.BlockSpec((8, LANES), fixed)],
        out_specs=[pl.BlockSpec((tm, LANES), lambda i: (i, 0)), pl.BlockSpec((n_blk, LANES), fixed)],
        scratch_shapes=[pltpu.VMEM((8, LANES), F32), pltpu.VMEM((n_blk, 3 * LANES), F32)],
        compiler_params=_params(1), name="router",
    )(fields, counts)


def _moe_kernel(te_ref, nt_ref, x_ref, wg_ref, wu_ref, wd_ref, o_ref, gate_bf, up_bf, down_bf):
    i = pl.program_id(0)
    valid = i < nt_ref[0]
    fresh = (i == 0) | (te_ref[i] != te_ref[jnp.maximum(i - 1, 0)])

    @pl.when(valid & fresh)
    def _():
        gate_bf[...] = wg_ref[0].astype(BF16)
        up_bf[...] = wu_ref[0].astype(BF16)
        down_bf[...] = wd_ref[0].astype(BF16)

    @pl.when(valid)
    def _():
        x = x_ref[...].astype(BF16)
        act = _silu(_dot(x, gate_bf[...])) * _dot(x, up_bf[...])
        o_ref[...] = _dot(act.astype(BF16), down_bf[...]).astype(o_ref.dtype)

    @pl.when(jnp.logical_not(valid))
    def _():
        o_ref[...] = jnp.zeros_like(o_ref)


def _moe_experts(x_sorted, tile_expert, n_tiles_used, wg, wu, wd):
    tm = MOE_TILE
    rows, d = x_sorted.shape
    ff = wg.shape[2]
    grid_spec = pltpu.PrefetchScalarGridSpec(
        num_scalar_prefetch=2, grid=(rows // tm,),
        in_specs=[pl.BlockSpec((tm, d), lambda i, te, nt: (i, 0)),
                  pl.BlockSpec((1, d, ff), lambda i, te, nt: (te[i], 0, 0)),
                  pl.BlockSpec((1, d, ff), lambda i, te, nt: (te[i], 0, 0)),
                  pl.BlockSpec((1, ff, d), lambda i, te, nt: (te[i], 0, 0))],
        out_specs=pl.BlockSpec((tm, d), lambda i, te, nt: (i, 0)),
        scratch_shapes=[pltpu.VMEM((d, ff), BF16), pltpu.VMEM((d, ff), BF16), pltpu.VMEM((ff, d), BF16)])
    return pl.pallas_call(
        _moe_kernel, out_shape=jax.ShapeDtypeStruct((rows, d), F32), grid_spec=grid_spec,
        compiler_params=_params(1), name="moe_experts",
    )(tile_expert, n_tiles_used, x_sorted, wg, wu, wd)


def _combine_kernel(x_ref, y1_ref, y2_ref, rt_ref, mod_ref, o_ref):
    f = rt_ref[0, :, 2:3] * y1_ref[0] + rt_ref[0, :, 3:4] * y2_ref[0]
    o_ref[0] = x_ref[0] + mod_ref[0, 5:6, :] * f


def _combine(x, y1, y2, route, mod_l, ctx_tiles, ctx_row):
    bsz, s, d = x.shape
    tm = ROW_TILE
    tok = lambda b, i: (b, i, 0)
    mod_idx = lambda b, i: (jnp.where(i < ctx_tiles, ctx_row, b), 0, 0)
    return pl.pallas_call(
        _combine_kernel, out_shape=jax.ShapeDtypeStruct(x.shape, F32), grid=(bsz, s // tm),
        in_specs=[pl.BlockSpec((1, tm, d), tok), pl.BlockSpec((1, tm, d), tok), pl.BlockSpec((1, tm, d), tok),
                  pl.BlockSpec((1, tm, LANES), tok), pl.BlockSpec((1, 6, d), mod_idx)],
        out_specs=pl.BlockSpec((1, tm, d), tok),
        compiler_params=_params(2), name="moe_combine",
    )(x, y1.reshape(x.shape), y2.reshape(x.shape), route.reshape(bsz, s, LANES), mod_l)


def _hier_moe(h2, fields, counts, layer, wg, wu, wd):
    bsz, s, d = h2.shape
    t = bsz * s
    tm = MOE_TILE
    n_tiles = 2 * t // tm + N_EXPERTS
    route, src = _router(fields.reshape(t, LANES), counts, n_tiles * tm)
    tiles_per = (counts[0, :N_EXPERTS].astype(jnp.int32) + tm - 1) // tm
    tile_end = jnp.cumsum(tiles_per)
    tile_ids = jnp.arange(n_tiles, dtype=jnp.int32)
    tile_expert = jnp.minimum(jnp.sum((tile_end[None, :] <= tile_ids[:, None]).astype(jnp.int32), axis=1),
                              N_EXPERTS - 1) + layer * N_EXPERTS
    pos = route[:, 6:8].astype(jnp.int32)
    x_sorted = jnp.take(h2.reshape(t, d), src.reshape(-1), axis=0, mode="clip")
    y_sorted = _moe_experts(x_sorted, tile_expert, tile_end[-1:], wg.reshape(-1, d, wg.shape[-1]),
                            wu.reshape(-1, d, wu.shape[-1]), wd.reshape(-1, wd.shape[-2], d))
    y1 = jnp.take(y_sorted, pos[:, 0], axis=0, mode="clip")
    y2 = jnp.take(y_sorted, pos[:, 1], axis=0, mode="clip")
    return y1, y2, route


def _attn_kernel(q_ref, k_ref, v_ref, bias_ref, qw_ref, kw_ref, o_ref, qn_ref, kn_ref, *, ctx_len, rows):
    lane = lax.broadcasted_iota(jnp.int32, (1, LANES), 1)
    first = lane < NA_HEAD_DIM

    r_head = lax.broadcasted_iota(jnp.int32, (LANES, LANES), 0) // NA_HEAD_DIM
    c_head = lax.broadcasted_iota(jnp.int32, (LANES, LANES), 1) // NA_HEAD_DIM
    same_head = jnp.where(r_head == c_head, 1.0, 0.0).astype(BF16)

    def head_norm(x, w):
        ms = _dot((x * x).astype(BF16), same_head) * (1.0 / NA_HEAD_DIM)
        return x * lax.rsqrt(ms + RMS_EPS) * w

    kn_ref[...] = head_norm(k_ref[0].astype(F32), kw_ref[...]).astype(BF16)
    qn_ref[...] = (head_norm(q_ref[0, ctx_len:, :].astype(F32), qw_ref[...]) * NA_HEAD_DIM ** -0.5).astype(BF16)
    n_groups = rows // ATTN_GROUP_ROWS
    n_q = ATTN_GROUP_ROWS * GRID_W
    n_loc = ATTN_KEY_ROWS * GRID_W
    k_ctx = kn_ref[0:ctx_len, :]
    v_ctx = v_ref[0, 0:ctx_len, :]

    def one_group(g):
        kind = jnp.where(g == 0, 0, jnp.where(g == n_groups - 1, 2, 1))
        kr0 = jnp.clip(g * ATTN_GROUP_ROWS - NA_KH // 2, 0, rows - ATTN_KEY_ROWS)
        q_rows = pl.ds(pl.multiple_of(g * n_q, n_q), n_q)
        q = qn_ref[q_rows, :]
        zero = jnp.zeros_like(q)
        q2 = jnp.concatenate([jnp.where(first, q, zero), jnp.where(first, zero, q)], axis=0)
        k_off = pl.multiple_of(ctx_len + kr0 * GRID_W, GRID_W)
        s_loc = _dot_nt(q2, kn_ref[pl.ds(k_off, n_loc), :])
        s_ctx = _dot_nt(q2, k_ctx)
        p_loc, p_ctx, inv = [], [], []
        for hh in range(2):
            sl = s_loc[hh * n_q:(hh + 1) * n_q] + bias_ref[hh, kind]
            sc = s_ctx[hh * n_q:(hh + 1) * n_q]
            m = jnp.maximum(jnp.max(sl, axis=-1, keepdims=True), jnp.max(sc, axis=-1, keepdims=True))
            el = jnp.exp(sl - m)
            ec = jnp.exp(sc - m)
            inv.append(1.0 / (jnp.sum(el, axis=-1, keepdims=True) + jnp.sum(ec, axis=-1, keepdims=True)))
            p_loc.append(el.astype(BF16))
            p_ctx.append(ec.astype(BF16))
        o = (_dot(jnp.concatenate(p_loc, axis=0), v_ref[0, pl.ds(k_off, n_loc), :])
             + _dot(jnp.concatenate(p_ctx, axis=0), v_ctx))
        o_ref[0, q_rows, :] = jnp.where(first, o[:n_q] * inv[0], o[n_q:] * inv[1]).astype(o_ref.dtype)

    def body(i, carry):
        for j in range(ATTN_GROUPS_PER_TRIP):
            one_group(i * ATTN_GROUPS_PER_TRIP + j)
        return carry

    lax.fori_loop(0, n_groups // ATTN_GROUPS_PER_TRIP, body, 0)


def _attn_group_layout(rows):
    n_groups = rows // ATTN_GROUP_ROWS
    assert rows % ATTN_GROUP_ROWS == 0 and rows >= ATTN_KEY_ROWS and n_groups >= 2
    u = np.arange(ATTN_GROUP_ROWS)[:, None]
    i = np.arange(ATTN_KEY_ROWS)[None, :]

    def layout(g):
        r = g * ATTN_GROUP_ROWS + u
        r0 = np.clip(r - NA_KH // 2, 0, rows - NA_KH)
        kr = np.clip(g * ATTN_GROUP_ROWS - NA_KH // 2, 0, rows - ATTN_KEY_ROWS) + i
        return (kr >= r0) & (kr < r0 + NA_KH), kr - r + NA_KH - 1

    kinds = [layout(0), layout(1), layout(n_groups - 1)]
    for g in range(1, n_groups - 1):
        valid, d = layout(g)
        assert (valid == kinds[1][0]).all() and (d[valid] == kinds[1][1][valid]).all()
    return np.stack([k[0] for k in kinds]), np.stack([k[1] for k in kinds])


def _bias_windows(rpb, rows):
    qc = np.arange(GRID_W)
    c0 = np.clip(qc - NA_KW // 2, 0, GRID_W - NA_KW)
    kc = np.arange(GRID_W)
    inwin = (kc[None, :] >= c0[:, None]) & (kc[None, :] < c0[:, None] + NA_KW)
    coff = kc[None, :] - qc[:, None] + NA_KW - 1
    pick = (coff[..., None] == np.arange(2 * NA_KW - 1)) & inwin[..., None]
    tab = jnp.einsum("hdo,qko->dhqk", rpb.astype(F32), jnp.asarray(pick, F32), precision=HIGHEST)
    tab = jnp.where(inwin[None, None], tab, NEG)
    valid, d = _attn_group_layout(rows)
    masked = jnp.full(tab.shape[1:], NEG, F32)
    blocks = []
    for kind in range(valid.shape[0]):
        for u in range(ATTN_GROUP_ROWS):
            blocks.append(jnp.concatenate(
                [tab[int(d[kind, u, i])] if valid[kind, u, i] else masked for i in range(ATTN_KEY_ROWS)], axis=-1))
    win = jnp.stack(blocks, axis=1)
    return win.reshape(rpb.shape[0], valid.shape[0], ATTN_GROUP_ROWS * GRID_W, ATTN_KEY_ROWS * GRID_W)


def _neighbourhood_attention(q, k, v, bias, qn_w, kn_w, ctx_len):
    bsz, s, w = q.shape
    seq = s - ctx_len
    rows = seq // GRID_W
    n_pairs = NA_HEADS // 2
    pair = lambda b, p: (b, 0, p)
    row2 = lambda b, p: (0, 0)
    return pl.pallas_call(
        functools.partial(_attn_kernel, ctx_len=ctx_len, rows=rows),
        out_shape=jax.ShapeDtypeStruct((bsz, seq, w), BF16),
        grid=(bsz, n_pairs),
        in_specs=[pl.BlockSpec((1, s, LANES), pair), pl.BlockSpec((1, s, LANES), pair),
                  pl.BlockSpec((1, s, LANES), pair),
                  pl.BlockSpec((2,) + bias.shape[1:], lambda b, p: (p, 0, 0, 0)),
                  pl.BlockSpec((1, LANES), row2), pl.BlockSpec((1, LANES), row2)],
        out_specs=pl.BlockSpec((1, seq, LANES), pair),
        scratch_shapes=[pltpu.VMEM((seq, LANES), BF16), pltpu.VMEM((s, LANES), BF16)],
        compiler_params=_params(2), name="neighbourhood_attention",
    )(q, k, v, bias, jnp.tile(qn_w, 2).reshape(1, LANES), jnp.tile(kn_w, 2).reshape(1, LANES))


def _place_cols(pieces, width=LANES):
    lead = pieces[0][1].shape[:-1]
    out, at = [], 0
    for off, a in pieces:
        if off > at:
            out.append(jnp.zeros(lead + (off - at,), F32))
        out.append(a.astype(F32))
        at = off + a.shape[-1]
    if at < width:
        out.append(jnp.zeros(lead + (width - at,), F32))
    return jnp.concatenate(out, axis=-1)


def _pad_row(pieces, width=LANES):
    return _place_cols(pieces, width).reshape(1, width)


def _router_params(w_group, b_group, w_expert, b_expert):
    w = _place_cols([(0, w_expert), (ROUTE_G, w_group)])
    return _split_weight(w), _pad_row([(0, b_expert), (ROUTE_G, b_group)])


def kernel(x, c, ctx, c_ctx, norm1_w, norm2_w, mod_w, mod_b, ab_w_in, ab_conv_w, ab_conv_b, ssd_a_log, ssd_dt_bias, ssd_d, ssd_norm_w, ml_i_bias, ml_f_bias, ml_norm_w, ab_w_out, na_w_qkv, na_q_norm, na_k_norm, na_rpb, na_w_out, moe_w_group, moe_b_group, moe_w_expert, moe_b_expert, moe_w_gate, moe_w_up, moe_w_down):
    bsz, seq, d = x.shape
    ctx_len = ctx.shape[1]
    depth = mod_w.shape[0]
    assert depth == 2 and ctx_len % ROW_TILE == 0 and seq % ROW_TILE == 0 and bsz < 8
    ctx_tiles = ctx_len // ROW_TILE
    lat_tiles = seq // ROW_TILE
    ctx_row = bsz

    cvec = jnp.concatenate([c, c_ctx[None], jnp.zeros((7 - bsz, d), c.dtype)], axis=0).astype(F32)
    mod = _mod_vectors(cvec, mod_w, mod_b)
    xs = (ctx, x)

    ssd_w = SSD_HEADS * SSD_HEAD_DIM
    xbc_w = ssd_w + 2 * SSD_GROUPS * SSD_STATE
    qk_w, v_w = ML_HEADS * ML_DK, ML_HEADS * ML_DV
    sizes = (ssd_w, xbc_w, 2 * SSD_HEADS, qk_w, qk_w, v_w, v_w, 2 * ML_HEADS, 2 * ML_HEADS)
    w_z, w_xbc, w_dt, w_q, w_k, w_v, w_o, w_i, w_f = jnp.split(ab_w_in[0], np.cumsum(sizes)[:-1].tolist(), axis=1)
    w_gate = _place_cols([(GATE_DT, w_dt), (GATE_I, w_i), (GATE_F, w_f)])
    weights = [w.astype(BF16) for w in (w_z, w_xbc, w_q, w_k, w_v, w_o)] + [_split_weight(w_gate)]
    z, xbc, q, k, v, og, gates = _norm_mod_matmul(xs, norm1_w[0], mod[0], weights, [BF16] * 6 + [F32],
                                                  ctx_tiles, ctx_row)
    xbc = _conv_silu(xbc, ab_conv_w[0], ab_conv_b[0], ctx_len)
    a_neg = -jnp.exp(ssd_a_log[0].astype(F32))
    dsk_row = jnp.repeat(ssd_d[0].astype(F32), SSD_HEAD_DIM).reshape(1, ssd_w)
    y = None
    hm = None
    for dr in range(2):
        rev = dr == 1
        dtb_row = _pad_row([(GATE_DT + dr * SSD_HEADS, ssd_dt_bias[0, dr])])
        a_row = _pad_row([(GATE_DT + dr * SSD_HEADS, a_neg[dr])])
        y = _ssd_scan(xbc, gates, dtb_row, a_row, y if rev else dsk_row, rev=rev, ctx_len=ctx_len)
        ib_row = _pad_row([(GATE_I + dr * ML_HEADS, ml_i_bias[0, dr])])
        fb_row = _pad_row([(GATE_F + dr * ML_HEADS, ml_f_bias[0, dr])])
        hm = _mlstm_scan(q, k, v, gates, ib_row, fb_row, hm, rev=rev, ctx_len=ctx_len)
    w_r, b_r = _router_params(moe_w_group[0], moe_b_group[0], moe_w_expert[0], moe_b_expert[0])
    w_out = ab_w_out[0].astype(BF16)
    x1, h2, fields, counts = _post_call(
        "mix", xs, [y, z, hm, og],
        [ssd_norm_w[0].reshape(1, ssd_w), ml_norm_w[0].reshape(1, v_w)], [w_out[:ssd_w], w_out[ssd_w:]],
        mod[0], norm2_w[0], w_r, b_r, 0, ctx_tiles + lat_tiles, ctx_tiles, ctx_row)
    moe0 = _hier_moe(h2, fields, counts, 0, moe_w_gate, moe_w_up, moe_w_down)

    w_qkv = na_w_qkv[0].astype(BF16)
    na_w = NA_HEADS * NA_HEAD_DIM
    q, k, v, xs = _norm_mod_matmul(x1, norm1_w[1], mod[1],
                                   [w_qkv[:, :na_w], w_qkv[:, na_w:2 * na_w], w_qkv[:, 2 * na_w:]],
                                   [BF16] * 3, ctx_tiles, ctx_row, pending_moe=(*moe0, mod[0]))
    bias = _bias_windows(na_rpb[0], seq // GRID_W)
    attn = _neighbourhood_attention(q, k, v, bias, na_q_norm[0], na_k_norm[0], ctx_len)
    w_r, b_r = _router_params(moe_w_group[1], moe_b_group[1], moe_w_expert[1], moe_b_expert[1])
    x1, h2, fields, counts = _post_call("attn", xs, [attn], [], [na_w_out[0].astype(BF16)],
                                        mod[1], norm2_w[1], w_r, b_r, ctx_tiles, lat_tiles, ctx_tiles, ctx_row)
    y1, y2, route = _hier_moe(h2, fields, counts, 1, moe_w_gate, moe_w_up, moe_w_down)
    return _combine(x1, y1, y2, route, mod[1], 0, ctx_row)
```

```python
import functools

import numpy as np
import jax
import jax.numpy as jnp
from jax import lax
from jax.experimental import pallas as pl
from jax.experimental.pallas import tpu as pltpu

F32 = jnp.float32
BF16 = jnp.bfloat16
HIGHEST = lax.Precision.HIGHEST

RMS_EPS = 1e-6
GRID_W = 64
SSD_HEADS = 16
SSD_HEAD_DIM = 64
SSD_GROUPS = 2
SSD_STATE = 128
ML_HEADS = 4
ML_DK = 128
ML_DV = 256
NA_HEADS = 16
NA_HEAD_DIM = 64
NA_KH = 8
NA_KW = 16
MOE_GROUPS = 4
MOE_EXPERTS = 8
N_EXPERTS = MOE_GROUPS * MOE_EXPERTS

LANES = 128
ROW_TILE = 256
SCAN_CHUNK = 128
SSD_BLOCK_CHUNKS = 2
ML_CHUNK = 256
MOE_TILE = 256
ATTN_GROUP_ROWS = 4
ATTN_KEY_ROWS = ATTN_GROUP_ROWS + NA_KH - 1
ATTN_GROUPS_PER_TRIP = 16
VMEM_LIMIT = 56 * 1024 * 1024

GATE_DT = 0
GATE_I = 2 * SSD_HEADS
GATE_F = GATE_I + 2 * ML_HEADS
ROUTE_G = N_EXPERTS
NEG = -1e30


def _params(n_axes):
    return pltpu.CompilerParams(dimension_semantics=("arbitrary",) * n_axes,
                                vmem_limit_bytes=VMEM_LIMIT)


def _silu(x):
    return x * jax.nn.sigmoid(x)


def _softplus(x):
    return jnp.maximum(x, 0.0) + jnp.log1p(jnp.exp(-jnp.abs(x)))


def _rms(x, w):
    return x * lax.rsqrt(jnp.mean(x * x, axis=-1, keepdims=True) + RMS_EPS) * w


def _dot(a, b):
    return jnp.dot(a, b, preferred_element_type=F32)


def _dot_nt(a, b):
    return lax.dot_general(a, b, (((1,), (1,)), ((), ())), preferred_element_type=F32)


def _dot_hi(a, b):
    return jnp.dot(a, b, precision=HIGHEST, preferred_element_type=F32)


def _split_bf16(x, terms):
    parts = []
    for _ in range(terms - 1):
        p = x.astype(BF16)
        parts.append(p)
        x = x - p.astype(F32)
    parts.append(x.astype(BF16))
    return parts


def _split_weight(w):
    return jnp.stack(_split_bf16(w.astype(F32), 2))


def _dot_split(a, w2_ref):
    a_hi, a_lo = _split_bf16(a, 2)
    return _dot(a_hi, w2_ref[0]) + _dot(a_lo, w2_ref[0]) + _dot(a_hi, w2_ref[1])


def _cumsum_dot(tri, x):
    tri = jnp.where(tri, 1.0, 0.0).astype(BF16)
    hi, mid, lo = _split_bf16(x, 3)
    return _dot(tri, hi) + _dot(tri, mid) + _dot(tri, lo)


def _mod_kernel(c_ref, w_ref, b_ref, o_ref):
    o_ref[0] = _dot_hi(_silu(c_ref[...]), w_ref[0]) + b_ref[0]


def _mod_vectors(cvec, mod_w, mod_b, tn=512):
    depth, d, n = mod_w.shape
    rows = cvec.shape[0]
    out = pl.pallas_call(
        _mod_kernel,
        out_shape=jax.ShapeDtypeStruct((depth, rows, n), F32),
        grid=(depth, n // tn),
        in_specs=[pl.BlockSpec((rows, d), lambda l, j: (0, 0)),
                  pl.BlockSpec((1, d, tn), lambda l, j: (l, 0, j)),
                  pl.BlockSpec((1, 1, tn), lambda l, j: (l, 0, j))],
        out_specs=pl.BlockSpec((1, rows, tn), lambda l, j: (l, 0, j)),
        compiler_params=_params(2),
        name="mod_vectors",
    )(cvec, mod_w, mod_b.reshape(depth, 1, n))
    return out.reshape(depth, rows, 6, d)


def _stream_tile(refs, split_tiles, tile):
    if not split_tiles:
        return refs[0][0], refs[1:]
    return jnp.where(tile < split_tiles, refs[0][0], refs[1][0]), refs[2:]


def _stream_specs(xs, tm, offset=0):
    if not isinstance(xs, tuple):
        return [xs], [pl.BlockSpec((1, tm, xs.shape[2]), lambda b, i: (b, i + offset, 0))], 0
    ctx, lat = xs
    split = ctx.shape[1] // tm
    d = ctx.shape[2]
    return ([ctx, lat],
            [pl.BlockSpec((1, tm, d), lambda b, i: (b, jnp.minimum(i + offset, split - 1), 0)),
             pl.BlockSpec((1, tm, d), lambda b, i: (b, jnp.maximum(i + offset - split, 0), 0))], split)


def _nmm_kernel(*refs, n_out, pending_moe, split_tiles):
    x, refs = _stream_tile(refs, split_tiles, pl.program_id(1))
    if pending_moe:
        y1_ref, y2_ref, rt_ref, pmod_ref = refs[:4]
        refs = refs[4:]
        x = x + pmod_ref[0, 5:6, :] * (rt_ref[0, :, 2:3] * y1_ref[0] + rt_ref[0, :, 3:4] * y2_ref[0])
        refs[-1][0] = x
        refs = refs[:-1]
    nw_ref, mod_ref = refs[:2]
    refs = refs[2:]
    w_refs, o_refs = refs[:n_out], refs[n_out:]
    h = _rms(x, nw_ref[...])
    h = h * (1.0 + mod_ref[0, 1:2, :]) + mod_ref[0, 0:1, :]
    hb = h.astype(BF16)
    for w_ref, o_ref in zip(w_refs, o_refs):
        if len(w_ref.shape) == 3:
            o_ref[0] = _dot_split(h, w_ref)
        else:
            o_ref[0] = _dot(hb, w_ref[...]).astype(o_ref.dtype)


def _norm_mod_matmul(xs, norm_w, mod_l, weights, out_dtypes, ctx_tiles, ctx_row, pending_moe=None):
    tm = ROW_TILE
    args, in_specs, split_tiles = _stream_specs(xs, tm)
    bsz, d = args[0].shape[0], args[0].shape[2]
    s = sum(a.shape[1] for a in args)
    tok = lambda b, i: (b, i, 0)
    mod_idx = lambda b, i: (jnp.where(i < ctx_tiles, ctx_row, b), 0, 0)
    if pending_moe is not None:
        y1, y2, route, mod_prev = pending_moe
        args += [y1.reshape(bsz, s, d), y2.reshape(bsz, s, d), route.reshape(bsz, s, LANES), mod_prev]
        in_specs += [pl.BlockSpec((1, tm, d), tok), pl.BlockSpec((1, tm, d), tok),
                     pl.BlockSpec((1, tm, LANES), tok), pl.BlockSpec((1, 6, d), mod_idx)]
    args += [norm_w.reshape(1, d), mod_l, *weights]
    in_specs += [pl.BlockSpec((1, d), lambda b, i: (0, 0)), pl.BlockSpec((1, 6, d), mod_idx)]
    in_specs += [pl.BlockSpec(w.shape, lambda b, i, nd=w.ndim: (0,) * nd) for w in weights]
    out_shape = [jax.ShapeDtypeStruct((bsz, s, w.shape[-1]), dt) for w, dt in zip(weights, out_dtypes)]
    out_specs = [pl.BlockSpec((1, tm, w.shape[-1]), tok) for w in weights]
    if pending_moe is not None:
        out_shape.append(jax.ShapeDtypeStruct((bsz, s, d), F32))
        out_specs.append(pl.BlockSpec((1, tm, d), tok))
    return pl.pallas_call(
        functools.partial(_nmm_kernel, n_out=len(weights), pending_moe=pending_moe is not None,
                          split_tiles=split_tiles),
        out_shape=out_shape, grid=(bsz, s // tm), in_specs=in_specs, out_specs=out_specs,
        compiler_params=_params(2), name="norm_mod_matmul",
    )(*args)


def _conv_kernel(u_ref, w_ref, b_ref, o_ref, *, ctx_len):
    u = u_ref[0].astype(F32)
    s = u.shape[0]
    t = lax.broadcasted_iota(jnp.int32, u.shape, 0)
    prev = jnp.where((t == 0) | (t == ctx_len), 0.0, pltpu.roll(u, 1, axis=0))
    nxt = jnp.where((t == ctx_len - 1) | (t == s - 1), 0.0, pltpu.roll(u, s - 1, axis=0))
    y = prev * w_ref[0:1, :] + u * w_ref[1:2, :] + nxt * w_ref[2:3, :] + b_ref[...]
    o_ref[0] = _silu(y).astype(o_ref.dtype)


def _conv_silu(u, conv_w, conv_b, ctx_len, tc=2 * LANES):
    bsz, s, ch = u.shape
    return pl.pallas_call(
        functools.partial(_conv_kernel, ctx_len=ctx_len),
        out_shape=jax.ShapeDtypeStruct(u.shape, u.dtype),
        grid=(bsz, ch // tc),
        in_specs=[pl.BlockSpec((1, s, tc), lambda b, j: (b, 0, j)),
                  pl.BlockSpec((3, tc), lambda b, j: (0, j)),
                  pl.BlockSpec((1, tc), lambda b, j: (0, j))],
        out_specs=pl.BlockSpec((1, s, tc), lambda b, j: (b, 0, j)),
        compiler_params=_params(2), name="conv_silu",
    )(u, conv_w, conv_b.reshape(1, ch))


def _scan_chunk_index(c, rev, n_ctx, n_all):
    if not rev:
        return c
    return jnp.where(c < n_ctx, n_ctx - 1 - c, n_ctx + n_all - 1 - c)


def _tri(n, rev):
    row = lax.broadcasted_iota(jnp.int32, (n, n), 0)
    col = lax.broadcasted_iota(jnp.int32, (n, n), 1)
    return (col >= row) if rev else (col <= row)


def _mixer_scan_kernel(xs_ref, bc_ref, g_ref, dtb_ref, a_ref, ex_ref, ssd_extra_ref, q_ref, k_ref, v_ref, ib_ref, fb_ref,
                       *refs, rev, off, d):
    y_ref, hm_ref, st_ref, c_st, n_st, m_st = refs[-6:]

    @pl.when(pl.program_id(1) == 0)
    def _():
        for ref in (st_ref, c_st, n_st, m_st):
            ref[...] = jnp.zeros_like(ref)

    _ssd_block((xs_ref, bc_ref, g_ref, dtb_ref, a_ref, ex_ref, ssd_extra_ref, y_ref, st_ref), rev, off)
    ml_acc = refs[:-6]
    _mlstm_block((q_ref, k_ref, v_ref, g_ref, ib_ref, fb_ref, *ml_acc, hm_ref, c_st, n_st, m_st), rev, d)


def _ssd_block(refs, rev, off):
    xs_ref = refs[0]
    n = SCAN_CHUNK
    n_sub = xs_ref.shape[1] // n
    for sub in (range(n_sub - 1, -1, -1) if rev else range(n_sub)):
        _ssd_chunk(refs, slice(sub * n, (sub + 1) * n), rev, off)


def _ssd_chunk(refs, rows, rev, off):
    if rev:
        xs_ref, bc_ref, g_ref, dtb_ref, a_ref, ex_ref, acc_ref, o_ref, st_ref = refs
    else:
        xs_ref, bc_ref, g_ref, dtb_ref, a_ref, ex_ref, dsk_ref, o_ref, st_ref = refs
    n = SCAN_CHUNK
    last = 0 if rev else n - 1
    tri = _tri(n, rev)
    dt = _softplus(g_ref[0, rows, :] + dtb_ref[...])
    log_a = dt * a_ref[...]
    cs = _cumsum_dot(tri, log_a)
    cs_t = cs.T
    dt_hi = dt.astype(BF16)
    dt_lo = (dt - dt_hi.astype(F32)).astype(BF16)
    dt_full = _dot(dt_hi, ex_ref[...]) + _dot(dt_lo, ex_ref[...])
    xs = xs_ref[0, rows, :].astype(F32)
    xdt = (xs * dt_full).astype(BF16)
    lo_half = lax.broadcasted_iota(jnp.int32, (1, LANES), 1) < SSD_HEAD_DIM
    gw = SSD_GROUPS * SSD_STATE
    heads_per_group = SSD_HEADS // SSD_GROUPS
    for g in range(SSD_GROUPS):
        b_g = bc_ref[0, rows, g * SSD_STATE:(g + 1) * SSD_STATE]
        c_g = bc_ref[0, rows, gw + g * SSD_STATE:gw + (g + 1) * SSD_STATE]
        cb = _dot_nt(c_g, b_g)
        b_t = b_g.astype(F32).T
        for e in range(0, heads_per_group, 2):
            h0 = g * heads_per_group + e
            pair = h0 // 2
            sl = slice(pair * LANES, (pair + 1) * LANES)
            x_pair = xdt[:, sl]
            zero = jnp.zeros_like(x_pair)
            y, upd, a_bc, tots = None, None, [], []
            for j in range(2):
                col = off + h0 + j
                x_j = jnp.where(lo_half, x_pair, zero) if j == 0 else jnp.where(lo_half, zero, x_pair)
                a_b = jnp.broadcast_to(cs[:, col:col + 1], (n, LANES))
                a_row = cs_t[col:col + 1, :]
                tot = cs_t[col:col + 1, last:last + 1]
                decay = jnp.exp(jnp.where(tri, a_b - a_row, -jnp.inf))
                y_j = _dot((cb * decay).astype(BF16), x_j)
                upd_j = _dot((b_t * jnp.exp(tot - a_row)).astype(BF16), x_j)
                y = y_j if y is None else y + y_j
                upd = upd_j if upd is None else upd + upd_j
                a_bc.append(a_b)
                tots.append(tot)
            state = st_ref[pair]
            y = y + _dot(c_g, state.astype(BF16)) * jnp.exp(jnp.where(lo_half, a_bc[0], a_bc[1]))
            st_ref[pair] = state * jnp.exp(jnp.where(lo_half, tots[0], tots[1])) + upd
            if rev:
                y = y + acc_ref[0, rows, sl].astype(F32)
            else:
                y = y + dsk_ref[:, sl] * xs[:, sl]
            o_ref[0, rows, sl] = y.astype(o_ref.dtype)


def _mixer_scan(xbc, gates, q, k, v, dtb_row, a_row, ib_row, fb_row, ssd_extra, ml_acc, *, rev, ctx_len):
    bsz, s, _ = xbc.shape
    n = ML_CHUNK
    assert n == SCAN_CHUNK * SSD_BLOCK_CHUNKS and ctx_len % n == 0 and s % n == 0
    assert SCAN_CHUNK == LANES and 2 * SSD_HEAD_DIM == LANES and (SSD_HEADS // SSD_GROUPS) % 2 == 0
    n_all = s // n
    n_ctx = ctx_len // n
    w = SSD_HEADS * SSD_HEAD_DIM
    bcw = 2 * SSD_GROUPS * SSD_STATE
    qw, vw = ML_HEADS * ML_DK, ML_HEADS * ML_DV
    off = GATE_DT + (SSD_HEADS if rev else 0)
    cidx = functools.partial(_scan_chunk_index, rev=rev, n_ctx=n_ctx, n_all=n_all)
    tok = lambda b, c: (b, cidx(c), 0)
    const = lambda b, c: (0, 0)
    expand = jnp.asarray(np.arange(LANES)[:, None] == off + np.arange(w)[None, :] // SSD_HEAD_DIM, BF16)
    in_specs = [pl.BlockSpec((1, n, w), tok),
                pl.BlockSpec((1, n, bcw), lambda b, c: (b, cidx(c), w // bcw)),
                pl.BlockSpec((1, n, LANES), tok),
                pl.BlockSpec((1, LANES), const), pl.BlockSpec((1, LANES), const),
                pl.BlockSpec((LANES, w), const),
                pl.BlockSpec((1, n, w), tok) if rev else pl.BlockSpec((1, w), const),
                pl.BlockSpec((1, n, qw), tok), pl.BlockSpec((1, n, qw), tok), pl.BlockSpec((1, n, vw), tok),
                pl.BlockSpec((1, LANES), const), pl.BlockSpec((1, LANES), const)]
    args = [xbc, xbc, gates, dtb_row, a_row, expand, ssd_extra, q, k, v, ib_row, fb_row]
    if rev:
        in_specs.append(pl.BlockSpec((1, n, vw), tok))
        args.append(ml_acc)
    return pl.pallas_call(
        functools.partial(_mixer_scan_kernel, rev=rev, off=off, d=1 if rev else 0),
        out_shape=[jax.ShapeDtypeStruct((bsz, s, w), BF16), jax.ShapeDtypeStruct((bsz, s, vw), F32)],
        grid=(bsz, n_all), in_specs=in_specs,
        out_specs=[pl.BlockSpec((1, n, w), tok), pl.BlockSpec((1, n, vw), tok)],
        scratch_shapes=[pltpu.VMEM((SSD_HEADS // 2, SSD_STATE, LANES), F32),
                        pltpu.VMEM((ML_HEADS, ML_DK, ML_DV), F32),
                        pltpu.VMEM((ML_HEADS, ML_DK, LANES), F32),
                        pltpu.VMEM((ML_HEADS, 8, LANES), F32)],
        compiler_params=_params(2), name="mixer_scan_bwd" if rev else "mixer_scan_fwd",
    )(*args)


def _mlstm_block(refs, rev, d):
    if rev:
        q_ref, k_ref, v_ref, g_ref, ib_ref, fb_ref, acc_ref, o_ref, c_st, n_st, m_st = refs
    else:
        q_ref, k_ref, v_ref, g_ref, ib_ref, fb_ref, o_ref, c_st, n_st, m_st = refs
    n = q_ref.shape[1]
    last = 0 if rev else n - 1
    tri = _tri(n, rev)
    g = g_ref[0]
    log_i = g + ib_ref[...]
    log_f = -_softplus(-(g + fb_ref[...]))
    cs = _cumsum_dot(tri, log_f)
    cs_t = cs.T
    li_t = log_i.T
    for h in range(ML_HEADS):
        ci = GATE_I + ML_HEADS * d + h
        cf = GATE_F + ML_HEADS * d + h
        b_col = cs[:, cf:cf + 1]
        off_row = cs_t[cf:cf + 1, :] - li_t[ci:ci + 1, :]
        tot = cs_t[cf:cf + 1, last:last + 1]
        m_prev = m_st[h, 0:1, 0:1]
        pmax = jnp.max(jnp.where(tri, -off_row, -jnp.inf), axis=-1, keepdims=True)
        u_b = jnp.broadcast_to(-jnp.maximum(pmax, m_prev), (n, LANES))
        b_b = jnp.broadcast_to(b_col, (n, LANES))
        w_inter = jnp.exp(m_prev + u_b)
        a_end = tot - off_row
        m_loc = jnp.max(a_end, axis=-1, keepdims=True)
        w_end = jnp.exp(a_end - m_loc)
        qh = q_ref[0, :, h * ML_DK:(h + 1) * ML_DK]
        kf = k_ref[0, :, h * ML_DK:(h + 1) * ML_DK].astype(F32) * (ML_DK ** -0.5)
        kb = kf.astype(BF16)
        vh = v_ref[0, :, h * ML_DV:(h + 1) * ML_DV]
        s_mat = _dot_nt(qh, kb) * jnp.exp(jnp.where(tri, jnp.tile(u_b, (1, n // LANES)) - off_row, -jnp.inf))
        s_hi, s_lo = _split_bf16(s_mat, 2)
        c_prev = c_st[h]
        n_prev = n_st[h]
        ones = jnp.ones((n, LANES), BF16)
        den = _dot(s_hi, ones) + _dot(s_lo, ones) + _dot(qh, n_prev.astype(BF16)) * w_inter
        inv = 1.0 / jnp.maximum(jnp.abs(den), jnp.exp(u_b - b_b))
        reps = ML_DV // LANES
        out = (_dot(s_hi, vh) * jnp.tile(inv, (1, reps))
               + _dot(qh, c_prev.astype(BF16)) * jnp.tile(w_inter * inv, (1, reps)))
        kw_t = (kf.T * w_end).astype(BF16)
        c_chunk = _dot(kw_t, vh)
        n_chunk = _dot(kw_t, ones)
        m_new = jnp.maximum(tot + m_prev, m_loc)
        a_sc = jnp.exp(tot + m_prev - m_new)
        b_sc = jnp.exp(m_loc - m_new)
        c_st[h] = c_prev * a_sc + c_chunk * b_sc
        n_st[h] = n_prev * a_sc + n_chunk * b_sc
        m_st[h] = jnp.broadcast_to(m_new, m_st.shape[1:])
        if rev:
            out = out + acc_ref[0, :, h * ML_DV:(h + 1) * ML_DV].astype(F32)
        o_ref[0, :, h * ML_DV:(h + 1) * ML_DV] = out.astype(o_ref.dtype)


def _post_kernel(*refs, mode, split_tiles, x_tile_off, n_tiles, n_steps):
    k = pl.program_id(0)
    tile = jnp.minimum(k, n_steps - 1) % n_tiles
    x, refs = _stream_tile(refs, split_tiles, tile + x_tile_off)
    lg_sc = refs[-1]
    refs = refs[:-1]
    cnt_ref = refs[-1]

    @pl.when(k == 0)
    def _():
        lg_sc[...] = jnp.zeros_like(lg_sc)
        cnt_ref[...] = jnp.zeros_like(cnt_ref)

    lg_prev = lg_sc[...]
    if mode == "mix":
        (y_ref, z_ref, hm_ref, og_ref, snw_ref, mnw_ref, wa_ref, wb_ref,
         mod_ref, n2_ref, wr_ref, br_ref, xo_ref, h2_ref, rf_ref, cnt_ref) = refs
        y = _rms(y_ref[0].astype(F32) * _silu(z_ref[0].astype(F32)), snw_ref[...])
        og = og_ref[0].astype(F32)
        o = _dot(y.astype(BF16), wa_ref[...])
        parts = []
        for h in range(ML_HEADS):
            sl = slice(h * ML_DV, (h + 1) * ML_DV)
            parts.append(_rms(hm_ref[0, :, sl], mnw_ref[:, sl]) * jax.nn.sigmoid(og[:, sl]))
        o = o + _dot(jnp.concatenate(parts, axis=-1).astype(BF16), wb_ref[...])
    else:
        (a_ref, wa_ref, mod_ref, n2_ref, wr_ref, br_ref, xo_ref, h2_ref, rf_ref, cnt_ref) = refs
        o = _dot(a_ref[0], wa_ref[...])
    x_new = x + mod_ref[0, 2:3, :] * o
    xo_ref[0] = x_new
    h2 = _rms(x_new, n2_ref[...]) * (1.0 + mod_ref[0, 4:5, :]) + mod_ref[0, 3:4, :]
    h2_ref[0] = h2
    lg_sc[...] = _dot_split(h2, wr_ref) + br_ref[...]
    rf_ref[0] = _route_fields(lg_prev, cnt_ref, jnp.where(k > 0, 1.0, 0.0))


def _post_call(mode, x, acts, rows, mats, mod_l, norm2_w, w_router, b_router, x_tile_off, n_tiles, ctx_tiles, ctx_row):
    tm = ROW_TILE
    x_args, in_specs, split_tiles = _stream_specs(x, tm, x_tile_off)
    bsz, d = x_args[0].shape[0], x_args[0].shape[2]
    s_out = n_tiles * tm
    tok = lambda b, i: (b, i, 0)
    const = lambda b, i: (0, 0)
    mod_idx = lambda b, i: (jnp.where(i + x_tile_off < ctx_tiles, ctx_row, b), 0, 0)
    in_specs += [pl.BlockSpec((1, tm, a.shape[2]), tok) for a in acts]
    in_specs += [pl.BlockSpec(r.shape, const) for r in rows]
    in_specs += [pl.BlockSpec(m.shape, const) for m in mats]
    in_specs += [pl.BlockSpec((1, 6, d), mod_idx), pl.BlockSpec((1, d), const),
                 pl.BlockSpec(w_router.shape, lambda b, i: (0, 0, 0)), pl.BlockSpec((1, LANES), const)]
    out_shape = [jax.ShapeDtypeStruct((bsz, s_out, d), F32),
                 jax.ShapeDtypeStruct((bsz, s_out, d), F32),
                 jax.ShapeDtypeStruct((bsz, s_out, LANES), F32),
                 jax.ShapeDtypeStruct((8, LANES), F32)]
    out_specs = [pl.BlockSpec((1, tm, d), tok), pl.BlockSpec((1, tm, d), tok),
                 pl.BlockSpec((1, tm, LANES), tok), pl.BlockSpec((8, LANES), const)]
    n_steps = bsz * n_tiles

    def flat(index_map, lag=0):
        def at(k):
            kk = jnp.clip(k - lag, 0, n_steps - 1)
            return index_map(kk // n_tiles, kk % n_tiles)
        return at

    in_specs = [pl.BlockSpec(sp.block_shape, flat(sp.index_map)) for sp in in_specs]
    out_specs = [pl.BlockSpec(sp.block_shape, flat(sp.index_map, lag=int(j == 2))) for j, sp in enumerate(out_specs)]
    return pl.pallas_call(
        functools.partial(_post_kernel, mode=mode, split_tiles=split_tiles, x_tile_off=x_tile_off,
                          n_tiles=n_tiles, n_steps=n_steps),
        out_shape=out_shape, grid=(n_steps + 1,), in_specs=in_specs, out_specs=out_specs,
        scratch_shapes=[pltpu.VMEM((tm, LANES), F32)],
        compiler_params=_params(1), name="post_" + mode,
    )(*x_args, *acts, *rows, *mats, mod_l, norm2_w.reshape(1, d), w_router, b_router)


def _router_kernel(f_ref, cnt_ref, route_ref, src_ref, off_ref, inv_ref, *, moe_tile, n_tokens):
    i = pl.program_id(0)
    tm = f_ref.shape[0]

    @pl.when(i == 0)
    def _():
        tiles = jnp.ceil(cnt_ref[...] * (1.0 / moe_tile))
        r = lax.broadcasted_iota(jnp.int32, (LANES, LANES), 0)
        c = lax.broadcasted_iota(jnp.int32, (LANES, LANES), 1)
        earlier = jnp.where(r < c, 1.0, 0.0).astype(BF16)
        off_ref[...] = _dot(tiles.astype(BF16), earlier) * float(moe_tile)

    f = f_ref[...]
    lane = lax.broadcasted_iota(jnp.int32, f.shape, 1).astype(F32)
    off = off_ref[0:1, :]
    pos1 = jnp.sum(jnp.where(lane == f[:, 0:1], off, 0.0), axis=-1, keepdims=True) + f[:, 4:5]
    pos2 = jnp.sum(jnp.where(lane == f[:, 1:2], off, 0.0), axis=-1, keepdims=True) + f[:, 5:6]
    route_ref[...] = jnp.where(lane == 6.0, pos1, jnp.where(lane == 7.0, pos2, f))

    n_blk = inv_ref.shape[0]
    pos_t = jnp.where(lane == 0.0, pos1, jnp.where(lane == 1.0, pos2, 0.0)).T
    blk = lax.broadcasted_iota(jnp.int32, (n_blk, tm), 0).astype(F32)
    tok = (i * tm + lax.broadcasted_iota(jnp.int32, (tm, 1), 0)).astype(F32)
    tok_hi = jnp.floor(tok * (1.0 / LANES))
    tok_lo = tok - tok_hi * LANES
    lhs, rhs = [], []
    for k, pos in enumerate((pos1, pos2)):
        blk_of = jnp.floor(pos_t[k:k + 1, :] * (1.0 / LANES))
        lhs.append(jnp.where(blk == blk_of, 1.0, 0.0).astype(BF16))
        hit = lane == pos - jnp.floor(pos * (1.0 / LANES)) * LANES
        rhs.append(jnp.concatenate([jnp.where(hit, tok_hi, 0.0), jnp.where(hit, tok_lo, 0.0),
                                    jnp.where(hit, 1.0, 0.0)], axis=1).astype(BF16))
    upd = _dot(jnp.concatenate(lhs, axis=1), jnp.concatenate(rhs, axis=0))

    @pl.when(i == 0)
    def _():
        inv_ref[...] = upd

    @pl.when(i > 0)
    def _():
        inv_ref[...] = inv_ref[...] + upd

    @pl.when(i == pl.num_programs(0) - 1)
    def _():
        acc = inv_ref[...]
        n_blk = acc.shape[0]
        slot = (lax.broadcasted_iota(jnp.int32, (n_blk, LANES), 0) * LANES
                + lax.broadcasted_iota(jnp.int32, (n_blk, LANES), 1)).astype(F32)
        spare = slot - n_tokens * jnp.floor((slot + 0.5) * (1.0 / n_tokens))
        src = jnp.where(acc[:, 2 * LANES:] > 0.0, acc[:, :LANES] * LANES + acc[:, LANES:2 * LANES], spare)
        src_ref[...] = src.astype(jnp.int32)


def _route_fields(lg, cnt_ref, live):
    tm = lg.shape[0]
    lane = lax.broadcasted_iota(jnp.int32, lg.shape, 1).astype(F32)
    big = float(LANES)
    is_g = (lane >= ROUTE_G) & (lane < ROUTE_G + MOE_GROUPS)
    lgg = jnp.where(is_g, lg, -jnp.inf)
    g_max = jnp.max(lgg, axis=-1, keepdims=True)
    g_idx = jnp.min(jnp.where(lgg == g_max, lane - ROUTE_G, big), axis=-1, keepdims=True)
    g_prob = 1.0 / jnp.sum(jnp.exp(lgg - g_max), axis=-1, keepdims=True)
    lo = g_idx * MOE_EXPERTS
    le = jnp.where((lane >= lo) & (lane < lo + MOE_EXPERTS), lg, -jnp.inf)
    l1 = jnp.max(le, axis=-1, keepdims=True)
    i1 = jnp.min(jnp.where(le == l1, lane, big), axis=-1, keepdims=True)
    le2 = jnp.where(lane == i1, -jnp.inf, le)
    l2 = jnp.max(le2, axis=-1, keepdims=True)
    i2 = jnp.min(jnp.where(le2 == l2, lane, big), axis=-1, keepdims=True)
    r = jnp.exp(l2 - l1)
    w1 = g_prob / (1.0 + r)
    w2 = w1 * r
    oh1 = jnp.where(lane == i1, 1.0, 0.0)
    oh2 = jnp.where(lane == i2, 1.0, 0.0)
    oh = oh1 + oh2
    row = lax.broadcasted_iota(jnp.int32, (tm, tm), 0)
    col = lax.broadcasted_iota(jnp.int32, (tm, tm), 1)
    before = jnp.where(col < row, 1.0, 0.0).astype(BF16)
    prefix = _dot(before, oh.astype(BF16)) + cnt_ref[0:1, :]
    rank1 = jnp.sum(prefix * oh1, axis=-1, keepdims=True)
    rank2 = jnp.sum(prefix * oh2, axis=-1, keepdims=True)
    cnt_ref[...] = cnt_ref[...] + jnp.sum(oh, axis=0, keepdims=True) * live
    fields = (i1, i2, w1, w2, rank1, rank2)
    out = jnp.zeros_like(lg)
    for j, f in enumerate(fields):
        out = jnp.where(lane == float(j), f, out)
    return out


def _router(fields, counts, n_sorted):
    t = fields.shape[0]
    tm = ROW_TILE
    n_blk = n_sorted // LANES
    fixed = lambda i: (0, 0)
    return pl.pallas_call(
        functools.partial(_router_kernel, moe_tile=MOE_TILE, n_tokens=t),
        out_shape=[jax.ShapeDtypeStruct((t, LANES), F32), jax.ShapeDtypeStruct((n_blk, LANES), jnp.int32)],
        grid=(t // tm,),
        in_specs=[pl.BlockSpec((tm, LANES), lambda i: (i, 0)), pl.BlockSpec((8, LANES), fixed)],
        out_specs=[pl.BlockSpec((tm, LANES), lambda i: (i, 0)), pl.BlockSpec((n_blk, LANES), fixed)],
        scratch_shapes=[pltpu.VMEM((8, LANES), F32), pltpu.VMEM((n_blk, 3 * LANES), F32)],
        compiler_params=_params(1), name="router",
    )(fields, counts)


def _moe_kernel(te_ref, nt_ref, x_ref, wg_ref, wu_ref, wd_ref, o_ref, gate_bf, up_bf, down_bf):
    i = pl.program_id(0)
    valid = i < nt_ref[0]
    fresh = (i == 0) | (te_ref[i] != te_ref[jnp.maximum(i - 1, 0)])

    @pl.when(valid & fresh)
    def _():
        gate_bf[...] = wg_ref[0].astype(BF16)
        up_bf[...] = wu_ref[0].astype(BF16)
        down_bf[...] = wd_ref[0].astype(BF16)

    @pl.when(valid)
    def _():
        x = x_ref[...].astype(BF16)
        act = _silu(_dot(x, gate_bf[...])) * _dot(x, up_bf[...])
        o_ref[...] = _dot(act.astype(BF16), down_bf[...]).astype(o_ref.dtype)

    @pl.when(jnp.logical_not(valid))
    def _():
        o_ref[...] = jnp.zeros_like(o_ref)


def _moe_experts(x_sorted, tile_expert, n_tiles_used, wg, wu, wd):
    tm = MOE_TILE
    rows, d = x_sorted.shape
    ff = wg.shape[2]
    grid_spec = pltpu.PrefetchScalarGridSpec(
        num_scalar_prefetch=2, grid=(rows // tm,),
        in_specs=[pl.BlockSpec((tm, d), lambda i, te, nt: (i, 0)),
                  pl.BlockSpec((1, d, ff), lambda i, te, nt: (te[i], 0, 0)),
                  pl.BlockSpec((1, d, ff), lambda i, te, nt: (te[i], 0, 0)),
                  pl.BlockSpec((1, ff, d), lambda i, te, nt: (te[i], 0, 0))],
        out_specs=pl.BlockSpec((tm, d), lambda i, te, nt: (i, 0)),
        scratch_shapes=[pltpu.VMEM((d, ff), BF16), pltpu.VMEM((d, ff), BF16), pltpu.VMEM((ff, d), BF16)])
    return pl.pallas_call(
        _moe_kernel, out_shape=jax.ShapeDtypeStruct((rows, d), F32), grid_spec=grid_spec,
        compiler_params=_params(1), name="moe_experts",
    )(tile_expert, n_tiles_used, x_sorted, wg, wu, wd)


def _combine_kernel(x_ref, y1_ref, y2_ref, rt_ref, mod_ref, o_ref):
    f = rt_ref[0, :, 2:3] * y1_ref[0] + rt_ref[0, :, 3:4] * y2_ref[0]
    o_ref[0] = x_ref[0] + mod_ref[0, 5:6, :] * f


def _combine(x, y1, y2, route, mod_l, ctx_tiles, ctx_row):
    bsz, s, d = x.shape
    tm = ROW_TILE
    tok = lambda b, i: (b, i, 0)
    mod_idx = lambda b, i: (jnp.where(i < ctx_tiles, ctx_row, b), 0, 0)
    return pl.pallas_call(
        _combine_kernel, out_shape=jax.ShapeDtypeStruct(x.shape, F32), grid=(bsz, s // tm),
        in_specs=[pl.BlockSpec((1, tm, d), tok), pl.BlockSpec((1, tm, d), tok), pl.BlockSpec((1, tm, d), tok),
                  pl.BlockSpec((1, tm, LANES), tok), pl.BlockSpec((1, 6, d), mod_idx)],
        out_specs=pl.BlockSpec((1, tm, d), tok),
        compiler_params=_params(2), name="moe_combine",
    )(x, y1.reshape(x.shape), y2.reshape(x.shape), route.reshape(bsz, s, LANES), mod_l)


def _hier_moe(h2, fields, counts, layer, wg, wu, wd):
    bsz, s, d = h2.shape
    t = bsz * s
    tm = MOE_TILE
    n_tiles = 2 * t // tm + N_EXPERTS
    route, src = _router(fields.reshape(t, LANES), counts, n_tiles * tm)
    tiles_per = (counts[0, :N_EXPERTS].astype(jnp.int32) + tm - 1) // tm
    tile_end = jnp.cumsum(tiles_per)
    tile_ids = jnp.arange(n_tiles, dtype=jnp.int32)
    tile_expert = jnp.minimum(jnp.sum((tile_end[None, :] <= tile_ids[:, None]).astype(jnp.int32), axis=1),
                              N_EXPERTS - 1) + layer * N_EXPERTS
    pos = route[:, 6:8].astype(jnp.int32)
    x_sorted = jnp.take(h2.reshape(t, d), src.reshape(-1), axis=0, mode="clip")
    y_sorted = _moe_experts(x_sorted, tile_expert, tile_end[-1:], wg.reshape(-1, d, wg.shape[-1]),
                            wu.reshape(-1, d, wu.shape[-1]), wd.reshape(-1, wd.shape[-2], d))
    y1 = jnp.take(y_sorted, pos[:, 0], axis=0, mode="clip")
    y2 = jnp.take(y_sorted, pos[:, 1], axis=0, mode="clip")
    return y1, y2, route


def _attn_kernel(q_ref, k_ref, v_ref, bias_ref, qw_ref, kw_ref, o_ref, qn_ref, kn_ref, *, ctx_len, rows):
    lane = lax.broadcasted_iota(jnp.int32, (1, LANES), 1)
    first = lane < NA_HEAD_DIM

    r_head = lax.broadcasted_iota(jnp.int32, (LANES, LANES), 0) // NA_HEAD_DIM
    c_head = lax.broadcasted_iota(jnp.int32, (LANES, LANES), 1) // NA_HEAD_DIM
    same_head = jnp.where(r_head == c_head, 1.0, 0.0).astype(BF16)

    def head_norm(x, w):
        ms = _dot((x * x).astype(BF16), same_head) * (1.0 / NA_HEAD_DIM)
        return x * lax.rsqrt(ms + RMS_EPS) * w

    kn_ref[...] = head_norm(k_ref[0].astype(F32), kw_ref[...]).astype(BF16)
    qn_ref[...] = (head_norm(q_ref[0, ctx_len:, :].astype(F32), qw_ref[...]) * NA_HEAD_DIM ** -0.5).astype(BF16)
    n_groups = rows // ATTN_GROUP_ROWS
    n_q = ATTN_GROUP_ROWS * GRID_W
    n_loc = ATTN_KEY_ROWS * GRID_W
    k_ctx = kn_ref[0:ctx_len, :]
    v_ctx = v_ref[0, 0:ctx_len, :]

    def one_group(g):
        kind = jnp.where(g == 0, 0, jnp.where(g == n_groups - 1, 2, 1))
        kr0 = jnp.clip(g * ATTN_GROUP_ROWS - NA_KH // 2, 0, rows - ATTN_KEY_ROWS)
        q_rows = pl.ds(pl.multiple_of(g * n_q, n_q), n_q)
        q = qn_ref[q_rows, :]
        zero = jnp.zeros_like(q)
        q2 = jnp.concatenate([jnp.where(first, q, zero), jnp.where(first, zero, q)], axis=0)
        k_off = pl.multiple_of(ctx_len + kr0 * GRID_W, GRID_W)
        s_loc = _dot_nt(q2, kn_ref[pl.ds(k_off, n_loc), :])
        s_ctx = _dot_nt(q2, k_ctx)
        p_loc, p_ctx, inv = [], [], []
        for hh in range(2):
            sl = s_loc[hh * n_q:(hh + 1) * n_q] + bias_ref[hh, kind]
            sc = s_ctx[hh * n_q:(hh + 1) * n_q]
            m = jnp.maximum(jnp.max(sl, axis=-1, keepdims=True), jnp.max(sc, axis=-1, keepdims=True))
            el = jnp.exp(sl - m)
            ec = jnp.exp(sc - m)
            inv.append(1.0 / (jnp.sum(el, axis=-1, keepdims=True) + jnp.sum(ec, axis=-1, keepdims=True)))
            p_loc.append(el.astype(BF16))
            p_ctx.append(ec.astype(BF16))
        o = (_dot(jnp.concatenate(p_loc, axis=0), v_ref[0, pl.ds(k_off, n_loc), :])
             + _dot(jnp.concatenate(p_ctx, axis=0), v_ctx))
        o_ref[0, q_rows, :] = jnp.where(first, o[:n_q] * inv[0], o[n_q:] * inv[1]).astype(o_ref.dtype)

    per_trip = min(ATTN_GROUPS_PER_TRIP, n_groups)
    assert n_groups % per_trip == 0

    def body(i, carry):
        for j in range(per_trip):
            one_group(i * per_trip + j)
        return carry

    lax.fori_loop(0, n_groups // per_trip, body, 0)


def _attn_group_layout(rows):
    n_groups = rows // ATTN_GROUP_ROWS
    assert rows % ATTN_GROUP_ROWS == 0 and rows >= ATTN_KEY_ROWS and n_groups >= 2
    u = np.arange(ATTN_GROUP_ROWS)[:, None]
    i = np.arange(ATTN_KEY_ROWS)[None, :]

    def layout(g):
        r = g * ATTN_GROUP_ROWS + u
        r0 = np.clip(r - NA_KH // 2, 0, rows - NA_KH)
        kr = np.clip(g * ATTN_GROUP_ROWS - NA_KH // 2, 0, rows - ATTN_KEY_ROWS) + i
        return (kr >= r0) & (kr < r0 + NA_KH), kr - r + NA_KH - 1

    kinds = [layout(0), layout(1), layout(n_groups - 1)]
    for g in range(1, n_groups - 1):
        valid, d = layout(g)
        assert (valid == kinds[1][0]).all() and (d[valid] == kinds[1][1][valid]).all()
    return np.stack([k[0] for k in kinds]), np.stack([k[1] for k in kinds])


def _bias_windows(rpb, rows):
    qc = np.arange(GRID_W)
    c0 = np.clip(qc - NA_KW // 2, 0, GRID_W - NA_KW)
    kc = np.arange(GRID_W)
    inwin = (kc[None, :] >= c0[:, None]) & (kc[None, :] < c0[:, None] + NA_KW)
    coff = kc[None, :] - qc[:, None] + NA_KW - 1
    pick = (coff[..., None] == np.arange(2 * NA_KW - 1)) & inwin[..., None]
    tab = jnp.einsum("hdo,qko->dhqk", rpb.astype(F32), jnp.asarray(pick, F32), precision=HIGHEST)
    tab = jnp.where(inwin[None, None], tab, NEG)
    valid, d = _attn_group_layout(rows)
    masked = jnp.full(tab.shape[1:], NEG, F32)
    blocks = []
    for kind in range(valid.shape[0]):
        for u in range(ATTN_GROUP_ROWS):
            blocks.append(jnp.concatenate(
                [tab[int(d[kind, u, i])] if valid[kind, u, i] else masked for i in range(ATTN_KEY_ROWS)], axis=-1))
    win = jnp.stack(blocks, axis=1)
    return win.reshape(rpb.shape[0], valid.shape[0], ATTN_GROUP_ROWS * GRID_W, ATTN_KEY_ROWS * GRID_W)


def _neighbourhood_attention(q, k, v, bias, qn_w, kn_w, ctx_len):
    bsz, s, w = q.shape
    seq = s - ctx_len
    rows = seq // GRID_W
    n_pairs = NA_HEADS // 2
    pair = lambda b, p: (b, 0, p)
    row2 = lambda b, p: (0, 0)
    return pl.pallas_call(
        functools.partial(_attn_kernel, ctx_len=ctx_len, rows=rows),
        out_shape=jax.ShapeDtypeStruct((bsz, seq, w), BF16),
        grid=(bsz, n_pairs),
        in_specs=[pl.BlockSpec((1, s, LANES), pair), pl.BlockSpec((1, s, LANES), pair),
                  pl.BlockSpec((1, s, LANES), pair),
                  pl.BlockSpec((2,) + bias.shape[1:], lambda b, p: (p, 0, 0, 0)),
                  pl.BlockSpec((1, LANES), row2), pl.BlockSpec((1, LANES), row2)],
        out_specs=pl.BlockSpec((1, seq, LANES), pair),
        scratch_shapes=[pltpu.VMEM((seq, LANES), BF16), pltpu.VMEM((s, LANES), BF16)],
        compiler_params=_params(2), name="neighbourhood_attention",
    )(q, k, v, bias, jnp.tile(qn_w, 2).reshape(1, LANES), jnp.tile(kn_w, 2).reshape(1, LANES))


def _place_cols(pieces, width=LANES):
    lead = pieces[0][1].shape[:-1]
    out, at = [], 0
    for off, a in pieces:
        if off > at:
            out.append(jnp.zeros(lead + (off - at,), F32))
        out.append(a.astype(F32))
        at = off + a.shape[-1]
    if at < width:
        out.append(jnp.zeros(lead + (width - at,), F32))
    return jnp.concatenate(out, axis=-1)


def _pad_row(pieces, width=LANES):
    return _place_cols(pieces, width).reshape(1, width)


def _router_params(w_group, b_group, w_expert, b_expert):
    w = _place_cols([(0, w_expert), (ROUTE_G, w_group)])
    return _split_weight(w), _pad_row([(0, b_expert), (ROUTE_G, b_group)])


def kernel(x, c, ctx, c_ctx, norm1_w, norm2_w, mod_w, mod_b, ab_w_in, ab_conv_w, ab_conv_b, ssd_a_log, ssd_dt_bias, ssd_d, ssd_norm_w, ml_i_bias, ml_f_bias, ml_norm_w, ab_w_out, na_w_qkv, na_q_norm, na_k_norm, na_rpb, na_w_out, moe_w_group, moe_b_group, moe_w_expert, moe_b_expert, moe_w_gate, moe_w_up, moe_w_down):
    bsz, seq, d = x.shape
    ctx_len = ctx.shape[1]
    depth = mod_w.shape[0]
    assert depth == 2 and ctx_len % ROW_TILE == 0 and seq % ROW_TILE == 0 and bsz < 8
    ctx_tiles = ctx_len // ROW_TILE
    lat_tiles = seq // ROW_TILE
    ctx_row = bsz

    cvec = jnp.concatenate([c, c_ctx[None], jnp.zeros((7 - bsz, d), c.dtype)], axis=0).astype(F32)
    mod = _mod_vectors(cvec, mod_w, mod_b)
    xs = (ctx, x)

    ssd_w = SSD_HEADS * SSD_HEAD_DIM
    xbc_w = ssd_w + 2 * SSD_GROUPS * SSD_STATE
    qk_w, v_w = ML_HEADS * ML_DK, ML_HEADS * ML_DV
    sizes = (ssd_w, xbc_w, 2 * SSD_HEADS, qk_w, qk_w, v_w, v_w, 2 * ML_HEADS, 2 * ML_HEADS)
    w_z, w_xbc, w_dt, w_q, w_k, w_v, w_o, w_i, w_f = jnp.split(ab_w_in[0], np.cumsum(sizes)[:-1].tolist(), axis=1)
    w_gate = _place_cols([(GATE_DT, w_dt), (GATE_I, w_i), (GATE_F, w_f)])
    weights = [w.astype(BF16) for w in (w_z, w_xbc, w_q, w_k, w_v, w_o)] + [_split_weight(w_gate)]
    z, xbc, q, k, v, og, gates = _norm_mod_matmul(xs, norm1_w[0], mod[0], weights, [BF16] * 6 + [F32],
                                                  ctx_tiles, ctx_row)
    xbc = _conv_silu(xbc, ab_conv_w[0], ab_conv_b[0], ctx_len)
    a_neg = -jnp.exp(ssd_a_log[0].astype(F32))
    dsk_row = jnp.repeat(ssd_d[0].astype(F32), SSD_HEAD_DIM).reshape(1, ssd_w)
    y = None
    hm = None
    for dr in range(2):
        rev = dr == 1
        dtb_row = _pad_row([(GATE_DT + dr * SSD_HEADS, ssd_dt_bias[0, dr])])
        a_row = _pad_row([(GATE_DT + dr * SSD_HEADS, a_neg[dr])])
        ib_row = _pad_row([(GATE_I + dr * ML_HEADS, ml_i_bias[0, dr])])
        fb_row = _pad_row([(GATE_F + dr * ML_HEADS, ml_f_bias[0, dr])])
        y, hm = _mixer_scan(xbc, gates, q, k, v, dtb_row, a_row, ib_row, fb_row, y if rev else dsk_row, hm,
                            rev=rev, ctx_len=ctx_len)
    w_r, b_r = _router_params(moe_w_group[0], moe_b_group[0], moe_w_expert[0], moe_b_expert[0])
    w_out = ab_w_out[0].astype(BF16)
    x1, h2, fields, counts = _post_call(
        "mix", xs, [y, z, hm, og],
        [ssd_norm_w[0].reshape(1, ssd_w), ml_norm_w[0].reshape(1, v_w)], [w_out[:ssd_w], w_out[ssd_w:]],
        mod[0], norm2_w[0], w_r, b_r, 0, ctx_tiles + lat_tiles, ctx_tiles, ctx_row)
    moe0 = _hier_moe(h2, fields, counts, 0, moe_w_gate, moe_w_up, moe_w_down)

    w_qkv = na_w_qkv[0].astype(BF16)
    na_w = NA_HEADS * NA_HEAD_DIM
    q, k, v, xs = _norm_mod_matmul(x1, norm1_w[1], mod[1],
                                   [w_qkv[:, :na_w], w_qkv[:, na_w:2 * na_w], w_qkv[:, 2 * na_w:]],
                                   [BF16] * 3, ctx_tiles, ctx_row, pending_moe=(*moe0, mod[0]))
    bias = _bias_windows(na_rpb[0], seq // GRID_W)
    attn = _neighbourhood_attention(q, k, v, bias, na_q_norm[0], na_k_norm[0], ctx_len)
    w_r, b_r = _router_params(moe_w_group[1], moe_b_group[1], moe_w_expert[1], moe_b_expert[1])
    x1, h2, fields, counts = _post_call("attn", xs, [attn], [], [na_w_out[0].astype(BF16)],
                                        mod[1], norm2_w[1], w_r, b_r, ctx_tiles, lat_tiles, ctx_tiles, ctx_row)
    y1, y2, route = _hier_moe(h2, fields, counts, 1, moe_w_gate, moe_w_up, moe_w_down)
    return _combine(x1, y1, y2, route, mod[1], 0, ctx_row)
```

```python
import functools

import numpy as np
import jax
import jax.numpy as jnp
from jax import lax
from jax.experimental import pallas as pl
from jax.experimental.pallas import tpu as pltpu

F32 = jnp.float32
BF16 = jnp.bfloat16
HIGHEST = lax.Precision.HIGHEST

RMS_EPS = 1e-6
GRID_W = 64
SSD_HEADS = 16
SSD_HEAD_DIM = 64
SSD_GROUPS = 2
SSD_STATE = 128
ML_HEADS = 4
ML_DK = 128
ML_DV = 256
NA_HEADS = 16
NA_HEAD_DIM = 64
NA_KH = 8
NA_KW = 16
MOE_GROUPS = 4
MOE_EXPERTS = 8
N_EXPERTS = MOE_GROUPS * MOE_EXPERTS

LANES = 128
ROW_TILE = 256
SCAN_CHUNK = 128
SSD_BLOCK_CHUNKS = 2
ML_CHUNK = 256
MOE_TILE = 256
ATTN_GROUP_ROWS = 4
ATTN_KEY_ROWS = ATTN_GROUP_ROWS + NA_KH - 1
ATTN_GROUPS_PER_TRIP = 16
VMEM_LIMIT = 56 * 1024 * 1024

GATE_DT = 0
GATE_I = 2 * SSD_HEADS
GATE_F = GATE_I + 2 * ML_HEADS
ROUTE_G = N_EXPERTS
NEG = -1e30


def _params(n_axes):
    return pltpu.CompilerParams(dimension_semantics=("arbitrary",) * n_axes,
                                vmem_limit_bytes=VMEM_LIMIT)


def _silu(x):
    return x * jax.nn.sigmoid(x)


def _softplus(x):
    return jnp.maximum(x, 0.0) + jnp.log1p(jnp.exp(-jnp.abs(x)))


def _rms(x, w):
    return x * lax.rsqrt(jnp.mean(x * x, axis=-1, keepdims=True) + RMS_EPS) * w


def _dot(a, b):
    return jnp.dot(a, b, preferred_element_type=F32)


def _dot_nt(a, b):
    return lax.dot_general(a, b, (((1,), (1,)), ((), ())), preferred_element_type=F32)


def _dot_hi(a, b):
    return jnp.dot(a, b, precision=HIGHEST, preferred_element_type=F32)


def _split_bf16(x, terms):
    parts = []
    for _ in range(terms - 1):
        p = x.astype(BF16)
        parts.append(p)
        x = x - p.astype(F32)
    parts.append(x.astype(BF16))
    return parts


def _split_weight(w):
    return jnp.stack(_split_bf16(w.astype(F32), 2))


def _dot_split(a, w2_ref):
    a_hi, a_lo = _split_bf16(a, 2)
    return _dot(a_hi, w2_ref[0]) + _dot(a_lo, w2_ref[0]) + _dot(a_hi, w2_ref[1])


def _cumsum_dot(tri, x):
    tri = jnp.where(tri, 1.0, 0.0).astype(BF16)
    hi, mid, lo = _split_bf16(x, 3)
    return _dot(tri, hi) + _dot(tri, mid) + _dot(tri, lo)


def _mod_kernel(c_ref, w_ref, b_ref, o_ref):
    o_ref[0] = _dot_hi(_silu(c_ref[...]), w_ref[0]) + b_ref[0]


def _mod_vectors(cvec, mod_w, mod_b, tn=1024):
    depth, d, n = mod_w.shape
    rows = cvec.shape[0]
    out = pl.pallas_call(
        _mod_kernel,
        out_shape=jax.ShapeDtypeStruct((depth, rows, n), F32),
        grid=(depth, n // tn),
        in_specs=[pl.BlockSpec((rows, d), lambda l, j: (0, 0)),
                  pl.BlockSpec((1, d, tn), lambda l, j: (l, 0, j)),
                  pl.BlockSpec((1, 1, tn), lambda l, j: (l, 0, j))],
        out_specs=pl.BlockSpec((1, rows, tn), lambda l, j: (l, 0, j)),
        compiler_params=_params(2),
        name="mod_vectors",
    )(cvec, mod_w, mod_b.reshape(depth, 1, n))
    return out.reshape(depth, rows, 6, d)


def _stream_tile(refs, split_tiles, tile):
    if not split_tiles:
        return refs[0][0], refs[1:]
    return jnp.where(tile < split_tiles, refs[0][0], refs[1][0]), refs[2:]


def _stream_specs(xs, tm, offset=0):
    if not isinstance(xs, tuple):
        return [xs], [pl.BlockSpec((1, tm, xs.shape[2]), lambda b, i: (b, i + offset, 0))], 0
    ctx, lat = xs
    split = ctx.shape[1] // tm
    d = ctx.shape[2]
    return ([ctx, lat],
            [pl.BlockSpec((1, tm, d), lambda b, i: (b, jnp.minimum(i + offset, split - 1), 0)),
             pl.BlockSpec((1, tm, d), lambda b, i: (b, jnp.maximum(i + offset - split, 0), 0))], split)


def _nmm_kernel(*refs, n_out, pending_moe, split_tiles):
    x, refs = _stream_tile(refs, split_tiles, pl.program_id(1))
    if pending_moe:
        y1_ref, y2_ref, rt_ref, pmod_ref = refs[:4]
        refs = refs[4:]
        x = x + pmod_ref[0, 5:6, :] * (rt_ref[0, :, 2:3] * y1_ref[0] + rt_ref[0, :, 3:4] * y2_ref[0])
        refs[-1][0] = x
        refs = refs[:-1]
    nw_ref, mod_ref = refs[:2]
    refs = refs[2:]
    w_refs, o_refs = refs[:n_out], refs[n_out:]
    h = _rms(x, nw_ref[...])
    h = h * (1.0 + mod_ref[0, 1:2, :]) + mod_ref[0, 0:1, :]
    hb = h.astype(BF16)
    for w_ref, o_ref in zip(w_refs, o_refs):
        if len(w_ref.shape) == 3:
            o_ref[0] = _dot_split(h, w_ref)
        else:
            o_ref[0] = _dot(hb, w_ref[...]).astype(o_ref.dtype)


def _norm_mod_matmul(xs, norm_w, mod_l, weights, out_dtypes, ctx_tiles, ctx_row, pending_moe=None):
    tm = ROW_TILE
    args, in_specs, split_tiles = _stream_specs(xs, tm)
    bsz, d = args[0].shape[0], args[0].shape[2]
    s = sum(a.shape[1] for a in args)
    tok = lambda b, i: (b, i, 0)
    mod_idx = lambda b, i: (jnp.where(i < ctx_tiles, ctx_row, b), 0, 0)
    if pending_moe is not None:
        y1, y2, route, mod_prev = pending_moe
        args += [y1.reshape(bsz, s, d), y2.reshape(bsz, s, d), route.reshape(bsz, s, LANES), mod_prev]
        in_specs += [pl.BlockSpec((1, tm, d), tok), pl.BlockSpec((1, tm, d), tok),
                     pl.BlockSpec((1, tm, LANES), tok), pl.BlockSpec((1, 6, d), mod_idx)]
    args += [norm_w.reshape(1, d), mod_l, *weights]
    in_specs += [pl.BlockSpec((1, d), lambda b, i: (0, 0)), pl.BlockSpec((1, 6, d), mod_idx)]
    in_specs += [pl.BlockSpec(w.shape, lambda b, i, nd=w.ndim: (0,) * nd) for w in weights]
    out_shape = [jax.ShapeDtypeStruct((bsz, s, w.shape[-1]), dt) for w, dt in zip(weights, out_dtypes)]
    out_specs = [pl.BlockSpec((1, tm, w.shape[-1]), tok) for w in weights]
    if pending_moe is not None:
        out_shape.append(jax.ShapeDtypeStruct((bsz, s, d), F32))
        out_specs.append(pl.BlockSpec((1, tm, d), tok))
    return pl.pallas_call(
        functools.partial(_nmm_kernel, n_out=len(weights), pending_moe=pending_moe is not None,
                          split_tiles=split_tiles),
        out_shape=out_shape, grid=(bsz, s // tm), in_specs=in_specs, out_specs=out_specs,
        compiler_params=_params(2), name="norm_mod_matmul",
    )(*args)


def _conv_kernel(u_ref, w_ref, b_ref, o_ref, *, ctx_len):
    u = u_ref[0].astype(F32)
    s = u.shape[0]
    t = lax.broadcasted_iota(jnp.int32, u.shape, 0)
    prev = jnp.where((t == 0) | (t == ctx_len), 0.0, pltpu.roll(u, 1, axis=0))
    nxt = jnp.where((t == ctx_len - 1) | (t == s - 1), 0.0, pltpu.roll(u, s - 1, axis=0))
    y = prev * w_ref[0:1, :] + u * w_ref[1:2, :] + nxt * w_ref[2:3, :] + b_ref[...]
    o_ref[0] = _silu(y).astype(o_ref.dtype)


def _conv_silu(u, conv_w, conv_b, ctx_len, tc=2 * LANES):
    bsz, s, ch = u.shape
    return pl.pallas_call(
        functools.partial(_conv_kernel, ctx_len=ctx_len),
        out_shape=jax.ShapeDtypeStruct(u.shape, u.dtype),
        grid=(bsz, ch // tc),
        in_specs=[pl.BlockSpec((1, s, tc), lambda b, j: (b, 0, j)),
                  pl.BlockSpec((3, tc), lambda b, j: (0, j)),
                  pl.BlockSpec((1, tc), lambda b, j: (0, j))],
        out_specs=pl.BlockSpec((1, s, tc), lambda b, j: (b, 0, j)),
        compiler_params=_params(2), name="conv_silu",
    )(u, conv_w, conv_b.reshape(1, ch))


def _scan_chunk_index(c, rev, n_ctx, n_all):
    if not rev:
        return c
    return jnp.where(c < n_ctx, n_ctx - 1 - c, n_ctx + n_all - 1 - c)


def _tri(n, rev):
    row = lax.broadcasted_iota(jnp.int32, (n, n), 0)
    col = lax.broadcasted_iota(jnp.int32, (n, n), 1)
    return (col >= row) if rev else (col <= row)


def _mixer_scan_kernel(xs_ref, bc_ref, g_ref, dtb_ref, a_ref, ex_ref, ssd_extra_ref, q_ref, k_ref, v_ref, ib_ref, fb_ref,
                       *refs, rev, off, d):
    y_ref, hm_ref, st_ref, c_st, n_st, m_st = refs[-6:]

    @pl.when(pl.program_id(1) == 0)
    def _():
        for ref in (st_ref, c_st, n_st, m_st):
            ref[...] = jnp.zeros_like(ref)

    _ssd_block((xs_ref, bc_ref, g_ref, dtb_ref, a_ref, ex_ref, ssd_extra_ref, y_ref, st_ref), rev, off)
    ml_acc = refs[:-6]
    _mlstm_block((q_ref, k_ref, v_ref, g_ref, ib_ref, fb_ref, *ml_acc, hm_ref, c_st, n_st, m_st), rev, d)


def _ssd_block(refs, rev, off):
    xs_ref = refs[0]
    n = SCAN_CHUNK
    n_sub = xs_ref.shape[1] // n
    for sub in (range(n_sub - 1, -1, -1) if rev else range(n_sub)):
        _ssd_chunk(refs, slice(sub * n, (sub + 1) * n), rev, off)


def _ssd_chunk(refs, rows, rev, off):
    if rev:
        xs_ref, bc_ref, g_ref, dtb_ref, a_ref, ex_ref, acc_ref, o_ref, st_ref = refs
    else:
        xs_ref, bc_ref, g_ref, dtb_ref, a_ref, ex_ref, dsk_ref, o_ref, st_ref = refs
    n = SCAN_CHUNK
    last = 0 if rev else n - 1
    tri = _tri(n, rev)
    dt = _softplus(g_ref[0, rows, :] + dtb_ref[...])
    log_a = dt * a_ref[...]
    cs = _cumsum_dot(tri, log_a)
    cs_t = cs.T
    dt_hi = dt.astype(BF16)
    dt_lo = (dt - dt_hi.astype(F32)).astype(BF16)
    dt_full = _dot(dt_hi, ex_ref[...]) + _dot(dt_lo, ex_ref[...])
    xs = xs_ref[0, rows, :].astype(F32)
    xdt = (xs * dt_full).astype(BF16)
    lo_half = lax.broadcasted_iota(jnp.int32, (1, LANES), 1) < SSD_HEAD_DIM
    gw = SSD_GROUPS * SSD_STATE
    heads_per_group = SSD_HEADS // SSD_GROUPS
    for g in range(SSD_GROUPS):
        b_g = bc_ref[0, rows, g * SSD_STATE:(g + 1) * SSD_STATE]
        c_g = bc_ref[0, rows, gw + g * SSD_STATE:gw + (g + 1) * SSD_STATE]
        cb = _dot_nt(c_g, b_g)
        b_t = b_g.astype(F32).T
        for e in range(0, heads_per_group, 2):
            h0 = g * heads_per_group + e
            pair = h0 // 2
            sl = slice(pair * LANES, (pair + 1) * LANES)
            x_pair = xdt[:, sl]
            zero = jnp.zeros_like(x_pair)
            y, upd, a_bc, tots = None, None, [], []
            for j in range(2):
                col = off + h0 + j
                x_j = jnp.where(lo_half, x_pair, zero) if j == 0 else jnp.where(lo_half, zero, x_pair)
                a_b = jnp.broadcast_to(cs[:, col:col + 1], (n, LANES))
                a_row = cs_t[col:col + 1, :]
                tot = cs_t[col:col + 1, last:last + 1]
                decay = jnp.exp(jnp.where(tri, a_b - a_row, -jnp.inf))
                y_j = _dot((cb * decay).astype(BF16), x_j)
                upd_j = _dot((b_t * jnp.exp(tot - a_row)).astype(BF16), x_j)
                y = y_j if y is None else y + y_j
                upd = upd_j if upd is None else upd + upd_j
                a_bc.append(a_b)
                tots.append(tot)
            state = st_ref[pair]
            y = y + _dot(c_g, state.astype(BF16)) * jnp.exp(jnp.where(lo_half, a_bc[0], a_bc[1]))
            st_ref[pair] = state * jnp.exp(jnp.where(lo_half, tots[0], tots[1])) + upd
            if rev:
                y = y + acc_ref[0, rows, sl].astype(F32)
            else:
                y = y + dsk_ref[:, sl] * xs[:, sl]
            o_ref[0, rows, sl] = y.astype(o_ref.dtype)


def _mixer_scan(xbc, gates, q, k, v, dtb_row, a_row, ib_row, fb_row, ssd_extra, ml_acc, *, rev, ctx_len):
    bsz, s, _ = xbc.shape
    n = ML_CHUNK
    assert n == SCAN_CHUNK * SSD_BLOCK_CHUNKS and ctx_len % n == 0 and s % n == 0
    assert SCAN_CHUNK == LANES and 2 * SSD_HEAD_DIM == LANES and (SSD_HEADS // SSD_GROUPS) % 2 == 0
    n_all = s // n
    n_ctx = ctx_len // n
    w = SSD_HEADS * SSD_HEAD_DIM
    bcw = 2 * SSD_GROUPS * SSD_STATE
    qw, vw = ML_HEADS * ML_DK, ML_HEADS * ML_DV
    off = GATE_DT + (SSD_HEADS if rev else 0)
    cidx = functools.partial(_scan_chunk_index, rev=rev, n_ctx=n_ctx, n_all=n_all)
    tok = lambda b, c: (b, cidx(c), 0)
    const = lambda b, c: (0, 0)
    expand = jnp.asarray(np.arange(LANES)[:, None] == off + np.arange(w)[None, :] // SSD_HEAD_DIM, BF16)
    in_specs = [pl.BlockSpec((1, n, w), tok),
                pl.BlockSpec((1, n, bcw), lambda b, c: (b, cidx(c), w // bcw)),
                pl.BlockSpec((1, n, LANES), tok),
                pl.BlockSpec((1, LANES), const), pl.BlockSpec((1, LANES), const),
                pl.BlockSpec((LANES, w), const),
                pl.BlockSpec((1, n, w), tok) if rev else pl.BlockSpec((1, w), const),
                pl.BlockSpec((1, n, qw), tok), pl.BlockSpec((1, n, qw), tok), pl.BlockSpec((1, n, vw), tok),
                pl.BlockSpec((1, LANES), const), pl.BlockSpec((1, LANES), const)]
    args = [xbc, xbc, gates, dtb_row, a_row, expand, ssd_extra, q, k, v, ib_row, fb_row]
    if rev:
        in_specs.append(pl.BlockSpec((1, n, vw), tok))
        args.append(ml_acc)
    return pl.pallas_call(
        functools.partial(_mixer_scan_kernel, rev=rev, off=off, d=1 if rev else 0),
        out_shape=[jax.ShapeDtypeStruct((bsz, s, w), BF16), jax.ShapeDtypeStruct((bsz, s, vw), BF16)],
        grid=(bsz, n_all), in_specs=in_specs,
        out_specs=[pl.BlockSpec((1, n, w), tok), pl.BlockSpec((1, n, vw), tok)],
        scratch_shapes=[pltpu.VMEM((SSD_HEADS // 2, SSD_STATE, LANES), F32),
                        pltpu.VMEM((ML_HEADS, ML_DK, ML_DV), F32),
                        pltpu.VMEM((ML_HEADS, ML_DK, LANES), F32),
                        pltpu.VMEM((ML_HEADS, 8, LANES), F32)],
        compiler_params=_params(2), name="mixer_scan_bwd" if rev else "mixer_scan_fwd",
    )(*args)


def _mlstm_block(refs, rev, d):
    if rev:
        q_ref, k_ref, v_ref, g_ref, ib_ref, fb_ref, acc_ref, o_ref, c_st, n_st, m_st = refs
    else:
        q_ref, k_ref, v_ref, g_ref, ib_ref, fb_ref, o_ref, c_st, n_st, m_st = refs
    n = q_ref.shape[1]
    last = 0 if rev else n - 1
    tri = _tri(n, rev)
    g = g_ref[0]
    log_i = g + ib_ref[...]
    log_f = -_softplus(-(g + fb_ref[...]))
    cs = _cumsum_dot(tri, log_f)
    cs_t = cs.T
    li_t = log_i.T
    for h in range(ML_HEADS):
        ci = GATE_I + ML_HEADS * d + h
        cf = GATE_F + ML_HEADS * d + h
        b_col = cs[:, cf:cf + 1]
        off_row = cs_t[cf:cf + 1, :] - li_t[ci:ci + 1, :]
        tot = cs_t[cf:cf + 1, last:last + 1]
        m_prev = m_st[h, 0:1, 0:1]
        pmax = jnp.max(jnp.where(tri, -off_row, -jnp.inf), axis=-1, keepdims=True)
        u_b = jnp.broadcast_to(-jnp.maximum(pmax, m_prev), (n, LANES))
        b_b = jnp.broadcast_to(b_col, (n, LANES))
        w_inter = jnp.exp(m_prev + u_b)
        a_end = tot - off_row
        m_loc = jnp.max(a_end, axis=-1, keepdims=True)
        w_end = jnp.exp(a_end - m_loc)
        qh = q_ref[0, :, h * ML_DK:(h + 1) * ML_DK]
        kf = k_ref[0, :, h * ML_DK:(h + 1) * ML_DK].astype(F32) * (ML_DK ** -0.5)
        kb = kf.astype(BF16)
        vh = v_ref[0, :, h * ML_DV:(h + 1) * ML_DV]
        s_mat = _dot_nt(qh, kb) * jnp.exp(jnp.where(tri, jnp.tile(u_b, (1, n // LANES)) - off_row, -jnp.inf))
        s_hi, s_lo = _split_bf16(s_mat, 2)
        c_prev = c_st[h]
        n_prev = n_st[h]
        ones = jnp.ones((n, LANES), BF16)
        den = _dot(s_hi, ones) + _dot(s_lo, ones) + _dot(qh, n_prev.astype(BF16)) * w_inter
        inv = 1.0 / jnp.maximum(jnp.abs(den), jnp.exp(u_b - b_b))
        reps = ML_DV // LANES
        out = (_dot(s_hi, vh) * jnp.tile(inv, (1, reps))
               + _dot(qh, c_prev.astype(BF16)) * jnp.tile(w_inter * inv, (1, reps)))
        kw_t = (kf.T * w_end).astype(BF16)
        c_chunk = _dot(kw_t, vh)
        n_chunk = _dot(kw_t, ones)
        m_new = jnp.maximum(tot + m_prev, m_loc)
        a_sc = jnp.exp(tot + m_prev - m_new)
        b_sc = jnp.exp(m_loc - m_new)
        c_st[h] = c_prev * a_sc + c_chunk * b_sc
        n_st[h] = n_prev * a_sc + n_chunk * b_sc
        m_st[h] = jnp.broadcast_to(m_new, m_st.shape[1:])
        if rev:
            out = out + acc_ref[0, :, h * ML_DV:(h + 1) * ML_DV].astype(F32)
        o_ref[0, :, h * ML_DV:(h + 1) * ML_DV] = out.astype(o_ref.dtype)


def _post_kernel(*refs, mode, split_tiles, x_tile_off, n_tiles, n_steps):
    k = pl.program_id(0)
    tile = jnp.minimum(k, n_steps - 1) % n_tiles
    x, refs = _stream_tile(refs, split_tiles, tile + x_tile_off)
    lg_sc = refs[-1]
    refs = refs[:-1]
    cnt_ref = refs[-1]

    @pl.when(k == 0)
    def _():
        lg_sc[...] = jnp.zeros_like(lg_sc)
        cnt_ref[...] = jnp.zeros_like(cnt_ref)

    lg_prev = lg_sc[...]
    if mode == "mix":
        (y_ref, z_ref, hm_ref, og_ref, snw_ref, mnw_ref, wa_ref, wb_ref,
         mod_ref, n2_ref, wr_ref, br_ref, xo_ref, h2_ref, rf_ref, cnt_ref) = refs
        y = _rms(y_ref[0].astype(F32) * _silu(z_ref[0].astype(F32)), snw_ref[...])
        og = og_ref[0].astype(F32)
        o = _dot(y.astype(BF16), wa_ref[...])
        parts = []
        for h in range(ML_HEADS):
            sl = slice(h * ML_DV, (h + 1) * ML_DV)
            parts.append(_rms(hm_ref[0, :, sl].astype(F32), mnw_ref[:, sl]) * jax.nn.sigmoid(og[:, sl]))
        o = o + _dot(jnp.concatenate(parts, axis=-1).astype(BF16), wb_ref[...])
    else:
        (a_ref, wa_ref, mod_ref, n2_ref, wr_ref, br_ref, xo_ref, h2_ref, rf_ref, cnt_ref) = refs
        o = _dot(a_ref[0], wa_ref[...])
    x_new = x + mod_ref[0, 2:3, :] * o
    xo_ref[0] = x_new
    h2 = _rms(x_new, n2_ref[...]) * (1.0 + mod_ref[0, 4:5, :]) + mod_ref[0, 3:4, :]
    h2_ref[0] = h2
    lg_sc[...] = _dot_split(h2, wr_ref) + br_ref[...]
    rf_ref[0] = _route_fields(lg_prev, cnt_ref, jnp.where(k > 0, 1.0, 0.0))


def _post_call(mode, x, acts, rows, mats, mod_l, norm2_w, w_router, b_router, x_tile_off, n_tiles, ctx_tiles, ctx_row):
    tm = ROW_TILE
    x_args, in_specs, split_tiles = _stream_specs(x, tm, x_tile_off)
    bsz, d = x_args[0].shape[0], x_args[0].shape[2]
    s_out = n_tiles * tm
    tok = lambda b, i: (b, i, 0)
    const = lambda b, i: (0, 0)
    mod_idx = lambda b, i: (jnp.where(i + x_tile_off < ctx_tiles, ctx_row, b), 0, 0)
    in_specs += [pl.BlockSpec((1, tm, a.shape[2]), tok) for a in acts]
    in_specs += [pl.BlockSpec(r.shape, const) for r in rows]
    in_specs += [pl.BlockSpec(m.shape, const) for m in mats]
    in_specs += [pl.BlockSpec((1, 6, d), mod_idx), pl.BlockSpec((1, d), const),
                 pl.BlockSpec(w_router.shape, lambda b, i: (0, 0, 0)), pl.BlockSpec((1, LANES), const)]
    out_shape = [jax.ShapeDtypeStruct((bsz, s_out, d), F32),
                 jax.ShapeDtypeStruct((bsz, s_out, d), F32),
                 jax.ShapeDtypeStruct((bsz, s_out, LANES), F32),
                 jax.ShapeDtypeStruct((8, LANES), F32)]
    out_specs = [pl.BlockSpec((1, tm, d), tok), pl.BlockSpec((1, tm, d), tok),
                 pl.BlockSpec((1, tm, LANES), tok), pl.BlockSpec((8, LANES), const)]
    n_steps = bsz * n_tiles

    def flat(index_map, lag=0):
        def at(k):
            kk = jnp.clip(k - lag, 0, n_steps - 1)
            return index_map(kk // n_tiles, kk % n_tiles)
        return at

    in_specs = [pl.BlockSpec(sp.block_shape, flat(sp.index_map)) for sp in in_specs]
    out_specs = [pl.BlockSpec(sp.block_shape, flat(sp.index_map, lag=int(j == 2))) for j, sp in enumerate(out_specs)]
    return pl.pallas_call(
        functools.partial(_post_kernel, mode=mode, split_tiles=split_tiles, x_tile_off=x_tile_off,
                          n_tiles=n_tiles, n_steps=n_steps),
        out_shape=out_shape, grid=(n_steps + 1,), in_specs=in_specs, out_specs=out_specs,
        scratch_shapes=[pltpu.VMEM((tm, LANES), F32)],
        compiler_params=_params(1), name="post_" + mode,
    )(*x_args, *acts, *rows, *mats, mod_l, norm2_w.reshape(1, d), w_router, b_router)


def _router_kernel(f_ref, cnt_ref, route_ref, src_ref, off_ref, inv_ref, *, moe_tile, n_tokens):
    i = pl.program_id(0)
    tm = f_ref.shape[0]

    @pl.when(i == 0)
    def _():
        tiles = jnp.ceil(cnt_ref[...] * (1.0 / moe_tile))
        r = lax.broadcasted_iota(jnp.int32, (LANES, LANES), 0)
        c = lax.broadcasted_iota(jnp.int32, (LANES, LANES), 1)
        earlier = jnp.where(r < c, 1.0, 0.0).astype(BF16)
        off_ref[...] = _dot(tiles.astype(BF16), earlier) * float(moe_tile)

    f = f_ref[...]
    lane = lax.broadcasted_iota(jnp.int32, f.shape, 1).astype(F32)
    off = off_ref[0:1, :]
    pos1 = jnp.sum(jnp.where(lane == f[:, 0:1], off, 0.0), axis=-1, keepdims=True) + f[:, 4:5]
    pos2 = jnp.sum(jnp.where(lane == f[:, 1:2], off, 0.0), axis=-1, keepdims=True) + f[:, 5:6]
    route_ref[...] = jnp.where(lane == 6.0, pos1, jnp.where(lane == 7.0, pos2, f))

    n_blk = inv_ref.shape[0]
    pos_t = jnp.where(lane == 0.0, pos1, jnp.where(lane == 1.0, pos2, 0.0)).T
    blk = lax.broadcasted_iota(jnp.int32, (n_blk, tm), 0).astype(F32)
    tok = (i * tm + lax.broadcasted_iota(jnp.int32, (tm, 1), 0)).astype(F32)
    tok_hi = jnp.floor(tok * (1.0 / LANES))
    tok_lo = tok - tok_hi * LANES
    lhs, rhs = [], []
    for k, pos in enumerate((pos1, pos2)):
        blk_of = jnp.floor(pos_t[k:k + 1, :] * (1.0 / LANES))
        lhs.append(jnp.where(blk == blk_of, 1.0, 0.0).astype(BF16))
        hit = lane == pos - jnp.floor(pos * (1.0 / LANES)) * LANES
        rhs.append(jnp.concatenate([jnp.where(hit, tok_hi, 0.0), jnp.where(hit, tok_lo, 0.0),
                                    jnp.where(hit, 1.0, 0.0)], axis=1).astype(BF16))
    upd = _dot(jnp.concatenate(lhs, axis=1), jnp.concatenate(rhs, axis=0))

    @pl.when(i == 0)
    def _():
        inv_ref[...] = upd

    @pl.when(i > 0)
    def _():
        inv_ref[...] = inv_ref[...] + upd

    @pl.when(i == pl.num_programs(0) - 1)
    def _():
        acc = inv_ref[...]
        n_blk = acc.shape[0]
        slot = (lax.broadcasted_iota(jnp.int32, (n_blk, LANES), 0) * LANES
                + lax.broadcasted_iota(jnp.int32, (n_blk, LANES), 1)).astype(F32)
        spare = slot - n_tokens * jnp.floor((slot + 0.5) * (1.0 / n_tokens))
        src = jnp.where(acc[:, 2 * LANES:] > 0.0, acc[:, :LANES] * LANES + acc[:, LANES:2 * LANES], spare)
        src_ref[...] = src.astype(jnp.int32)


def _route_fields(lg, cnt_ref, live):
    tm = lg.shape[0]
    lane = lax.broadcasted_iota(jnp.int32, lg.shape, 1).astype(F32)
    big = float(LANES)
    is_g = (lane >= ROUTE_G) & (lane < ROUTE_G + MOE_GROUPS)
    lgg = jnp.where(is_g, lg, -jnp.inf)
    g_max = jnp.max(lgg, axis=-1, keepdims=True)
    g_idx = jnp.min(jnp.where(lgg == g_max, lane - ROUTE_G, big), axis=-1, keepdims=True)
    g_prob = 1.0 / jnp.sum(jnp.exp(lgg - g_max), axis=-1, keepdims=True)
    lo = g_idx * MOE_EXPERTS
    le = jnp.where((lane >= lo) & (lane < lo + MOE_EXPERTS), lg, -jnp.inf)
    l1 = jnp.max(le, axis=-1, keepdims=True)
    i1 = jnp.min(jnp.where(le == l1, lane, big), axis=-1, keepdims=True)
    le2 = jnp.where(lane == i1, -jnp.inf, le)
    l2 = jnp.max(le2, axis=-1, keepdims=True)
    i2 = jnp.min(jnp.where(le2 == l2, lane, big), axis=-1, keepdims=True)
    r = jnp.exp(l2 - l1)
    w1 = g_prob / (1.0 + r)
    w2 = w1 * r
    oh1 = jnp.where(lane == i1, 1.0, 0.0)
    oh2 = jnp.where(lane == i2, 1.0, 0.0)
    oh = oh1 + oh2
    row = lax.broadcasted_iota(jnp.int32, (tm, tm), 0)
    col = lax.broadcasted_iota(jnp.int32, (tm, tm), 1)
    before = jnp.where(col < row, 1.0, 0.0).astype(BF16)
    prefix = _dot(before, oh.astype(BF16)) + cnt_ref[0:1, :]
    rank1 = jnp.sum(prefix * oh1, axis=-1, keepdims=True)
    rank2 = jnp.sum(prefix * oh2, axis=-1, keepdims=True)
    cnt_ref[...] = cnt_ref[...] + jnp.sum(oh, axis=0, keepdims=True) * live
    fields = (i1, i2, w1, w2, rank1, rank2)
    out = jnp.zeros_like(lg)
    for j, f in enumerate(fields):
        out = jnp.where(lane == float(j), f, out)
    return out


def _router(fields, counts, n_sorted):
    t = fields.shape[0]
    tm = ROW_TILE
    n_blk = n_sorted // LANES
    fixed = lambda i: (0, 0)
    return pl.pallas_call(
        functools.partial(_router_kernel, moe_tile=MOE_TILE, n_tokens=t),
        out_shape=[jax.ShapeDtypeStruct((t, LANES), F32), jax.ShapeDtypeStruct((n_blk, LANES), jnp.int32)],
        grid=(t // tm,),
        in_specs=[pl.BlockSpec((tm, LANES), lambda i: (i, 0)), pl.BlockSpec((8, LANES), fixed)],
        out_specs=[pl.BlockSpec((tm, LANES), lambda i: (i, 0)), pl.BlockSpec((n_blk, LANES), fixed)],
        scratch_shapes=[pltpu.VMEM((8, LANES), F32), pltpu.VMEM((n_blk, 3 * LANES), F32)],
        compiler_params=_params(1), name="router",
    )(fields, counts)


def _moe_kernel(te_ref, nt_ref, x_ref, wg_ref, wu_ref, wd_ref, o_ref, gate_bf, up_bf, down_bf):
    i = pl.program_id(0)
    valid = i < nt_ref[0]
    fresh = (i == 0) | (te_ref[i] != te_ref[jnp.maximum(i - 1, 0)])

    @pl.when(valid & fresh)
    def _():
        gate_bf[...] = wg_ref[0].astype(BF16)
        up_bf[...] = wu_ref[0].astype(BF16)
        down_bf[...] = wd_ref[0].astype(BF16)

    @pl.when(valid)
    def _():
        x = x_ref[...].astype(BF16)
        act = _silu(_dot(x, gate_bf[...])) * _dot(x, up_bf[...])
        o_ref[...] = _dot(act.astype(BF16), down_bf[...]).astype(o_ref.dtype)

    @pl.when(jnp.logical_not(valid))
    def _():
        o_ref[...] = jnp.zeros_like(o_ref)


def _moe_experts(x_sorted, tile_expert, n_tiles_used, wg, wu, wd):
    tm = MOE_TILE
    rows, d = x_sorted.shape
    ff = wg.shape[2]
    grid_spec = pltpu.PrefetchScalarGridSpec(
        num_scalar_prefetch=2, grid=(rows // tm,),
        in_specs=[pl.BlockSpec((tm, d), lambda i, te, nt: (i, 0)),
                  pl.BlockSpec((1, d, ff), lambda i, te, nt: (te[i], 0, 0)),
                  pl.BlockSpec((1, d, ff), lambda i, te, nt: (te[i], 0, 0)),
                  pl.BlockSpec((1, ff, d), lambda i, te, nt: (te[i], 0, 0))],
        out_specs=pl.BlockSpec((tm, d), lambda i, te, nt: (i, 0)),
        scratch_shapes=[pltpu.VMEM((d, ff), BF16), pltpu.VMEM((d, ff), BF16), pltpu.VMEM((ff, d), BF16)])
    return pl.pallas_call(
        _moe_kernel, out_shape=jax.ShapeDtypeStruct((rows, d), F32), grid_spec=grid_spec,
        compiler_params=_params(1), name="moe_experts",
    )(tile_expert, n_tiles_used, x_sorted, wg, wu, wd)


def _combine_kernel(x_ref, y1_ref, y2_ref, rt_ref, mod_ref, o_ref):
    f = rt_ref[0, :, 2:3] * y1_ref[0] + rt_ref[0, :, 3:4] * y2_ref[0]
    o_ref[0] = x_ref[0] + mod_ref[0, 5:6, :] * f


def _combine(x, y1, y2, route, mod_l, ctx_tiles, ctx_row):
    bsz, s, d = x.shape
    tm = ROW_TILE
    tok = lambda b, i: (b, i, 0)
    mod_idx = lambda b, i: (jnp.where(i < ctx_tiles, ctx_row, b), 0, 0)
    return pl.pallas_call(
        _combine_kernel, out_shape=jax.ShapeDtypeStruct(x.shape, F32), grid=(bsz, s // tm),
        in_specs=[pl.BlockSpec((1, tm, d), tok), pl.BlockSpec((1, tm, d), tok), pl.BlockSpec((1, tm, d), tok),
                  pl.BlockSpec((1, tm, LANES), tok), pl.BlockSpec((1, 6, d), mod_idx)],
        out_specs=pl.BlockSpec((1, tm, d), tok),
        compiler_params=_params(2), name="moe_combine",
    )(x, y1.reshape(x.shape), y2.reshape(x.shape), route.reshape(bsz, s, LANES), mod_l)


def _hier_moe(h2, fields, counts, layer, wg, wu, wd):
    bsz, s, d = h2.shape
    t = bsz * s
    tm = MOE_TILE
    n_tiles = 2 * t // tm + N_EXPERTS
    route, src = _router(fields.reshape(t, LANES), counts, n_tiles * tm)
    tiles_per = (counts[0, :N_EXPERTS].astype(jnp.int32) + tm - 1) // tm
    tile_end = jnp.cumsum(tiles_per)
    tile_ids = jnp.arange(n_tiles, dtype=jnp.int32)
    tile_expert = jnp.minimum(jnp.sum((tile_end[None, :] <= tile_ids[:, None]).astype(jnp.int32), axis=1),
                              N_EXPERTS - 1) + layer * N_EXPERTS
    pos = route[:, 6:8].astype(jnp.int32)
    x_sorted = jnp.take(h2.reshape(t, d), src.reshape(-1), axis=0, mode="clip")
    y_sorted = _moe_experts(x_sorted, tile_expert, tile_end[-1:], wg.reshape(-1, d, wg.shape[-1]),
                            wu.reshape(-1, d, wu.shape[-1]), wd.reshape(-1, wd.shape[-2], d))
    y1 = jnp.take(y_sorted, pos[:, 0], axis=0, mode="clip")
    y2 = jnp.take(y_sorted, pos[:, 1], axis=0, mode="clip")
    return y1, y2, route


def _attn_kernel(q_ref, k_ref, v_ref, bias_ref, qw_ref, kw_ref, o_ref, qn_ref, kn_ref, *, ctx_len, rows):
    lane = lax.broadcasted_iota(jnp.int32, (1, LANES), 1)
    first = lane < NA_HEAD_DIM

    r_head = lax.broadcasted_iota(jnp.int32, (LANES, LANES), 0) // NA_HEAD_DIM
    c_head = lax.broadcasted_iota(jnp.int32, (LANES, LANES), 1) // NA_HEAD_DIM
    same_head = jnp.where(r_head == c_head, 1.0, 0.0).astype(BF16)

    def head_norm(x, w):
        ms = _dot((x * x).astype(BF16), same_head) * (1.0 / NA_HEAD_DIM)
        return x * lax.rsqrt(ms + RMS_EPS) * w

    kn_ref[...] = head_norm(k_ref[0].astype(F32), kw_ref[...]).astype(BF16)
    qn_ref[...] = (head_norm(q_ref[0, ctx_len:, :].astype(F32), qw_ref[...]) * NA_HEAD_DIM ** -0.5).astype(BF16)
    n_groups = rows // ATTN_GROUP_ROWS
    n_q = ATTN_GROUP_ROWS * GRID_W
    n_loc = ATTN_KEY_ROWS * GRID_W
    k_ctx = kn_ref[0:ctx_len, :]
    v_ctx = v_ref[0, 0:ctx_len, :]

    def one_group(g):
        kind = jnp.where(g == 0, 0, jnp.where(g == n_groups - 1, 2, 1))
        kr0 = jnp.clip(g * ATTN_GROUP_ROWS - NA_KH // 2, 0, rows - ATTN_KEY_ROWS)
        q_rows = pl.ds(pl.multiple_of(g * n_q, n_q), n_q)
        q = qn_ref[q_rows, :]
        zero = jnp.zeros_like(q)
        q2 = jnp.concatenate([jnp.where(first, q, zero), jnp.where(first, zero, q)], axis=0)
        k_off = pl.multiple_of(ctx_len + kr0 * GRID_W, GRID_W)
        s_loc = _dot_nt(q2, kn_ref[pl.ds(k_off, n_loc), :])
        s_ctx = _dot_nt(q2, k_ctx)
        p_loc, p_ctx, inv = [], [], []
        for hh in range(2):
            sl = s_loc[hh * n_q:(hh + 1) * n_q] + bias_ref[hh, kind]
            sc = s_ctx[hh * n_q:(hh + 1) * n_q]
            m = jnp.maximum(jnp.max(sl, axis=-1, keepdims=True), jnp.max(sc, axis=-1, keepdims=True))
            el = jnp.exp(sl - m)
            ec = jnp.exp(sc - m)
            inv.append(1.0 / (jnp.sum(el, axis=-1, keepdims=True) + jnp.sum(ec, axis=-1, keepdims=True)))
            p_loc.append(el.astype(BF16))
            p_ctx.append(ec.astype(BF16))
        o = (_dot(jnp.concatenate(p_loc, axis=0), v_ref[0, pl.ds(k_off, n_loc), :])
             + _dot(jnp.concatenate(p_ctx, axis=0), v_ctx))
        o_ref[0, q_rows, :] = jnp.where(first, o[:n_q] * inv[0], o[n_q:] * inv[1]).astype(o_ref.dtype)

    per_trip = min(ATTN_GROUPS_PER_TRIP, n_groups)
    assert n_groups % per_trip == 0

    def body(i, carry):
        for j in range(per_trip):
            one_group(i * per_trip + j)
        return carry

    lax.fori_loop(0, n_groups // per_trip, body, 0)


def _attn_group_layout(rows):
    n_groups = rows // ATTN_GROUP_ROWS
    assert rows % ATTN_GROUP_ROWS == 0 and rows >= ATTN_KEY_ROWS and n_groups >= 2
    u = np.arange(ATTN_GROUP_ROWS)[:, None]
    i = np.arange(ATTN_KEY_ROWS)[None, :]

    def layout(g):
        r = g * ATTN_GROUP_ROWS + u
        r0 = np.clip(r - NA_KH // 2, 0, rows - NA_KH)
        kr = np.clip(g * ATTN_GROUP_ROWS - NA_KH // 2, 0, rows - ATTN_KEY_ROWS) + i
        return (kr >= r0) & (kr < r0 + NA_KH), kr - r + NA_KH - 1

    kinds = [layout(0), layout(1), layout(n_groups - 1)]
    for g in range(1, n_groups - 1):
        valid, d = layout(g)
        assert (valid == kinds[1][0]).all() and (d[valid] == kinds[1][1][valid]).all()
    return np.stack([k[0] for k in kinds]), np.stack([k[1] for k in kinds])


def _bias_windows(rpb, rows):
    qc = np.arange(GRID_W)
    c0 = np.clip(qc - NA_KW // 2, 0, GRID_W - NA_KW)
    kc = np.arange(GRID_W)
    inwin = (kc[None, :] >= c0[:, None]) & (kc[None, :] < c0[:, None] + NA_KW)
    coff = kc[None, :] - qc[:, None] + NA_KW - 1
    pick = (coff[..., None] == np.arange(2 * NA_KW - 1)) & inwin[..., None]
    tab = jnp.einsum("hdo,qko->dhqk", rpb.astype(F32), jnp.asarray(pick, F32), precision=HIGHEST)
    tab = jnp.where(inwin[None, None], tab, NEG)
    valid, d = _attn_group_layout(rows)
    masked = jnp.full(tab.shape[1:], NEG, F32)
    blocks = []
    for kind in range(valid.shape[0]):
        for u in range(ATTN_GROUP_ROWS):
            blocks.append(jnp.concatenate(
                [tab[int(d[kind, u, i])] if valid[kind, u, i] else masked for i in range(ATTN_KEY_ROWS)], axis=-1))
    win = jnp.stack(blocks, axis=1)
    return win.reshape(rpb.shape[0], valid.shape[0], ATTN_GROUP_ROWS * GRID_W, ATTN_KEY_ROWS * GRID_W)


def _neighbourhood_attention(q, k, v, bias, qn_w, kn_w, ctx_len):
    bsz, s, w = q.shape
    seq = s - ctx_len
    rows = seq // GRID_W
    n_pairs = NA_HEADS // 2
    pair = lambda b, p: (b, 0, p)
    row2 = lambda b, p: (0, 0)
    return pl.pallas_call(
        functools.partial(_attn_kernel, ctx_len=ctx_len, rows=rows),
        out_shape=jax.ShapeDtypeStruct((bsz, seq, w), BF16),
        grid=(bsz, n_pairs),
        in_specs=[pl.BlockSpec((1, s, LANES), pair), pl.BlockSpec((1, s, LANES), pair),
                  pl.BlockSpec((1, s, LANES), pair),
                  pl.BlockSpec((2,) + bias.shape[1:], lambda b, p: (p, 0, 0, 0)),
                  pl.BlockSpec((1, LANES), row2), pl.BlockSpec((1, LANES), row2)],
        out_specs=pl.BlockSpec((1, seq, LANES), pair),
        scratch_shapes=[pltpu.VMEM((seq, LANES), BF16), pltpu.VMEM((s, LANES), BF16)],
        compiler_params=_params(2), name="neighbourhood_attention",
    )(q, k, v, bias, jnp.tile(qn_w, 2).reshape(1, LANES), jnp.tile(kn_w, 2).reshape(1, LANES))


def _place_cols(pieces, width=LANES):
    lead = pieces[0][1].shape[:-1]
    out, at = [], 0
    for off, a in pieces:
        if off > at:
            out.append(jnp.zeros(lead + (off - at,), F32))
        out.append(a.astype(F32))
        at = off + a.shape[-1]
    if at < width:
        out.append(jnp.zeros(lead + (width - at,), F32))
    return jnp.concatenate(out, axis=-1)


def _pad_row(pieces, width=LANES):
    return _place_cols(pieces, width).reshape(1, width)


def _router_params(w_group, b_group, w_expert, b_expert):
    w = _place_cols([(0, w_expert), (ROUTE_G, w_group)])
    return _split_weight(w), _pad_row([(0, b_expert), (ROUTE_G, b_group)])


def kernel(x, c, ctx, c_ctx, norm1_w, norm2_w, mod_w, mod_b, ab_w_in, ab_conv_w, ab_conv_b, ssd_a_log, ssd_dt_bias, ssd_d, ssd_norm_w, ml_i_bias, ml_f_bias, ml_norm_w, ab_w_out, na_w_qkv, na_q_norm, na_k_norm, na_rpb, na_w_out, moe_w_group, moe_b_group, moe_w_expert, moe_b_expert, moe_w_gate, moe_w_up, moe_w_down):
    bsz, seq, d = x.shape
    ctx_len = ctx.shape[1]
    depth = mod_w.shape[0]
    assert depth == 2 and ctx_len % ROW_TILE == 0 and seq % ROW_TILE == 0 and bsz < 8
    ctx_tiles = ctx_len // ROW_TILE
    lat_tiles = seq // ROW_TILE
    ctx_row = bsz

    cvec = jnp.concatenate([c, c_ctx[None], jnp.zeros((7 - bsz, d), c.dtype)], axis=0).astype(F32)
    mod = _mod_vectors(cvec, mod_w, mod_b)
    xs = (ctx, x)

    ssd_w = SSD_HEADS * SSD_HEAD_DIM
    xbc_w = ssd_w + 2 * SSD_GROUPS * SSD_STATE
    qk_w, v_w = ML_HEADS * ML_DK, ML_HEADS * ML_DV
    sizes = (ssd_w, xbc_w, 2 * SSD_HEADS, qk_w, qk_w, v_w, v_w, 2 * ML_HEADS, 2 * ML_HEADS)
    w_z, w_xbc, w_dt, w_q, w_k, w_v, w_o, w_i, w_f = jnp.split(ab_w_in[0], np.cumsum(sizes)[:-1].tolist(), axis=1)
    w_gate = _place_cols([(GATE_DT, w_dt), (GATE_I, w_i), (GATE_F, w_f)])
    weights = [w.astype(BF16) for w in (w_z, w_xbc, w_q, w_k, w_v, w_o)] + [_split_weight(w_gate)]
    z, xbc, q, k, v, og, gates = _norm_mod_matmul(xs, norm1_w[0], mod[0], weights, [BF16] * 6 + [F32],
                                                  ctx_tiles, ctx_row)
    xbc = _conv_silu(xbc, ab_conv_w[0], ab_conv_b[0], ctx_len)
    a_neg = -jnp.exp(ssd_a_log[0].astype(F32))
    dsk_row = jnp.repeat(ssd_d[0].astype(F32), SSD_HEAD_DIM).reshape(1, ssd_w)
    y = None
    hm = None
    for dr in range(2):
        rev = dr == 1
        dtb_row = _pad_row([(GATE_DT + dr * SSD_HEADS, ssd_dt_bias[0, dr])])
        a_row = _pad_row([(GATE_DT + dr * SSD_HEADS, a_neg[dr])])
        ib_row = _pad_row([(GATE_I + dr * ML_HEADS, ml_i_bias[0, dr])])
        fb_row = _pad_row([(GATE_F + dr * ML_HEADS, ml_f_bias[0, dr])])
        y, hm = _mixer_scan(xbc, gates, q, k, v, dtb_row, a_row, ib_row, fb_row, y if rev else dsk_row, hm,
                            rev=rev, ctx_len=ctx_len)
    w_r, b_r = _router_params(moe_w_group[0], moe_b_group[0], moe_w_expert[0], moe_b_expert[0])
    w_out = ab_w_out[0].astype(BF16)
    x1, h2, fields, counts = _post_call(
        "mix", xs, [y, z, hm, og],
        [ssd_norm_w[0].reshape(1, ssd_w), ml_norm_w[0].reshape(1, v_w)], [w_out[:ssd_w], w_out[ssd_w:]],
        mod[0], norm2_w[0], w_r, b_r, 0, ctx_tiles + lat_tiles, ctx_tiles, ctx_row)
    moe0 = _hier_moe(h2, fields, counts, 0, moe_w_gate, moe_w_up, moe_w_down)

    w_qkv = na_w_qkv[0].astype(BF16)
    na_w = NA_HEADS * NA_HEAD_DIM
    q, k, v, xs = _norm_mod_matmul(x1, norm1_w[1], mod[1],
                                   [w_qkv[:, :na_w], w_qkv[:, na_w:2 * na_w], w_qkv[:, 2 * na_w:]],
                                   [BF16] * 3, ctx_tiles, ctx_row, pending_moe=(*moe0, mod[0]))
    bias = _bias_windows(na_rpb[0], seq // GRID_W)
    attn = _neighbourhood_attention(q, k, v, bias, na_q_norm[0], na_k_norm[0], ctx_len)
    w_r, b_r = _router_params(moe_w_group[1], moe_b_group[1], moe_w_expert[1], moe_b_expert[1])
    x1, h2, fields, counts = _post_call("attn", xs, [attn], [], [na_w_out[0].astype(BF16)],
                                        mod[1], norm2_w[1], w_r, b_r, ctx_tiles, lat_tiles, ctx_tiles, ctx_row)
    y1, y2, route = _hier_moe(h2, fields, counts, 1, moe_w_gate, moe_w_up, moe_w_down)
    return _combine(x1, y1, y2, route, mod[1], 0, ctx_row)
```
